```python
import jax, jax.numpy as jnp
from jax import lax
import numpy as np

D_MODEL = 1024
BATCH = 4
SEQ = 4096
DEPTH = 4

HEAD_DIM = 64
N_FOX = 6
N_DSA = 5
N_SB = 5
N_IDX_HEADS = 8
IDX_DIM = 64
TOPK_MAX = 256
N_MEM = 256
N_CA_HEADS = 4
CA_HEAD_DIM = 128
D_FF = 2816
ROPE_THETA = 10000.0
Q_BLOCK = 128
NORM_EPS = 1e-6
N_BRANCH = 3
HALF_STEP = 0.5
FGATE_BIAS_OFFSET = 3.0

FOX_W = N_FOX * HEAD_DIM
DSA_W = N_DSA * HEAD_DIM
SB_W = N_SB * HEAD_DIM
CA_W = N_CA_HEADS * CA_HEAD_DIM
IN_SPLITS = (3 * FOX_W, N_FOX, 3 * DSA_W, N_IDX_HEADS * IDX_DIM, IDX_DIM, N_IDX_HEADS, 3 * SB_W, N_BRANCH * D_MODEL)
D_IN = 3 * FOX_W + N_FOX + 3 * DSA_W + N_IDX_HEADS * IDX_DIM + IDX_DIM + N_IDX_HEADS + 3 * SB_W + N_BRANCH * D_MODEL

kernel_name = 'hybrid_fox_dsa_stickbreak_macaron_trunk'


def rms_norm(x, g):
    xf = x.astype(jnp.float32)
    y = xf * lax.rsqrt(jnp.mean(xf * xf, axis=-1, keepdims=True) + NORM_EPS) * g.astype(jnp.float32)
    return y.astype(x.dtype)


def apply_rope(x, positions):
    d = x.shape[-1]
    half = d // 2
    inv_freq = jnp.power(ROPE_THETA, -jnp.arange(half, dtype=jnp.float32) * (2.0 / d))
    ang = positions.astype(jnp.float32)[..., None] * inv_freq
    cos = jnp.cos(ang)[:, :, None, :]
    sin = jnp.sin(ang)[:, :, None, :]
    xf = x.astype(jnp.float32)
    x1, x2 = xf[..., :half], xf[..., half:]
    return jnp.concatenate([x1 * cos - x2 * sin, x2 * cos + x1 * sin], axis=-1).astype(x.dtype)


def swiglu(h, w_gu, w_down):
    g, u = jnp.split(h @ w_gu, 2, axis=-1)
    return (jax.nn.silu(g) * u) @ w_down


def sweep_query_blocks(block_fn, seq_len):
    n_blocks = seq_len // Q_BLOCK
    out = lax.map(block_fn, jnp.arange(n_blocks, dtype=jnp.int32) * Q_BLOCK)
    out = jnp.moveaxis(out, 0, 1)
    return out.reshape((out.shape[0], seq_len) + out.shape[3:])


def fox_attention(q, k, v, log_f):
    seq_len, d = q.shape[1], q.shape[-1]
    scale = d ** -0.5
    c = jnp.cumsum(log_f, axis=1).transpose(0, 2, 1)
    key_pos = jnp.arange(seq_len)

    def block(start):
        qb = lax.dynamic_slice_in_dim(q, start, Q_BLOCK, axis=1)
        cb = lax.dynamic_slice_in_dim(c, start, Q_BLOCK, axis=2)
        logits = jnp.einsum('bqhd,bkhd->bhqk', qb, k).astype(jnp.float32) * scale
        logits = logits + cb[..., :, None] - c[:, :, None, :]
        qpos = start + jnp.arange(Q_BLOCK)
        causal = key_pos[None, :] <= qpos[:, None]
        p = jax.nn.softmax(jnp.where(causal, logits, -jnp.inf), axis=-1)
        return jnp.einsum('bhqk,bkhd->bqhd', p.astype(v.dtype), v)

    return sweep_query_blocks(block, seq_len)


def stick_breaking_attention(q, k, v):
    seq_len, d = q.shape[1], q.shape[-1]
    scale = d ** -0.5
    key_pos = jnp.arange(seq_len)

    def block(start):
        qb = lax.dynamic_slice_in_dim(q, start, Q_BLOCK, axis=1)
        z = jnp.einsum('bqhd,bkhd->bhqk', qb, k).astype(jnp.float32) * scale
        qpos = start + jnp.arange(Q_BLOCK)
        strict = key_pos[None, :] < qpos[:, None]
        log_beta = jax.nn.log_sigmoid(z)
        log_1m_beta = jnp.where(strict, jax.nn.log_sigmoid(-z), 0.0)
        after = lax.cumsum(log_1m_beta, axis=3, reverse=True) - log_1m_beta
        a = jnp.where(strict, jnp.exp(log_beta + after), 0.0)
        return jnp.einsum('bhqk,bkhd->bqhd', a.astype(v.dtype), v)

    return sweep_query_blocks(block, seq_len)


def dsa_sparse_attention(q, k, v, q_idx, k_idx, w_idx, topk):
    seq_len, d = q.shape[1], q.shape[-1]
    scale = d ** -0.5
    key_pos = jnp.arange(seq_len)

    def block(start):
        qb = lax.dynamic_slice_in_dim(q, start, Q_BLOCK, axis=1)
        qib = lax.dynamic_slice_in_dim(q_idx, start, Q_BLOCK, axis=1)
        wb = lax.dynamic_slice_in_dim(w_idx, start, Q_BLOCK, axis=1)
        head_scores = jax.nn.relu(jnp.einsum('bqhd,bkd->bqhk', qib, k_idx).astype(jnp.float32))
        score = jnp.einsum('bqhk,bqh->bqk', head_scores, wb)
        qpos = start + jnp.arange(Q_BLOCK)
        causal = key_pos[None, :] <= qpos[:, None]
        score = jnp.where(causal[None], score, -jnp.inf)
        _, idx = lax.top_k(score, topk)
        valid = idx <= qpos[None, :, None]
        gather = jax.vmap(lambda arr, ii: arr[ii])
        k_sel = gather(k, idx)
        v_sel = gather(v, idx)
        logits = jnp.einsum('bqhd,bqkhd->bhqk', qb, k_sel).astype(jnp.float32) * scale
        logits = jnp.where(valid[:, None], logits, -jnp.inf)
        p = jax.nn.softmax(logits, axis=-1)
        return jnp.einsum('bhqk,bqkhd->bqhd', p.astype(v_sel.dtype), v_sel)

    return sweep_query_blocks(block, seq_len)


def hybrid_mixer(h, positions, w_in, b_fgate, w_fox_out, w_dsa_out, w_sb_out, w_out):
    b, s, _ = h.shape
    proj = h @ w_in
    offsets = np.cumsum(np.array(IN_SPLITS))[:-1].tolist()
    fox_qkv, fox_f, dsa_qkv, idx_q, idx_k, idx_w, sb_qkv, gates = jnp.split(proj, offsets, axis=-1)

    fq, fk, fv = [t[:, :, 0] for t in jnp.split(fox_qkv.reshape(b, s, 3, N_FOX, HEAD_DIM), 3, axis=2)]
    log_f = jax.nn.log_sigmoid(fox_f.astype(jnp.float32) + b_fgate.astype(jnp.float32))
    o_fox = fox_attention(fq, fk, fv, log_f).reshape(b, s, FOX_W)

    dq, dk, dv = [t[:, :, 0] for t in jnp.split(dsa_qkv.reshape(b, s, 3, N_DSA, HEAD_DIM), 3, axis=2)]
    dq = apply_rope(dq, positions)
    dk = apply_rope(dk, positions)
    q_idx = apply_rope(idx_q.reshape(b, s, N_IDX_HEADS, IDX_DIM), positions)
    k_idx = apply_rope(idx_k.reshape(b, s, 1, IDX_DIM), positions)[:, :, 0]
    w_idx = idx_w.astype(jnp.float32) * (N_IDX_HEADS ** -0.5) * (IDX_DIM ** -0.5)
    topk = min(TOPK_MAX, s // 4)
    o_dsa = dsa_sparse_attention(dq, dk, dv, q_idx, k_idx, w_idx, topk).reshape(b, s, DSA_W)

    sq, sk, sv = [t[:, :, 0] for t in jnp.split(sb_qkv.reshape(b, s, 3, N_SB, HEAD_DIM), 3, axis=2)]
    o_sb = stick_breaking_attention(sq, sk, sv).reshape(b, s, SB_W)

    g = jax.nn.sigmoid(gates.reshape(b, s, N_BRANCH, D_MODEL))
    merged = g[:, :, 0] * (o_fox @ w_fox_out) + g[:, :, 1] * (o_dsa @ w_dsa_out) + g[:, :, 2] * (o_sb @ w_sb_out)
    return merged @ w_out


def memory_cross_attention(h, mem_n, w_q, w_kv, w_o):
    b, s, _ = h.shape
    q = (h @ w_q).reshape(b, s, N_CA_HEADS, CA_HEAD_DIM)
    k, v = jnp.split((mem_n @ w_kv).reshape(b, mem_n.shape[1], 2, N_CA_HEADS, CA_HEAD_DIM), 2, axis=2)
    k, v = k[:, :, 0], v[:, :, 0]
    logits = jnp.einsum('bshd,bmhd->bhsm', q, k).astype(jnp.float32) * (CA_HEAD_DIM ** -0.5)
    p = jax.nn.softmax(logits, axis=-1)
    o = jnp.einsum('bhsm,bmhd->bshd', p.astype(v.dtype), v).reshape(b, s, CA_W)
    return o @ w_o


def setup_inputs(seed: int = 0) -> dict:
    key = jax.random.key(seed)
    ks = jax.random.split(key, 24)
    f32 = jnp.float32

    def dense(k, shape):
        return jax.random.normal(k, shape, f32) * (shape[-2] ** -0.5)

    def gain(k, shape):
        return 1.0 + 0.02 * jax.random.normal(k, shape, f32)

    return {
        'x': jax.random.normal(ks[0], (BATCH, SEQ, D_MODEL), f32),
        'mem': jax.random.normal(ks[1], (BATCH, N_MEM, D_MODEL), f32),
        'positions': jnp.broadcast_to(jnp.arange(SEQ, dtype=jnp.int32), (BATCH, SEQ)),
        'ffn1_norm': gain(ks[2], (DEPTH, D_MODEL)),
        'ffn1_w_gu': dense(ks[3], (DEPTH, D_MODEL, 2 * D_FF)),
        'ffn1_w_down': dense(ks[4], (DEPTH, D_FF, D_MODEL)),
        'mix_norm': gain(ks[5], (DEPTH, D_MODEL)),
        'w_in': dense(ks[6], (DEPTH, D_MODEL, D_IN)),
        'b_fgate': FGATE_BIAS_OFFSET + 0.1 * jax.random.normal(ks[7], (DEPTH, N_FOX), f32),
        'w_fox_out': dense(ks[8], (DEPTH, FOX_W, D_MODEL)),
        'w_dsa_out': dense(ks[9], (DEPTH, DSA_W, D_MODEL)),
        'w_sb_out': dense(ks[10], (DEPTH, SB_W, D_MODEL)),
        'w_out': dense(ks[11], (DEPTH, D_MODEL, D_MODEL)),
        'ca_norm': gain(ks[12], (DEPTH, D_MODEL)),
        'mem_norm': gain(ks[13], (DEPTH, D_MODEL)),
        'ca_w_q': dense(ks[14], (DEPTH, D_MODEL, CA_W)),
        'ca_w_kv': dense(ks[15], (DEPTH, D_MODEL, 2 * CA_W)),
        'ca_w_o': dense(ks[16], (DEPTH, CA_W, D_MODEL)),
        'ffn2_norm': gain(ks[17], (DEPTH, D_MODEL)),
        'ffn2_w_gu': dense(ks[18], (DEPTH, D_MODEL, 2 * D_FF)),
        'ffn2_w_down': dense(ks[19], (DEPTH, D_FF, D_MODEL)),
        'final_norm': gain(ks[20], (D_MODEL,)),
    }


def reference(x, mem, positions, ffn1_norm, ffn1_w_gu, ffn1_w_down, mix_norm, w_in, b_fgate, w_fox_out, w_dsa_out, w_sb_out, w_out, ca_norm, mem_norm, ca_w_q, ca_w_kv, ca_w_o, ffn2_norm, ffn2_w_gu, ffn2_w_down, final_norm):
    for l in range(DEPTH):
        x = x + HALF_STEP * swiglu(rms_norm(x, ffn1_norm[l]), ffn1_w_gu[l], ffn1_w_down[l])
        x = x + hybrid_mixer(rms_norm(x, mix_norm[l]), positions, w_in[l], b_fgate[l], w_fox_out[l], w_dsa_out[l], w_sb_out[l], w_out[l])
        x = x + memory_cross_attention(rms_norm(x, ca_norm[l]), rms_norm(mem, mem_norm[l]), ca_w_q[l], ca_w_kv[l], ca_w_o[l])
        x = x + HALF_STEP * swiglu(rms_norm(x, ffn2_norm[l]), ffn2_w_gu[l], ffn2_w_down[l])
    return rms_norm(x, final_norm)
```

```python
import functools

import jax
import jax.numpy as jnp
import numpy as np
from jax import lax
from jax.experimental import pallas as pl
from jax.experimental.pallas import tpu as pltpu

F32 = jnp.float32
BF16 = jnp.bfloat16
I32 = jnp.int32

D_MODEL = 1024
HEAD_DIM = 64
N_FOX = 6
N_DSA = 5
N_SB = 5
N_IDX_HEADS = 8
IDX_DIM = 64
TOPK_MAX = 256
N_CA_HEADS = 4
CA_HEAD_DIM = 128
D_FF = 2816
ROPE_THETA = 10000.0
NORM_EPS = 1e-6
N_BRANCH = 3
HALF_STEP = 0.5
FOX_W = N_FOX * HEAD_DIM
DSA_W = N_DSA * HEAD_DIM
SB_W = N_SB * HEAD_DIM
CA_W = N_CA_HEADS * CA_HEAD_DIM
IN_SPLITS = (3 * FOX_W, N_FOX, 3 * DSA_W, N_IDX_HEADS * IDX_DIM, IDX_DIM, N_IDX_HEADS, 3 * SB_W,
             N_BRANCH * D_MODEL)

LANES = 128
HEADS_PER_BLOCK = LANES // HEAD_DIM
PAIR_W = 3 * LANES
VMEM_LIMIT = 56 * 2**20

BLK_FOX_Q, BLK_FOX_K, BLK_FOX_V = 0, 3, 6
BLK_DSA_V = 9
BLK_SB_Q, BLK_SB_K, BLK_SB_V = 12, 15, 18
BLK_ROPE_START = 21
BLK_DSA_Q, BLK_DSA_K = 21, 24
BLK_IDX_KA = 27
BLK_IDX_Q = 28
BLK_IDX_KB = 32
N_PROJ_BLOCKS = 33
SMALL_FGATE_LANE = 0
SMALL_IDXW_LANE = 8

NEG_BIG = -1e30
SB_CUTOFF = -110.0
INT_MIN = -2**31


def _params(n_grid):
    return pltpu.CompilerParams(dimension_semantics=("arbitrary",) * n_grid, vmem_limit_bytes=VMEM_LIMIT)


def _resident(shape):
    nd = len(shape)
    return pl.BlockSpec(shape, lambda *_: (0,) * nd, pipeline_mode=pl.Buffered(1))


def _rms(x, g):
    return x * lax.rsqrt(jnp.mean(x * x, axis=-1, keepdims=True) + NORM_EPS) * g


def _dot(a, b):
    return jnp.dot(a, b, preferred_element_type=F32)


def _dot_nt(a, b):
    return lax.dot_general(a, b, (((1,), (1,)), ((), ())), preferred_element_type=F32)


def _split3(x):
    hi = x.astype(BF16)
    r = x - hi.astype(F32)
    mid = r.astype(BF16)
    lo = (r - mid.astype(F32)).astype(BF16)
    return hi, mid, lo


def _log1p_exp_neg_abs(z):
    return jnp.log1p(jnp.exp(-jnp.abs(z)))


def _ffn_chunks():
    out, c = [], 0
    while c < D_FF:
        w = min(512, D_FF - c)
        out.append((c, w))
        c += w
    return tuple(out)


def _ffn_kernel(*refs, final):
    if final:
        x_ref, g_ref, wgu_ref, wd_ref, fn_ref, o_ref, a_ref = refs
    else:
        x_ref, g_ref, wgu_ref, wd_ref, o_ref, a_ref = refs
    x = x_ref[...]
    h = _rms(x, g_ref[...]).astype(BF16)
    for c0, w in _ffn_chunks():
        g = _dot(h, wgu_ref[:, c0:c0 + w])
        u = _dot(h, wgu_ref[:, D_FF + c0:D_FF + c0 + w])
        a_ref[:, c0:c0 + w] = (g * jax.nn.sigmoid(g) * u).astype(BF16)
    y = x + HALF_STEP * _dot(a_ref[...], wd_ref[...])
    if final:
        y = _rms(y, fn_ref[...])
    o_ref[...] = y


def _ffn(x2, gain, w_gu, w_down, final_gain=None):
    m, d = x2.shape
    tm = min(512, m)
    final = final_gain is not None
    in_specs = [pl.BlockSpec((tm, d), lambda i: (i, 0)), _resident((1, d)),
                _resident((d, 2 * D_FF)), _resident((D_FF, d))]
    args = [x2, gain.reshape(1, d), w_gu.astype(BF16), w_down.astype(BF16)]
    if final:
        in_specs.append(_resident((1, d)))
        args.append(final_gain.reshape(1, d))
    return pl.pallas_call(
        functools.partial(_ffn_kernel, final=final),
        grid=(m // tm,),
        in_specs=in_specs,
        out_specs=pl.BlockSpec((tm, d), lambda i: (i, 0)),
        out_shape=jax.ShapeDtypeStruct((m, d), F32),
        scratch_shapes=[pltpu.VMEM((tm, D_FF), BF16)],
        compiler_params=_params(1),
        name="ffn",
    )(*args)


def _mixer_in_kernel(x_ref, g_ref, pos_ref, invf_ref, wb_ref, wg_ref, ws_ref, pb_ref, pg_ref, ps_ref):
    h = _rms(x_ref[...], g_ref[...]).astype(BF16)
    ang = pos_ref[...].astype(F32) * invf_ref[...]
    cos = jnp.cos(ang)
    sin = jnp.sin(ang)
    lane = lax.broadcasted_iota(I32, (1, LANES), 1)
    first_half = (lane % HEAD_DIM) < (HEAD_DIM // 2)
    sin_signed = jnp.where(first_half, -sin, sin)
    chunk = 3 * LANES
    for c0 in range(0, N_PROJ_BLOCKS * LANES, chunk):
        res = _dot(h, wb_ref[:, c0:c0 + chunk])
        if c0 >= BLK_ROPE_START * LANES:
            parts = []
            for b in range(chunk // LANES):
                xb = res[:, b * LANES:(b + 1) * LANES]
                partner = jnp.where(first_half, pltpu.roll(xb, LANES - HEAD_DIM // 2, 1),
                                    pltpu.roll(xb, HEAD_DIM // 2, 1))
                parts.append(xb * cos + partner * sin_signed)
            res = jnp.concatenate(parts, axis=1)
        pb_ref[:, c0:c0 + chunk] = res.astype(BF16)
    gchunk = 4 * LANES
    for c0 in range(0, N_BRANCH * D_MODEL, gchunk):
        pg_ref[:, c0:c0 + gchunk] = _dot(h, wg_ref[:, c0:c0 + gchunk])
    ps_ref[...] = _dot(h, ws_ref[...])


def _mixer_in(x2, gain, pos2, invf, wb, wg, ws):
    m, d = x2.shape
    tm = min(512, m)
    nb, ng = wb.shape[1], wg.shape[1]
    return pl.pallas_call(
        _mixer_in_kernel,
        grid=(m // tm,),
        in_specs=[pl.BlockSpec((tm, d), lambda i: (i, 0)), _resident((1, d)),
                  pl.BlockSpec((tm, 1), lambda i: (i, 0)), _resident((1, LANES)),
                  _resident((d, nb)), _resident((d, ng)), _resident((d, LANES))],
        out_specs=[pl.BlockSpec((tm, nb), lambda i: (i, 0)), pl.BlockSpec((tm, ng), lambda i: (i, 0)),
                   pl.BlockSpec((tm, LANES), lambda i: (i, 0))],
        out_shape=[jax.ShapeDtypeStruct((m, nb), BF16), jax.ShapeDtypeStruct((m, ng), F32),
                   jax.ShapeDtypeStruct((m, LANES), F32)],
        compiler_params=_params(1),
        name="mixer_in",
    )(x2, gain.reshape(1, d), pos2, invf, wb, wg, ws)


def _fox_cumsum_kernel(ps_ref, b_ref, c_ref, carry_ref):
    @pl.when(pl.program_id(1) == 0)
    def _():
        carry_ref[...] = jnp.zeros_like(carry_ref)

    tc = ps_ref.shape[0]
    z = ps_ref[...] + b_ref[...]
    log_f = jnp.minimum(z, 0.0) - _log1p_exp_neg_abs(z)
    row = lax.broadcasted_iota(I32, (tc, tc), 0)
    col = lax.broadcasted_iota(I32, (tc, tc), 1)
    tri = jnp.where(col <= row, 1.0, 0.0).astype(BF16)
    hi, mid, lo = _split3(log_f)
    cs = _dot(tri, hi) + _dot(tri, mid) + _dot(tri, lo) + carry_ref[...]
    c_ref[...] = cs
    carry_ref[...] = cs[tc - 1:tc, :]


def _fox_cumsum(ps, bias_row, batch, seq):
    tc = min(256, seq)
    nj = seq // tc
    return pl.pallas_call(
        _fox_cumsum_kernel,
        grid=(batch, nj),
        in_specs=[pl.BlockSpec((tc, LANES), lambda b, j: (b * nj + j, 0)), _resident((1, LANES))],
        out_specs=pl.BlockSpec((tc, LANES), lambda b, j: (b * nj + j, 0)),
        out_shape=jax.ShapeDtypeStruct(ps.shape, F32),
        scratch_shapes=[pltpu.VMEM((1, LANES), F32)],
        compiler_params=_params(2),
        name="fox_cumsum",
    )(ps, bias_row)


def _head_lane_mask(hh):
    lane = lax.broadcasted_iota(I32, (1, LANES), 1)
    return (lane < HEAD_DIM) if hh == 0 else (lane >= HEAD_DIM)


def _softmax_step(carry, s, vblk):
    m, l, acc = carry
    m_new = jnp.maximum(m, jnp.max(s, axis=1, keepdims=True))
    p = jnp.exp(s - m_new)
    alpha = jnp.exp(m - m_new)
    l = alpha * l + jnp.sum(p, axis=1, keepdims=True)
    acc = alpha * acc + _dot(p.astype(BF16), vblk)
    return m_new, l, acc


def _softmax_init(t):
    return (jnp.full((t, 1), NEG_BIG, F32), jnp.zeros((t, 1), F32), jnp.zeros((t, LANES), F32))


def _fox_kernel(q_ref, k_ref, v_ref, cc_ref, cr_ref, o_ref):
    t = q_ref.shape[0]
    i = pl.program_id(2)
    q = q_ref[...]
    cc = cc_ref[0, 0]
    row = lax.broadcasted_iota(I32, (t, t), 0)
    col = lax.broadcasted_iota(I32, (t, t), 1)
    causal = col <= row
    outs = []
    for hh in range(HEADS_PER_BLOCK):
        qh = jnp.where(_head_lane_mask(hh), q, jnp.zeros_like(q))
        cq = cc[:, hh:hh + 1]

        def block(kb, carry, masked, qh=qh, cq=cq, hh=hh):
            k0 = pl.multiple_of(kb * t, t)
            s = _dot_nt(qh, k_ref[pl.ds(k0, t), :])
            ck = cr_ref[0, 0, kb][hh:hh + 1, :]
            s = s + (cq - ck)
            if masked:
                s = jnp.where(causal, s, NEG_BIG)
            return _softmax_step(carry, s, v_ref[pl.ds(k0, t), :])

        carry = lax.fori_loop(0, i, lambda kb, c: block(kb, c, False), _softmax_init(t))
        _, l, acc = block(i, carry, True)
        outs.append(acc / l)
    o_ref[...] = jnp.where(_head_lane_mask(0), outs[0], outs[1]).astype(BF16)


def _fox_attention(pb, c_col, c_row, batch, seq):
    t = min(256, seq)
    nq = seq // t
    m = batch * seq
    return pl.pallas_call(
        _fox_kernel,
        grid=(batch, FOX_W // LANES, nq),
        in_specs=[pl.BlockSpec((t, LANES), lambda b, p, i: (b * nq + i, BLK_FOX_Q + p)),
                  pl.BlockSpec((seq, LANES), lambda b, p, i: (b, BLK_FOX_K + p)),
                  pl.BlockSpec((seq, LANES), lambda b, p, i: (b, BLK_FOX_V + p)),
                  pl.BlockSpec((1, 1, t, HEADS_PER_BLOCK), lambda b, p, i: (b, p, i, 0)),
                  pl.BlockSpec((1, 1, nq, HEADS_PER_BLOCK, t), lambda b, p, i: (b, p, 0, 0, 0))],
        out_specs=pl.BlockSpec((t, LANES), lambda b, p, i: (b * nq + i, p)),
        out_shape=jax.ShapeDtypeStruct((m, FOX_W), BF16),
        compiler_params=_params(3),
        name="fox_attention",
    )(pb, pb, pb, c_col, c_row)


def _sb_kernel(q_ref, k_ref, v_ref, o_ref, acc_ref, run_ref):
    t = q_ref.shape[0]
    i = pl.program_id(2)
    q = q_ref[...]
    row = lax.broadcasted_iota(I32, (t, t), 0)
    col = lax.broadcasted_iota(I32, (t, t), 1)
    strict = col < row
    suffix = jnp.where(row > col, 1.0, 0.0).astype(BF16)
    outs = []
    for hh in range(HEADS_PER_BLOCK):
        qh = jnp.where(_head_lane_mask(hh), q, jnp.zeros_like(q))
        acc_ref[...] = jnp.zeros_like(acc_ref)
        run_ref[...] = jnp.zeros_like(run_ref)

        def process(kb, diag, qh=qh):
            k0 = pl.multiple_of(kb * t, t)
            z = _dot_nt(qh, k_ref[pl.ds(k0, t), :])
            sp = _log1p_exp_neg_abs(z)
            log_beta = jnp.minimum(z, 0.0) - sp
            log_1m = jnp.minimum(-z, 0.0) - sp
            if diag:
                log_1m = jnp.where(strict, log_1m, 0.0)
            hi, mid, lo = _split3(log_1m)
            after = _dot(hi, suffix) + _dot(mid, suffix) + _dot(lo, suffix) + run_ref[...]
            a = jnp.exp(log_beta + after)
            if diag:
                a = jnp.where(strict, a, 0.0)
            acc_ref[...] += _dot(a.astype(BF16), v_ref[pl.ds(k0, t), :])
            run_ref[...] += jnp.sum(log_1m, axis=1, keepdims=True)

        process(i, True)

        def cond(state):
            kb, run_max = state
            return jnp.logical_and(kb >= 0, run_max > SB_CUTOFF)

        def body(state):
            kb, _ = state
            process(kb, False)
            return kb - 1, jnp.max(run_ref[...])

        lax.while_loop(cond, body, (i - 1, jnp.max(run_ref[...])))
        outs.append(acc_ref[...])
    o_ref[...] = jnp.where(_head_lane_mask(0), outs[0], outs[1]).astype(BF16)


def _sb_attention(pb, batch, seq):
    t = min(128, seq)
    nq = seq // t
    m = batch * seq
    return pl.pallas_call(
        _sb_kernel,
        grid=(batch, PAIR_W // LANES, nq),
        in_specs=[pl.BlockSpec((t, LANES), lambda b, p, i: (b * nq + i, BLK_SB_Q + p)),
                  pl.BlockSpec((seq, LANES), lambda b, p, i: (b, BLK_SB_K + p)),
                  pl.BlockSpec((seq, LANES), lambda b, p, i: (b, BLK_SB_V + p))],
        out_specs=pl.BlockSpec((t, LANES), lambda b, p, i: (b * nq + i, p)),
        out_shape=jax.ShapeDtypeStruct((m, PAIR_W), BF16),
        scratch_shapes=[pltpu.VMEM((t, LANES), F32), pltpu.VMEM((t, 1), F32)],
        compiler_params=_params(3),
        name="sb_attention",
    )(pb, pb, pb)


def _dsa_kernel(q_ref, iq_ref, w_ref, k_ref, v_ref, ka_ref, kb_ref, o_ref, key_ref, bias_ref, x_ref,
                *, topk, seq):
    t = q_ref.shape[0]
    i = pl.program_id(1)
    n_kc = (i + 1) * (t // (2 * LANES))
    n_cb = (i + 1) * (t // LANES)
    kf = float(topk)
    imin = jnp.int32(INT_MIN)
    lane_t = lax.broadcasted_iota(I32, (t, LANES), 1)
    row_t = lax.broadcasted_iota(I32, (t, LANES), 0) + i * t

    iq = iq_ref[...]
    w = w_ref[...] * (N_IDX_HEADS ** -0.5) * (IDX_DIM ** -0.5)

    def index_chunk(c, _):
        k0 = pl.multiple_of(c * 2 * LANES, 2 * LANES)
        ka = ka_ref[pl.ds(k0, 2 * LANES), :]
        kb = kb_ref[pl.ds(k0, 2 * LANES), :]
        score = jnp.zeros((t, 2 * LANES), F32)
        for g in range(N_IDX_HEADS // 2):
            qg = iq[:, g * LANES:(g + 1) * LANES]
            lo = SMALL_IDXW_LANE + 2 * g
            score = score + jnp.maximum(_dot_nt(qg, ka), 0.0) * w[:, lo:lo + 1]
            score = score + jnp.maximum(_dot_nt(qg, kb), 0.0) * w[:, lo + 1:lo + 2]
        bits = lax.bitcast_convert_type(score + 0.0, I32)
        key = jnp.where(bits < 0, bits ^ jnp.int32(0x7FFFFFFF), bits)
        for half in range(2):
            kh = key[:, half * LANES:(half + 1) * LANES]
            col = lane_t + (k0 + half * LANES)
            key_ref[2 * c + half] = jnp.where(col <= row_t, kh, imin)
        return 0

    lax.fori_loop(0, n_kc, index_chunk, 0)

    def count(pred):
        def body(c, acc):
            return acc + pred(key_ref[c], c)
        acc = lax.fori_loop(0, n_cb, body, jnp.zeros((t, LANES), F32))
        return jnp.sum(acc, axis=1, keepdims=True)

    def bit_step(it, prefix):
        cand_u = prefix | lax.shift_left(jnp.int32(1), 31 - it)
        cand = jnp.broadcast_to(cand_u ^ imin, (t, LANES))
        cnt = count(lambda kblk, c: jnp.where(kblk >= cand, 1.0, 0.0))
        return jnp.where(cnt >= kf, cand_u, prefix)

    kth = lax.fori_loop(0, 32, bit_step, jnp.zeros((t, 1), I32)) ^ imin
    kth_b = jnp.broadcast_to(kth, (t, LANES))
    n_gt = count(lambda kblk, c: jnp.where(kblk > kth_b, 1.0, 0.0))
    n_ge = count(lambda kblk, c: jnp.where(kblk >= kth_b, 1.0, 0.0))
    need = kf - n_gt

    x_ref[...] = jnp.full((t, 1), seq, I32)

    @pl.when(jnp.max(n_ge) > kf)
    def _():
        n_bits = max(1, (seq - 1).bit_length())

        def tie_step(it, x):
            cand = x | lax.shift_left(jnp.int32(1), n_bits - 1 - it)
            cand_b = jnp.broadcast_to(cand, (t, LANES))
            g = count(lambda kblk, c: jnp.where(
                kblk == kth_b, jnp.where(lane_t + c * LANES < cand_b, 1.0, 0.0), 0.0))
            return jnp.where(g < need, cand, x)

        x_ref[...] = lax.fori_loop(0, n_bits, tie_step, jnp.zeros((t, 1), I32))

    x_b = jnp.broadcast_to(jnp.where(kth == imin, -1, x_ref[...]), (t, LANES))

    def select(c, _):
        kblk = key_ref[c]
        tie = jnp.where(kblk == kth_b, jnp.where(lane_t + c * LANES <= x_b, 0.0, NEG_BIG), NEG_BIG)
        bias_ref[c] = jnp.where(kblk > kth_b, 0.0, tie)
        return 0

    lax.fori_loop(0, n_cb, select, 0)

    for p in range(PAIR_W // LANES):
        q = q_ref[:, p * LANES:(p + 1) * LANES]
        outs = []
        for hh in range(HEADS_PER_BLOCK):
            if p * HEADS_PER_BLOCK + hh >= N_DSA:
                outs.append(jnp.zeros((t, LANES), F32))
                continue
            qh = jnp.where(_head_lane_mask(hh), q, jnp.zeros_like(q))

            def block(c, carry, qh=qh, p=p):
                k0 = pl.multiple_of(c * 2 * LANES, 2 * LANES)
                s = _dot_nt(qh, k_ref[pl.ds(k0, 2 * LANES), p * LANES:(p + 1) * LANES])
                s = s + jnp.concatenate([bias_ref[2 * c], bias_ref[2 * c + 1]], axis=1)
                return _softmax_step(carry, s, v_ref[pl.ds(k0, 2 * LANES), p * LANES:(p + 1) * LANES])

            _, l, acc = lax.fori_loop(0, n_kc, block, _softmax_init(t))
            outs.append(acc / l)
        o_ref[:, p * LANES:(p + 1) * LANES] = jnp.where(_head_lane_mask(0), outs[0], outs[1]).astype(BF16)


def _dsa_attention(pb, ps, batch, seq):
    t = min(256, seq)
    nq = seq // t
    m = batch * seq
    topk = min(TOPK_MAX, seq // 4)
    assert t % (2 * LANES) == 0
    return pl.pallas_call(
        functools.partial(_dsa_kernel, topk=topk, seq=seq),
        grid=(batch, nq),
        in_specs=[pl.BlockSpec((t, PAIR_W), lambda b, i: (b * nq + i, BLK_DSA_Q // 3)),
                  pl.BlockSpec((t, 4 * LANES), lambda b, i: (b * nq + i, BLK_IDX_Q // 4)),
                  pl.BlockSpec((t, LANES), lambda b, i: (b * nq + i, 0)),
                  pl.BlockSpec((seq, PAIR_W), lambda b, i: (b, BLK_DSA_K // 3)),
                  pl.BlockSpec((seq, PAIR_W), lambda b, i: (b, BLK_DSA_V // 3)),
                  pl.BlockSpec((seq, LANES), lambda b, i: (b, BLK_IDX_KA)),
                  pl.BlockSpec((seq, LANES), lambda b, i: (b, BLK_IDX_KB))],
        out_specs=pl.BlockSpec((t, PAIR_W), lambda b, i: (b * nq + i, 0)),
        out_shape=jax.ShapeDtypeStruct((m, PAIR_W), BF16),
        scratch_shapes=[pltpu.VMEM((seq // LANES, t, LANES), I32), pltpu.VMEM((seq // LANES, t, LANES), F32),
                        pltpu.VMEM((t, 1), I32)],
        compiler_params=_params(2),
        name="dsa_attention",
    )(pb, pb, ps, pb, pb, pb, pb)


def _mixer_out_kernel(x_ref, of_ref, od_ref, os_ref, g_ref, wf_ref, wd_ref, ws_ref, wo_ref, o_ref):
    d = x_ref.shape[1]
    merged = jax.nn.sigmoid(g_ref[:, 0:d]) * _dot(of_ref[...], wf_ref[...])
    merged = merged + jax.nn.sigmoid(g_ref[:, d:2 * d]) * _dot(od_ref[...], wd_ref[...])
    merged = merged + jax.nn.sigmoid(g_ref[:, 2 * d:3 * d]) * _dot(os_ref[...], ws_ref[...])
    o_ref[...] = x_ref[...] + _dot(merged.astype(BF16), wo_ref[...])


def _mixer_out(x2, o_fox, o_dsa, o_sb, gates, wf, wd, ws, wo):
    m, d = x2.shape
    tm = min(512, m)
    row = lambda w: pl.BlockSpec((tm, w), lambda i: (i, 0))
    return pl.pallas_call(
        _mixer_out_kernel,
        grid=(m // tm,),
        in_specs=[row(d), row(FOX_W), row(PAIR_W), row(PAIR_W), row(N_BRANCH * d),
                  _resident((FOX_W, d)), _resident((PAIR_W, d)), _resident((PAIR_W, d)), _resident((d, d))],
        out_specs=row(d),
        out_shape=jax.ShapeDtypeStruct((m, d), F32),
        compiler_params=_params(1),
        name="mixer_out",
    )(x2, o_fox, o_dsa, o_sb, gates, wf, wd, ws, wo)


def _norm_matmul_kernel(x_ref, g_ref, w_ref, o_ref):
    o_ref[...] = _dot(_rms(x_ref[...], g_ref[...]).astype(BF16), w_ref[...]).astype(o_ref.dtype)


def _norm_matmul(x2, gain, w, out_dtype):
    m, d = x2.shape
    n = w.shape[1]
    tm = min(512, m)
    return pl.pallas_call(
        _norm_matmul_kernel,
        grid=(m // tm,),
        in_specs=[pl.BlockSpec((tm, d), lambda i: (i, 0)), _resident((1, d)), _resident((d, n))],
        out_specs=pl.BlockSpec((tm, n), lambda i: (i, 0)),
        out_shape=jax.ShapeDtypeStruct((m, n), out_dtype),
        compiler_params=_params(1),
        name="norm_matmul",
    )(x2, gain.reshape(1, d), w)


def _cross_kernel(x_ref, g_ref, wq_ref, kv_ref, wo_ref, o_ref):
    x = x_ref[...]
    h = _rms(x, g_ref[...]).astype(BF16)
    q = _dot(h, wq_ref[...])
    scale = CA_HEAD_DIM ** -0.5
    outs = []
    for hd in range(N_CA_HEADS):
        qh = q[:, hd * CA_HEAD_DIM:(hd + 1) * CA_HEAD_DIM].astype(BF16)
        logits = _dot_nt(qh, kv_ref[:, hd * CA_HEAD_DIM:(hd + 1) * CA_HEAD_DIM]) * scale
        e = jnp.exp(logits - jnp.max(logits, axis=1, keepdims=True))
        pv = _dot(e.astype(BF16), kv_ref[:, CA_W + hd * CA_HEAD_DIM:CA_W + (hd + 1) * CA_HEAD_DIM])
        outs.append(pv / jnp.sum(e, axis=1, keepdims=True))
    o = jnp.concatenate(outs, axis=1).astype(BF16)
    o_ref[...] = x + _dot(o, wo_ref[...])


def _cross_attention(x2, gain, wq, kv, wo, batch, seq):
    m, d = x2.shape
    tm = min(512, seq)
    n_mem = kv.shape[0] // batch
    per_batch = seq // tm
    return pl.pallas_call(
        _cross_kernel,
        grid=(m // tm,),
        in_specs=[pl.BlockSpec((tm, d), lambda i: (i, 0)), _resident((1, d)), _resident((d, CA_W)),
                  pl.BlockSpec((n_mem, 2 * CA_W), lambda i: (i // per_batch, 0)), _resident((CA_W, d))],
        out_specs=pl.BlockSpec((tm, d), lambda i: (i, 0)),
        out_shape=jax.ShapeDtypeStruct((m, d), F32),
        compiler_params=_params(1),
        name="cross_attention",
    )(x2, gain.reshape(1, d), wq, kv, wo)


def _pad_cols(w, n):
    return jnp.pad(w, ((0, 0), (0, n - w.shape[1])))


def _pad_rows(w, n):
    return jnp.pad(w, ((0, n - w.shape[0]), (0, 0)))


def _mixer_weights(w_in):
    offs = np.cumsum((0,) + IN_SPLITS)
    part = lambda j: w_in[:, offs[j]:offs[j + 1]]
    thirds = lambda w: jnp.split(w, 3, axis=1)
    scale = HEAD_DIM ** -0.5
    fq, fk, fv = thirds(part(0))
    dq, dk, dv = thirds(part(2))
    sq, sk, sv = thirds(part(6))
    iq, ik, iw = part(3), part(4), part(5)
    zk = jnp.zeros_like(ik)
    wb = jnp.concatenate(
        [fq * scale, fk, fv, _pad_cols(dv, PAIR_W),
         _pad_cols(sq * scale, PAIR_W), _pad_cols(sk, PAIR_W), _pad_cols(sv, PAIR_W),
         _pad_cols(dq * scale, PAIR_W), _pad_cols(dk, PAIR_W),
         jnp.concatenate([ik, zk], axis=1), iq, jnp.concatenate([zk, ik], axis=1)], axis=1).astype(BF16)
    small = jnp.concatenate([part(1), jnp.zeros((w_in.shape[0], SMALL_IDXW_LANE - N_FOX), w_in.dtype), iw], axis=1)
    return wb, part(7).astype(BF16), _pad_cols(small, LANES).astype(BF16)


def _hybrid_mixer(x2, pos2, invf, batch, seq, gain, w_in, b_fgate, w_fox_out, w_dsa_out, w_sb_out, w_out):
    wb, wg, ws = _mixer_weights(w_in)
    pb, gates, ps = _mixer_in(x2, gain, pos2, invf, wb, wg, ws)

    bias_row = _pad_cols(b_fgate.reshape(1, N_FOX).astype(F32), LANES)
    c = _fox_cumsum(ps, bias_row, batch, seq)
    t = min(256, seq)
    c_heads = c[:, :N_FOX].reshape(batch, seq, N_FOX // HEADS_PER_BLOCK, HEADS_PER_BLOCK)
    c_col = c_heads.transpose(0, 2, 1, 3)
    c_row = c_heads.transpose(0, 2, 3, 1).reshape(batch, -1, HEADS_PER_BLOCK, seq // t, t).transpose(0, 1, 3, 2, 4)

    o_fox = _fox_attention(pb, c_col, c_row, batch, seq)
    o_dsa = _dsa_attention(pb, ps, batch, seq)
    o_sb = _sb_attention(pb, batch, seq)
    return _mixer_out(x2, o_fox, o_dsa, o_sb, gates, w_fox_out.astype(BF16),
                      _pad_rows(w_dsa_out, PAIR_W).astype(BF16), _pad_rows(w_sb_out, PAIR_W).astype(BF16),
                      w_out.astype(BF16))


def kernel(x, mem, positions, ffn1_norm, ffn1_w_gu, ffn1_w_down, mix_norm, w_in, b_fgate, w_fox_out, w_dsa_out,
           w_sb_out, w_out, ca_norm, mem_norm, ca_w_q, ca_w_kv, ca_w_o, ffn2_norm, ffn2_w_gu, ffn2_w_down,
           final_norm):
    batch, seq, d = x.shape
    depth = ffn1_norm.shape[0]
    x2 = x.reshape(batch * seq, d)
    mem2 = mem.reshape(-1, d)
    pos2 = positions.reshape(batch * seq, 1).astype(I32)
    half = HEAD_DIM // 2
    inv_freq = jnp.power(ROPE_THETA, -jnp.arange(half, dtype=F32) * (2.0 / HEAD_DIM))
    invf = jnp.tile(inv_freq, LANES // half).reshape(1, LANES)
    for l in range(depth):
        x2 = _ffn(x2, ffn1_norm[l], ffn1_w_gu[l], ffn1_w_down[l])
        x2 = _hybrid_mixer(x2, pos2, invf, batch, seq, mix_norm[l], w_in[l], b_fgate[l], w_fox_out[l],
                           w_dsa_out[l], w_sb_out[l], w_out[l])
        kv = _norm_matmul(mem2, mem_norm[l], ca_w_kv[l].astype(BF16), BF16)
        x2 = _cross_attention(x2, ca_norm[l], ca_w_q[l].astype(BF16), kv, ca_w_o[l].astype(BF16), batch, seq)
        x2 = _ffn(x2, ffn2_norm[l], ffn2_w_gu[l], ffn2_w_down[l],
                  final_gain=final_norm if l == depth - 1 else None)
    return x2.reshape(batch, seq, d)
```

```python
import functools

import jax
import jax.numpy as jnp
import numpy as np
from jax import lax
from jax.experimental import pallas as pl
from jax.experimental.pallas import tpu as pltpu

F32 = jnp.float32
BF16 = jnp.bfloat16
I32 = jnp.int32

D_MODEL = 1024
HEAD_DIM = 64
N_FOX = 6
N_DSA = 5
N_SB = 5
N_IDX_HEADS = 8
IDX_DIM = 64
TOPK_MAX = 256
N_CA_HEADS = 4
CA_HEAD_DIM = 128
D_FF = 2816
ROPE_THETA = 10000.0
NORM_EPS = 1e-6
N_BRANCH = 3
HALF_STEP = 0.5
FOX_W = N_FOX * HEAD_DIM
DSA_W = N_DSA * HEAD_DIM
SB_W = N_SB * HEAD_DIM
CA_W = N_CA_HEADS * CA_HEAD_DIM
IN_SPLITS = (3 * FOX_W, N_FOX, 3 * DSA_W, N_IDX_HEADS * IDX_DIM, IDX_DIM, N_IDX_HEADS, 3 * SB_W,
             N_BRANCH * D_MODEL)

LANES = 128
HEADS_PER_BLOCK = LANES // HEAD_DIM
PAIR_W = 3 * LANES
VMEM_LIMIT = 56 * 2**20

BLK_FOX_Q, BLK_FOX_K, BLK_FOX_V = 0, 3, 6
BLK_DSA_V = 9
BLK_SB_Q, BLK_SB_K, BLK_SB_V = 12, 15, 18
BLK_ROPE_START = 21
BLK_DSA_Q, BLK_DSA_K = 21, 24
BLK_IDX_KA = 27
BLK_IDX_Q = 28
BLK_IDX_KB = 32
N_PROJ_BLOCKS = 33
SMALL_FGATE_LANE = 0
SMALL_IDXW_LANE = 8

NEG_BIG = -1e30
SB_CUTOFF = -110.0
INT_MIN = -2**31


def _params(n_grid):
    return pltpu.CompilerParams(dimension_semantics=("arbitrary",) * n_grid, vmem_limit_bytes=VMEM_LIMIT)


def _resident(shape):
    nd = len(shape)
    return pl.BlockSpec(shape, lambda *_: (0,) * nd, pipeline_mode=pl.Buffered(1))


def _rms(x, g):
    return x * lax.rsqrt(jnp.mean(x * x, axis=-1, keepdims=True) + NORM_EPS) * g


def _dot(a, b):
    return jnp.dot(a, b, preferred_element_type=F32)


def _dot_nt(a, b):
    return lax.dot_general(a, b, (((1,), (1,)), ((), ())), preferred_element_type=F32)


def _split3(x):
    hi = x.astype(BF16)
    r = x - hi.astype(F32)
    mid = r.astype(BF16)
    lo = (r - mid.astype(F32)).astype(BF16)
    return hi, mid, lo


def _log1p_exp_neg_abs(z):
    return jnp.log1p(jnp.exp(-jnp.abs(z)))


def _ffn_chunks():
    out, c = [], 0
    while c < D_FF:
        w = min(512, D_FF - c)
        out.append((c, w))
        c += w
    return tuple(out)


def _ffn_kernel(*refs, final):
    if final:
        x_ref, g_ref, wgu_ref, wd_ref, fn_ref, o_ref, a_ref = refs
    else:
        x_ref, g_ref, wgu_ref, wd_ref, o_ref, a_ref = refs
    x = x_ref[...]
    h = _rms(x, g_ref[...]).astype(BF16)
    for c0, w in _ffn_chunks():
        g = _dot(h, wgu_ref[:, c0:c0 + w])
        u = _dot(h, wgu_ref[:, D_FF + c0:D_FF + c0 + w])
        a_ref[:, c0:c0 + w] = (g * jax.nn.sigmoid(g) * u).astype(BF16)
    y = x + HALF_STEP * _dot(a_ref[...], wd_ref[...])
    if final:
        y = _rms(y, fn_ref[...])
    o_ref[...] = y


def _ffn(x2, gain, w_gu, w_down, final_gain=None):
    m, d = x2.shape
    tm = min(512, m)
    final = final_gain is not None
    in_specs = [pl.BlockSpec((tm, d), lambda i: (i, 0)), _resident((1, d)),
                _resident((d, 2 * D_FF)), _resident((D_FF, d))]
    args = [x2, gain.reshape(1, d), w_gu.astype(BF16), w_down.astype(BF16)]
    if final:
        in_specs.append(_resident((1, d)))
        args.append(final_gain.reshape(1, d))
    return pl.pallas_call(
        functools.partial(_ffn_kernel, final=final),
        grid=(m // tm,),
        in_specs=in_specs,
        out_specs=pl.BlockSpec((tm, d), lambda i: (i, 0)),
        out_shape=jax.ShapeDtypeStruct((m, d), F32),
        scratch_shapes=[pltpu.VMEM((tm, D_FF), BF16)],
        compiler_params=_params(1),
        name="ffn",
    )(*args)


def _mixer_in_kernel(x_ref, g_ref, pos_ref, invf_ref, wb_ref, wg_ref, ws_ref, pb_ref, pg_ref, ps_ref):
    h = _rms(x_ref[...], g_ref[...]).astype(BF16)
    ang = pos_ref[...].astype(F32) * invf_ref[...]
    cos = jnp.cos(ang)
    sin = jnp.sin(ang)
    lane = lax.broadcasted_iota(I32, (1, LANES), 1)
    first_half = (lane % HEAD_DIM) < (HEAD_DIM // 2)
    sin_signed = jnp.where(first_half, -sin, sin)
    chunk = 3 * LANES
    for c0 in range(0, N_PROJ_BLOCKS * LANES, chunk):
        res = _dot(h, wb_ref[:, c0:c0 + chunk])
        if c0 >= BLK_ROPE_START * LANES:
            parts = []
            for b in range(chunk // LANES):
                xb = res[:, b * LANES:(b + 1) * LANES]
                partner = jnp.where(first_half, pltpu.roll(xb, LANES - HEAD_DIM // 2, 1),
                                    pltpu.roll(xb, HEAD_DIM // 2, 1))
                parts.append(xb * cos + partner * sin_signed)
            res = jnp.concatenate(parts, axis=1)
        pb_ref[:, c0:c0 + chunk] = res.astype(BF16)
    gchunk = 4 * LANES
    for c0 in range(0, N_BRANCH * D_MODEL, gchunk):
        pg_ref[:, c0:c0 + gchunk] = _dot(h, wg_ref[:, c0:c0 + gchunk])
    ps_ref[...] = _dot(h, ws_ref[...])


def _mixer_in(x2, gain, pos2, invf, wb, wg, ws):
    m, d = x2.shape
    tm = min(512, m)
    nb, ng = wb.shape[1], wg.shape[1]
    return pl.pallas_call(
        _mixer_in_kernel,
        grid=(m // tm,),
        in_specs=[pl.BlockSpec((tm, d), lambda i: (i, 0)), _resident((1, d)),
                  pl.BlockSpec((tm, 1), lambda i: (i, 0)), _resident((1, LANES)),
                  _resident((d, nb)), _resident((d, ng)), _resident((d, LANES))],
        out_specs=[pl.BlockSpec((tm, nb), lambda i: (i, 0)), pl.BlockSpec((tm, ng), lambda i: (i, 0)),
                   pl.BlockSpec((tm, LANES), lambda i: (i, 0))],
        out_shape=[jax.ShapeDtypeStruct((m, nb), BF16), jax.ShapeDtypeStruct((m, ng), F32),
                   jax.ShapeDtypeStruct((m, LANES), F32)],
        compiler_params=_params(1),
        name="mixer_in",
    )(x2, gain.reshape(1, d), pos2, invf, wb, wg, ws)


FOX_FEAT_ROWS = 16


def _fox_feature_maps():
    k_map = np.zeros((3, LANES, FOX_W), np.float32)
    q_map = np.zeros((3, LANES, LANES), np.float32)
    k_one = np.zeros((1, FOX_W), np.float32)
    q_one = np.zeros((1, LANES), np.float32)
    for h in range(N_FOX):
        p, hh = divmod(h, HEADS_PER_BLOCK)
        for j in range(3):
            k_map[j, h, p * LANES + 3 * hh + j] = -1.0
            q_map[j, h, FOX_FEAT_ROWS * h + 6 + j] = 1.0
            k_one[0, p * LANES + 6 + j] = 1.0
            q_one[0, FOX_FEAT_ROWS * h + 3 * hh + j] = 1.0
    return (jnp.asarray(k_map, BF16), jnp.asarray(q_map, BF16), jnp.asarray(k_one), jnp.asarray(q_one))


def _fox_cumsum_kernel(ps_ref, b_ref, kmap_ref, qmap_ref, kone_ref, qone_ref, kf_ref, qf_ref, carry_ref):
    @pl.when(pl.program_id(1) == 0)
    def _():
        carry_ref[...] = jnp.zeros_like(carry_ref)

    tc = ps_ref.shape[0]
    z = ps_ref[...] + b_ref[...]
    log_f = jnp.minimum(z, 0.0) - _log1p_exp_neg_abs(z)
    row = lax.broadcasted_iota(I32, (tc, tc), 0)
    col = lax.broadcasted_iota(I32, (tc, tc), 1)
    tri = jnp.where(col <= row, 1.0, 0.0).astype(BF16)
    hi, mid, lo = _split3(log_f)
    cs = _dot(tri, hi) + _dot(tri, mid) + _dot(tri, lo) + carry_ref[...]
    carry_ref[...] = cs[tc - 1:tc, :]
    terms = _split3(cs)
    kf = kone_ref[...]
    qf = qone_ref[...]
    for j in range(3):
        kf = kf + _dot(terms[j], kmap_ref[j])
        qf = qf + _dot(terms[j], qmap_ref[j])
    kf_ref[...] = kf.astype(BF16)
    qf_ref[...] = qf.astype(BF16)


def _fox_cumsum(ps, bias_row, batch, seq):
    tc = min(256, seq)
    nj = seq // tc
    m = batch * seq
    tok = lambda w: pl.BlockSpec((tc, w), lambda b, j: (b * nj + j, 0))
    k_side, q_tok = pl.pallas_call(
        _fox_cumsum_kernel,
        grid=(batch, nj),
        in_specs=[tok(LANES), _resident((1, LANES)), _resident((3, LANES, FOX_W)), _resident((3, LANES, LANES)),
                  _resident((1, FOX_W)), _resident((1, LANES))],
        out_specs=[tok(FOX_W), tok(LANES)],
        out_shape=[jax.ShapeDtypeStruct((m, FOX_W), BF16), jax.ShapeDtypeStruct((m, LANES), BF16)],
        scratch_shapes=[pltpu.VMEM((1, LANES), F32)],
        compiler_params=_params(2),
        name="fox_cumsum",
    )(ps, bias_row, *_fox_feature_maps())
    q_side = q_tok[:, :N_FOX * FOX_FEAT_ROWS].reshape(batch, seq, N_FOX, FOX_FEAT_ROWS).transpose(0, 2, 3, 1)
    return k_side, q_side


def _head_lane_mask(hh):
    lane = lax.broadcasted_iota(I32, (1, LANES), 1)
    return (lane < HEAD_DIM) if hh == 0 else (lane >= HEAD_DIM)


def _head_row_mask(hh):
    row = lax.broadcasted_iota(I32, (LANES, 1), 0)
    return (row < HEAD_DIM) if hh == 0 else (row >= HEAD_DIM)


def _transpose_bf16(x):
    return x.astype(F32).T.astype(BF16)


def _fill_transposed(src_ref, dst_ref):
    n_blocks, _, tk = dst_ref.shape
    for j in range(n_blocks):
        dst_ref[j] = _transpose_bf16(src_ref[j * tk:(j + 1) * tk, :])


def _softmax_steps(carry, scores, values_t):
    stats = []
    for (m, l, _), s in zip(carry, scores):
        m_new = jnp.maximum(m, jnp.max(s, axis=0, keepdims=True))
        p = jnp.exp(s - m_new)
        alpha = jnp.exp(m - m_new)
        stats.append((m_new, alpha * l + jnp.sum(p, axis=0, keepdims=True), alpha, p.astype(BF16)))
    return tuple((m_new, l, alpha * acc + _dot(v_t, p))
                 for (m_new, l, alpha, p), (_, _, acc), v_t in zip(stats, carry, values_t))


def _softmax_init(tq):
    return (jnp.full((1, tq), NEG_BIG, F32), jnp.zeros((1, tq), F32), jnp.zeros((HEAD_DIM, tq), F32))


def _heads_to_token_rows(out_t):
    return jnp.concatenate(out_t, axis=0).T.astype(BF16)


def _fox_kernel(q_ref, k_ref, v_ref, kf_ref, qf_ref, o_ref, vt_ref):
    t = q_ref.shape[0]
    i = pl.program_id(1)

    @pl.when(i == 0)
    def _():
        _fill_transposed(v_ref, vt_ref)

    row = lax.broadcasted_iota(I32, (t, t), 0)
    col = lax.broadcasted_iota(I32, (t, t), 1)
    causal = row <= col
    pad = jnp.zeros((LANES - FOX_FEAT_ROWS, t), BF16)
    qx = []
    for p in range(FOX_W // LANES):
        q_t = q_ref[:, p * LANES:(p + 1) * LANES].astype(F32).T
        for hh in range(HEADS_PER_BLOCK):
            qx.append(jnp.concatenate([jnp.where(_head_row_mask(hh), q_t, 0.0).astype(BF16),
                                       qf_ref[0, p * HEADS_PER_BLOCK + hh], pad], axis=0))

    def block(kb, carry, masked):
        k0 = pl.multiple_of(kb * t, t)
        scores, values_t = [], []
        for h in range(N_FOX):
            p = h // HEADS_PER_BLOCK
            kx = jnp.concatenate([k_ref[pl.ds(k0, t), p * LANES:(p + 1) * LANES],
                                  kf_ref[pl.ds(k0, t), p * LANES:(p + 1) * LANES]], axis=1)
            s = _dot(kx, qx[h])
            scores.append(jnp.where(causal, s, NEG_BIG) if masked else s)
            values_t.append(vt_ref[kb, h * HEAD_DIM:(h + 1) * HEAD_DIM, :])
        return _softmax_steps(carry, scores, values_t)

    init = tuple(_softmax_init(t) for _ in range(N_FOX))
    carry = lax.fori_loop(0, i, lambda kb, c: block(kb, c, False), init)
    carry = block(i, carry, True)
    for p in range(FOX_W // LANES):
        o_ref[:, p * LANES:(p + 1) * LANES] = _heads_to_token_rows(
            [acc / l for _, l, acc in carry[p * HEADS_PER_BLOCK:(p + 1) * HEADS_PER_BLOCK]])


def _fox_attention(pb, k_side, q_side, batch, seq):
    t = min(256, seq)
    nq = seq // t
    m = batch * seq
    return pl.pallas_call(
        _fox_kernel,
        grid=(batch, nq),
        in_specs=[pl.BlockSpec((t, FOX_W), lambda b, i: (b * nq + i, BLK_FOX_Q // 3)),
                  pl.BlockSpec((seq, FOX_W), lambda b, i: (b, BLK_FOX_K // 3)),
                  pl.BlockSpec((seq, FOX_W), lambda b, i: (b, BLK_FOX_V // 3)),
                  pl.BlockSpec((seq, FOX_W), lambda b, i: (b, 0)),
                  pl.BlockSpec((1, N_FOX, FOX_FEAT_ROWS, t), lambda b, i: (b, 0, 0, i))],
        out_specs=pl.BlockSpec((t, FOX_W), lambda b, i: (b * nq + i, 0)),
        out_shape=jax.ShapeDtypeStruct((m, FOX_W), BF16),
        scratch_shapes=[pltpu.VMEM((nq, FOX_W, t), BF16)],
        compiler_params=_params(2),
        name="fox_attention",
    )(pb, pb, pb, k_side, q_side)


def _sb_kernel(q_ref, k_ref, v_ref, o_ref, vt_ref):
    tq = q_ref.shape[0]
    tk = vt_ref.shape[2]
    i = pl.program_id(1)

    @pl.when(i == 0)
    def _():
        _fill_transposed(v_ref, vt_ref)

    heads = range(N_SB)
    q_t = [q_ref[:, p * LANES:(p + 1) * LANES].astype(F32).T for p in range(PAIR_W // LANES)]
    qh = [jnp.where(_head_row_mask(h % HEADS_PER_BLOCK), q_t[h // HEADS_PER_BLOCK], 0.0).astype(BF16) for h in heads]
    r = lax.broadcasted_iota(I32, (tk, tk), 0)
    c = lax.broadcasted_iota(I32, (tk, tk), 1)
    upper = jnp.where(c > r, 1.0, 0.0).astype(BF16)
    key_pos = lax.broadcasted_iota(I32, (tk, tq), 0)
    qry_pos = lax.broadcasted_iota(I32, (tk, tq), 1) + i * tq

    def process(kb, state, masked):
        k0 = pl.multiple_of(kb * tk, tk)
        strict = key_pos + k0 < qry_pos
        z = [_dot(k_ref[pl.ds(k0, tk), (h // HEADS_PER_BLOCK) * LANES:(h // HEADS_PER_BLOCK + 1) * LANES], qh[h])
             for h in heads]
        log_beta, log_1m = [], []
        for zh in z:
            sp = _log1p_exp_neg_abs(zh)
            log_beta.append(jnp.minimum(zh, 0.0) - sp)
            l1m = jnp.minimum(-zh, 0.0) - sp
            log_1m.append(jnp.where(strict, l1m, 0.0) if masked else l1m)
        after = []
        for h in heads:
            hi, mid, lo = _split3(log_1m[h])
            after.append(_dot(upper, hi) + _dot(upper, mid) + _dot(upper, lo) + state[h][0])
        weights = []
        for h in heads:
            a = jnp.exp(log_beta[h] + after[h])
            weights.append((jnp.where(strict, a, 0.0) if masked else a).astype(BF16))
        return tuple((state[h][0] + jnp.sum(log_1m[h], axis=0, keepdims=True),
                      state[h][1] + _dot(vt_ref[kb, h * HEAD_DIM:(h + 1) * HEAD_DIM, :], weights[h]))
                     for h in heads)

    def run_max(state):
        m = state[0][0]
        for run, _ in state[1:]:
            m = jnp.maximum(m, run)
        return jnp.max(m)

    state = tuple((jnp.zeros((1, tq), F32), jnp.zeros((HEAD_DIM, tq), F32)) for _ in heads)
    n_diag = tq // tk
    for d in range(n_diag):
        state = process((i + 1) * n_diag - 1 - d, state, True)

    def cond(carry):
        kb, worst, _ = carry
        return jnp.logical_and(kb >= 0, worst > SB_CUTOFF)

    def body(carry):
        kb, _, st = carry
        st = process(kb, st, False)
        return kb - 1, run_max(st), st

    _, _, state = lax.while_loop(cond, body, (i * n_diag - 1, run_max(state), state))
    for p in range(PAIR_W // LANES):
        outs = [state[h][1] for h in heads if h // HEADS_PER_BLOCK == p]
        outs += [jnp.zeros((HEAD_DIM, tq), F32)] * (HEADS_PER_BLOCK - len(outs))
        o_ref[:, p * LANES:(p + 1) * LANES] = _heads_to_token_rows(outs)


def _sb_attention(pb, batch, seq):
    tq = min(256, seq)
    tk = min(128, seq)
    nq = seq // tq
    m = batch * seq
    return pl.pallas_call(
        _sb_kernel,
        grid=(batch, nq),
        in_specs=[pl.BlockSpec((tq, PAIR_W), lambda b, i: (b * nq + i, BLK_SB_Q // 3)),
                  pl.BlockSpec((seq, PAIR_W), lambda b, i: (b, BLK_SB_K // 3)),
                  pl.BlockSpec((seq, PAIR_W), lambda b, i: (b, BLK_SB_V // 3))],
        out_specs=pl.BlockSpec((tq, PAIR_W), lambda b, i: (b * nq + i, 0)),
        out_shape=jax.ShapeDtypeStruct((m, PAIR_W), BF16),
        scratch_shapes=[pltpu.VMEM((seq // tk, PAIR_W, tk), BF16)],
        compiler_params=_params(2),
        name="sb_attention",
    )(pb, pb, pb)


def _dsa_kernel(q_ref, iq_ref, w_ref, k_ref, v_ref, ka_ref, kb_ref, o_ref, key_ref, bias_ref, x_ref, vt_ref,
                *, topk, seq):
    t = q_ref.shape[0]
    i = pl.program_id(1)

    @pl.when(i == 0)
    def _():
        _fill_transposed(v_ref, vt_ref)

    n_kb = i + 1
    kf = float(topk)
    imin = jnp.int32(INT_MIN)
    key_pos = lax.broadcasted_iota(I32, (t, t), 0)
    qry_pos = lax.broadcasted_iota(I32, (t, t), 1) + i * t

    iq_t = [_transpose_bf16(iq_ref[:, g * LANES:(g + 1) * LANES]) for g in range(N_IDX_HEADS // 2)]
    w_t = (w_ref[...] * (N_IDX_HEADS ** -0.5) * (IDX_DIM ** -0.5)).T

    def index_block(kb, _):
        k0 = pl.multiple_of(kb * t, t)
        ka = ka_ref[pl.ds(k0, t), :]
        kb_ = kb_ref[pl.ds(k0, t), :]
        score = jnp.zeros((t, t), F32)
        for g in range(N_IDX_HEADS // 2):
            lo = SMALL_IDXW_LANE + 2 * g
            score = score + jnp.maximum(_dot(ka, iq_t[g]), 0.0) * w_t[lo:lo + 1, :]
            score = score + jnp.maximum(_dot(kb_, iq_t[g]), 0.0) * w_t[lo + 1:lo + 2, :]
        bits = lax.bitcast_convert_type(score + 0.0, I32)
        key = jnp.where(bits < 0, bits ^ jnp.int32(0x7FFFFFFF), bits)
        key_ref[pl.ds(k0, t), :] = jnp.where(key_pos + k0 <= qry_pos, key, imin)
        return 0

    lax.fori_loop(0, n_kb, index_block, 0)

    def count(pred):
        def body(kb, acc):
            k0 = pl.multiple_of(kb * t, t)
            ind = pred(key_ref[pl.ds(k0, t), :], k0)
            parts = [ind[r:r + 8, :] for r in range(0, t, 8)]
            while len(parts) > 1:
                parts = [parts[j] + parts[j + 1] for j in range(0, len(parts), 2)]
            return acc + parts[0]
        acc = lax.fori_loop(0, n_kb, body, jnp.zeros((8, t), F32))
        return jnp.sum(acc, axis=0, keepdims=True)

    def bit_step(it, prefix):
        cand_u = prefix | lax.shift_left(jnp.int32(1), 31 - it)
        cand = cand_u ^ imin
        cnt = count(lambda kblk, k0: jnp.where(kblk >= cand, 1.0, 0.0))
        return jnp.where(cnt >= kf, cand_u, prefix)

    kth = lax.fori_loop(0, 32, bit_step, jnp.zeros((1, t), I32)) ^ imin
    n_gt = count(lambda kblk, k0: jnp.where(kblk > kth, 1.0, 0.0))
    n_ge = count(lambda kblk, k0: jnp.where(kblk >= kth, 1.0, 0.0))
    need = kf - n_gt

    x_ref[...] = jnp.full((1, t), seq, I32)

    @pl.when(jnp.max(n_ge) > kf)
    def _():
        n_bits = max(1, (seq - 1).bit_length())

        def tie_step(it, x):
            cand = x | lax.shift_left(jnp.int32(1), n_bits - 1 - it)
            g = count(lambda kblk, k0: jnp.where(
                kblk == kth, jnp.where(key_pos + k0 < cand, 1.0, 0.0), 0.0))
            return jnp.where(g < need, cand, x)

        x_ref[...] = lax.fori_loop(0, n_bits, tie_step, jnp.zeros((1, t), I32))

    x = jnp.where(kth == imin, -1, x_ref[...])

    def select(kb, _):
        k0 = pl.multiple_of(kb * t, t)
        kblk = key_ref[pl.ds(k0, t), :]
        tie = jnp.where(kblk == kth, jnp.where(key_pos + k0 <= x, 0.0, NEG_BIG), NEG_BIG)
        bias_ref[pl.ds(k0, t), :] = jnp.where(kblk > kth, 0.0, tie)
        return 0

    lax.fori_loop(0, n_kb, select, 0)

    q_t = [q_ref[:, p * LANES:(p + 1) * LANES].astype(F32).T for p in range(PAIR_W // LANES)]
    qh = [jnp.where(_head_row_mask(h % HEADS_PER_BLOCK), q_t[h // HEADS_PER_BLOCK], 0.0).astype(BF16)
          for h in range(N_DSA)]

    def block(kb, carry):
        k0 = pl.multiple_of(kb * t, t)
        bias = bias_ref[pl.ds(k0, t), :]
        scores, values_t = [], []
        for h in range(N_DSA):
            p = h // HEADS_PER_BLOCK
            scores.append(_dot(k_ref[pl.ds(k0, t), p * LANES:(p + 1) * LANES], qh[h]) + bias)
            values_t.append(vt_ref[kb, h * HEAD_DIM:(h + 1) * HEAD_DIM, :])
        return _softmax_steps(carry, scores, values_t)

    carry = lax.fori_loop(0, n_kb, block, tuple(_softmax_init(t) for _ in range(N_DSA)))
    outs = [acc / l for _, l, acc in carry]
    outs += [jnp.zeros((HEAD_DIM, t), F32)] * (PAIR_W // HEAD_DIM - N_DSA)
    for p in range(PAIR_W // LANES):
        o_ref[:, p * LANES:(p + 1) * LANES] = _heads_to_token_rows(
            outs[p * HEADS_PER_BLOCK:(p + 1) * HEADS_PER_BLOCK])


def _dsa_attention(pb, ps, batch, seq):
    t = min(256, seq)
    nq = seq // t
    m = batch * seq
    topk = min(TOPK_MAX, seq // 4)
    return pl.pallas_call(
        functools.partial(_dsa_kernel, topk=topk, seq=seq),
        grid=(batch, nq),
        in_specs=[pl.BlockSpec((t, PAIR_W), lambda b, i: (b * nq + i, BLK_DSA_Q // 3)),
                  pl.BlockSpec((t, 4 * LANES), lambda b, i: (b * nq + i, BLK_IDX_Q // 4)),
                  pl.BlockSpec((t, LANES), lambda b, i: (b * nq + i, 0)),
                  pl.BlockSpec((seq, PAIR_W), lambda b, i: (b, BLK_DSA_K // 3)),
                  pl.BlockSpec((seq, PAIR_W), lambda b, i: (b, BLK_DSA_V // 3)),
                  pl.BlockSpec((seq, LANES), lambda b, i: (b, BLK_IDX_KA)),
                  pl.BlockSpec((seq, LANES), lambda b, i: (b, BLK_IDX_KB))],
        out_specs=pl.BlockSpec((t, PAIR_W), lambda b, i: (b * nq + i, 0)),
        out_shape=jax.ShapeDtypeStruct((m, PAIR_W), BF16),
        scratch_shapes=[pltpu.VMEM((seq, t), I32), pltpu.VMEM((seq, t), F32), pltpu.VMEM((1, t), I32),
                        pltpu.VMEM((nq, PAIR_W, t), BF16)],
        compiler_params=_params(2),
        name="dsa_attention",
    )(pb, pb, ps, pb, pb, pb, pb)


def _mixer_out_kernel(x_ref, of_ref, od_ref, os_ref, g_ref, wf_ref, wd_ref, ws_ref, wo_ref, o_ref):
    d = x_ref.shape[1]
    merged = jax.nn.sigmoid(g_ref[:, 0:d]) * _dot(of_ref[...], wf_ref[...])
    merged = merged + jax.nn.sigmoid(g_ref[:, d:2 * d]) * _dot(od_ref[...], wd_ref[...])
    merged = merged + jax.nn.sigmoid(g_ref[:, 2 * d:3 * d]) * _dot(os_ref[...], ws_ref[...])
    o_ref[...] = x_ref[...] + _dot(merged.astype(BF16), wo_ref[...])


def _mixer_out(x2, o_fox, o_dsa, o_sb, gates, wf, wd, ws, wo):
    m, d = x2.shape
    tm = min(512, m)
    row = lambda w: pl.BlockSpec((tm, w), lambda i: (i, 0))
    return pl.pallas_call(
        _mixer_out_kernel,
        grid=(m // tm,),
        in_specs=[row(d), row(FOX_W), row(PAIR_W), row(PAIR_W), row(N_BRANCH * d),
                  _resident((FOX_W, d)), _resident((PAIR_W, d)), _resident((PAIR_W, d)), _resident((d, d))],
        out_specs=row(d),
        out_shape=jax.ShapeDtypeStruct((m, d), F32),
        compiler_params=_params(1),
        name="mixer_out",
    )(x2, o_fox, o_dsa, o_sb, gates, wf, wd, ws, wo)


def _norm_matmul_kernel(x_ref, g_ref, w_ref, o_ref):
    o_ref[...] = _dot(_rms(x_ref[...], g_ref[...]).astype(BF16), w_ref[...]).astype(o_ref.dtype)


def _norm_matmul(x2, gain, w, out_dtype):
    m, d = x2.shape
    n = w.shape[1]
    tm = min(512, m)
    return pl.pallas_call(
        _norm_matmul_kernel,
        grid=(m // tm,),
        in_specs=[pl.BlockSpec((tm, d), lambda i: (i, 0)), _resident((1, d)), _resident((d, n))],
        out_specs=pl.BlockSpec((tm, n), lambda i: (i, 0)),
        out_shape=jax.ShapeDtypeStruct((m, n), out_dtype),
        compiler_params=_params(1),
        name="norm_matmul",
    )(x2, gain.reshape(1, d), w)


def _cross_kernel(x_ref, g_ref, wq_ref, kv_ref, wo_ref, o_ref):
    x = x_ref[...]
    h = _rms(x, g_ref[...]).astype(BF16)
    q = _dot(h, wq_ref[...])
    scale = CA_HEAD_DIM ** -0.5
    outs = []
    for hd in range(N_CA_HEADS):
        qh = q[:, hd * CA_HEAD_DIM:(hd + 1) * CA_HEAD_DIM].astype(BF16)
        logits = _dot_nt(qh, kv_ref[:, hd * CA_HEAD_DIM:(hd + 1) * CA_HEAD_DIM]) * scale
        e = jnp.exp(logits - jnp.max(logits, axis=1, keepdims=True))
        pv = _dot(e.astype(BF16), kv_ref[:, CA_W + hd * CA_HEAD_DIM:CA_W + (hd + 1) * CA_HEAD_DIM])
        outs.append(pv / jnp.sum(e, axis=1, keepdims=True))
    o = jnp.concatenate(outs, axis=1).astype(BF16)
    o_ref[...] = x + _dot(o, wo_ref[...])


def _cross_attention(x2, gain, wq, kv, wo, batch, seq):
    m, d = x2.shape
    tm = min(512, seq)
    n_mem = kv.shape[0] // batch
    per_batch = seq // tm
    return pl.pallas_call(
        _cross_kernel,
        grid=(m // tm,),
        in_specs=[pl.BlockSpec((tm, d), lambda i: (i, 0)), _resident((1, d)), _resident((d, CA_W)),
                  pl.BlockSpec((n_mem, 2 * CA_W), lambda i: (i // per_batch, 0)), _resident((CA_W, d))],
        out_specs=pl.BlockSpec((tm, d), lambda i: (i, 0)),
        out_shape=jax.ShapeDtypeStruct((m, d), F32),
        compiler_params=_params(1),
        name="cross_attention",
    )(x2, gain.reshape(1, d), wq, kv, wo)


def _pad_cols(w, n):
    return jnp.pad(w, ((0, 0), (0, n - w.shape[1])))


def _pad_rows(w, n):
    return jnp.pad(w, ((0, n - w.shape[0]), (0, 0)))


def _mixer_weights(w_in):
    offs = np.cumsum((0,) + IN_SPLITS)
    part = lambda j: w_in[:, offs[j]:offs[j + 1]]
    thirds = lambda w: jnp.split(w, 3, axis=1)
    scale = HEAD_DIM ** -0.5
    fq, fk, fv = thirds(part(0))
    dq, dk, dv = thirds(part(2))
    sq, sk, sv = thirds(part(6))
    iq, ik, iw = part(3), part(4), part(5)
    zk = jnp.zeros_like(ik)
    wb = jnp.concatenate(
        [fq * scale, fk, fv, _pad_cols(dv, PAIR_W),
         _pad_cols(sq * scale, PAIR_W), _pad_cols(sk, PAIR_W), _pad_cols(sv, PAIR_W),
         _pad_cols(dq * scale, PAIR_W), _pad_cols(dk, PAIR_W),
         jnp.concatenate([ik, zk], axis=1), iq, jnp.concatenate([zk, ik], axis=1)], axis=1).astype(BF16)
    small = jnp.concatenate([part(1), jnp.zeros((w_in.shape[0], SMALL_IDXW_LANE - N_FOX), w_in.dtype), iw], axis=1)
    return wb, part(7).astype(BF16), _pad_cols(small, LANES).astype(BF16)


def _hybrid_mixer(x2, pos2, invf, batch, seq, gain, w_in, b_fgate, w_fox_out, w_dsa_out, w_sb_out, w_out):
    wb, wg, ws = _mixer_weights(w_in)
    pb, gates, ps = _mixer_in(x2, gain, pos2, invf, wb, wg, ws)

    bias_row = _pad_cols(b_fgate.reshape(1, N_FOX).astype(F32), LANES)
    k_side, q_side = _fox_cumsum(ps, bias_row, batch, seq)

    o_fox = _fox_attention(pb, k_side, q_side, batch, seq)
    o_dsa = _dsa_attention(pb, ps, batch, seq)
    o_sb = _sb_attention(pb, batch, seq)
    return _mixer_out(x2, o_fox, o_dsa, o_sb, gates, w_fox_out.astype(BF16),
                      _pad_rows(w_dsa_out, PAIR_W).astype(BF16), _pad_rows(w_sb_out, PAIR_W).astype(BF16),
                      w_out.astype(BF16))


def kernel(x, mem, positions, ffn1_norm, ffn1_w_gu, ffn1_w_down, mix_norm, w_in, b_fgate, w_fox_out, w_dsa_out,
           w_sb_out, w_out, ca_norm, mem_norm, ca_w_q, ca_w_kv, ca_w_o, ffn2_norm, ffn2_w_gu, ffn2_w_down,
           final_norm):
    batch, seq, d = x.shape
    depth = ffn1_norm.shape[0]
    x2 = x.reshape(batch * seq, d)
    mem2 = mem.reshape(-1, d)
    pos2 = positions.reshape(batch * seq, 1).astype(I32)
    half = HEAD_DIM // 2
    inv_freq = jnp.power(ROPE_THETA, -jnp.arange(half, dtype=F32) * (2.0 / HEAD_DIM))
    invf = jnp.tile(inv_freq, LANES // half).reshape(1, LANES)
    for l in range(depth):
        x2 = _ffn(x2, ffn1_norm[l], ffn1_w_gu[l], ffn1_w_down[l])
        x2 = _hybrid_mixer(x2, pos2, invf, batch, seq, mix_norm[l], w_in[l], b_fgate[l], w_fox_out[l],
                           w_dsa_out[l], w_sb_out[l], w_out[l])
        kv = _norm_matmul(mem2, mem_norm[l], ca_w_kv[l].astype(BF16), BF16)
        x2 = _cross_attention(x2, ca_norm[l], ca_w_q[l].astype(BF16), kv, ca_w_o[l].astype(BF16), batch, seq)
        x2 = _ffn(x2, ffn2_norm[l], ffn2_w_gu[l], ffn2_w_down[l],
                  final_gain=final_norm if l == depth - 1 else None)
    return x2.reshape(batch, seq, d)
```

```python
import functools

import jax
import jax.numpy as jnp
import numpy as np
from jax import lax
from jax.experimental import pallas as pl
from jax.experimental.pallas import tpu as pltpu

F32 = jnp.float32
BF16 = jnp.bfloat16
I32 = jnp.int32

D_MODEL = 1024
HEAD_DIM = 64
N_FOX = 6
N_DSA = 5
N_SB = 5
N_IDX_HEADS = 8
IDX_DIM = 64
TOPK_MAX = 256
N_CA_HEADS = 4
CA_HEAD_DIM = 128
D_FF = 2816
ROPE_THETA = 10000.0
NORM_EPS = 1e-6
N_BRANCH = 3
HALF_STEP = 0.5
FOX_W = N_FOX * HEAD_DIM
DSA_W = N_DSA * HEAD_DIM
SB_W = N_SB * HEAD_DIM
CA_W = N_CA_HEADS * CA_HEAD_DIM
IN_SPLITS = (3 * FOX_W, N_FOX, 3 * DSA_W, N_IDX_HEADS * IDX_DIM, IDX_DIM, N_IDX_HEADS, 3 * SB_W,
             N_BRANCH * D_MODEL)

LANES = 128
HEADS_PER_BLOCK = LANES // HEAD_DIM
PAIR_W = 3 * LANES
VMEM_LIMIT = 56 * 2**20

BLK_FOX_Q, BLK_FOX_K, BLK_FOX_V = 0, 3, 6
BLK_DSA_V = 9
BLK_SB_Q, BLK_SB_K, BLK_SB_V = 12, 15, 18
BLK_ROPE_START = 21
BLK_DSA_Q, BLK_DSA_K = 21, 24
BLK_IDX_KA = 27
BLK_IDX_Q = 28
BLK_IDX_KB = 32
N_PROJ_BLOCKS = 33
SMALL_FGATE_LANE = 0
SMALL_IDXW_LANE = 8

NEG_BIG = -1e30
SB_CUTOFF = -110.0
INT_MIN = -2**31


def _params(n_grid):
    return pltpu.CompilerParams(dimension_semantics=("arbitrary",) * n_grid, vmem_limit_bytes=VMEM_LIMIT)


def _resident(shape):
    nd = len(shape)
    return pl.BlockSpec(shape, lambda *_: (0,) * nd, pipeline_mode=pl.Buffered(1))


def _rms(x, g):
    return x * lax.rsqrt(jnp.mean(x * x, axis=-1, keepdims=True) + NORM_EPS) * g


def _dot(a, b):
    return jnp.dot(a, b, preferred_element_type=F32)


def _dot_nt(a, b):
    return lax.dot_general(a, b, (((1,), (1,)), ((), ())), preferred_element_type=F32)


def _split3(x):
    hi = x.astype(BF16)
    r = x - hi.astype(F32)
    mid = r.astype(BF16)
    lo = (r - mid.astype(F32)).astype(BF16)
    return hi, mid, lo


def _log1p_exp_neg_abs(z):
    return jnp.log1p(jnp.exp(-jnp.abs(z)))


def _ffn_chunks():
    out, c = [], 0
    while c < D_FF:
        w = min(512, D_FF - c)
        out.append((c, w))
        c += w
    return tuple(out)


def _ffn_kernel(*refs, final):
    if final:
        x_ref, g_ref, wgu_ref, wd_ref, fn_ref, o_ref, a_ref = refs
    else:
        x_ref, g_ref, wgu_ref, wd_ref, o_ref, a_ref = refs
    x = x_ref[...]
    h = _rms(x, g_ref[...]).astype(BF16)
    for c0, w in _ffn_chunks():
        g = _dot(h, wgu_ref[:, c0:c0 + w])
        u = _dot(h, wgu_ref[:, D_FF + c0:D_FF + c0 + w])
        a_ref[:, c0:c0 + w] = (g * jax.nn.sigmoid(g) * u).astype(BF16)
    y = x + HALF_STEP * _dot(a_ref[...], wd_ref[...])
    if final:
        y = _rms(y, fn_ref[...])
    o_ref[...] = y


def _ffn(x2, gain, w_gu, w_down, final_gain=None):
    m, d = x2.shape
    tm = min(512, m)
    final = final_gain is not None
    in_specs = [pl.BlockSpec((tm, d), lambda i: (i, 0)), _resident((1, d)),
                _resident((d, 2 * D_FF)), _resident((D_FF, d))]
    args = [x2, gain.reshape(1, d), w_gu.astype(BF16), w_down.astype(BF16)]
    if final:
        in_specs.append(_resident((1, d)))
        args.append(final_gain.reshape(1, d))
    return pl.pallas_call(
        functools.partial(_ffn_kernel, final=final),
        grid=(m // tm,),
        in_specs=in_specs,
        out_specs=pl.BlockSpec((tm, d), lambda i: (i, 0)),
        out_shape=jax.ShapeDtypeStruct((m, d), F32),
        scratch_shapes=[pltpu.VMEM((tm, D_FF), BF16)],
        compiler_params=_params(1),
        name="ffn",
    )(*args)


def _mixer_in_kernel(x_ref, g_ref, pos_ref, invf_ref, wb_ref, wg_ref, ws_ref, pb_ref, pg_ref, ps_ref):
    h = _rms(x_ref[...], g_ref[...]).astype(BF16)
    ang = pos_ref[...].astype(F32) * invf_ref[...]
    cos = jnp.cos(ang)
    sin = jnp.sin(ang)
    lane = lax.broadcasted_iota(I32, (1, LANES), 1)
    first_half = (lane % HEAD_DIM) < (HEAD_DIM // 2)
    sin_signed = jnp.where(first_half, -sin, sin)
    chunk = 3 * LANES
    for c0 in range(0, N_PROJ_BLOCKS * LANES, chunk):
        res = _dot(h, wb_ref[:, c0:c0 + chunk])
        if c0 >= BLK_ROPE_START * LANES:
            parts = []
            for b in range(chunk // LANES):
                xb = res[:, b * LANES:(b + 1) * LANES]
                partner = jnp.where(first_half, pltpu.roll(xb, LANES - HEAD_DIM // 2, 1),
                                    pltpu.roll(xb, HEAD_DIM // 2, 1))
                parts.append(xb * cos + partner * sin_signed)
            res = jnp.concatenate(parts, axis=1)
        pb_ref[:, c0:c0 + chunk] = res.astype(BF16)
    gchunk = 4 * LANES
    for c0 in range(0, N_BRANCH * D_MODEL, gchunk):
        pg_ref[:, c0:c0 + gchunk] = _dot(h, wg_ref[:, c0:c0 + gchunk])
    ps_ref[...] = _dot(h, ws_ref[...])


def _mixer_in(x2, gain, pos2, invf, wb, wg, ws):
    m, d = x2.shape
    tm = min(512, m)
    nb, ng = wb.shape[1], wg.shape[1]
    return pl.pallas_call(
        _mixer_in_kernel,
        grid=(m // tm,),
        in_specs=[pl.BlockSpec((tm, d), lambda i: (i, 0)), _resident((1, d)),
                  pl.BlockSpec((tm, 1), lambda i: (i, 0)), _resident((1, LANES)),
                  _resident((d, nb)), _resident((d, ng)), _resident((d, LANES))],
        out_specs=[pl.BlockSpec((tm, nb), lambda i: (i, 0)), pl.BlockSpec((tm, ng), lambda i: (i, 0)),
                   pl.BlockSpec((tm, LANES), lambda i: (i, 0))],
        out_shape=[jax.ShapeDtypeStruct((m, nb), BF16), jax.ShapeDtypeStruct((m, ng), F32),
                   jax.ShapeDtypeStruct((m, LANES), F32)],
        compiler_params=_params(1),
        name="mixer_in",
    )(x2, gain.reshape(1, d), pos2, invf, wb, wg, ws)


FOX_FEAT_ROWS = 16


def _fox_feature_maps():
    k_map = np.zeros((3, LANES, FOX_W), np.float32)
    q_map = np.zeros((3, LANES, LANES), np.float32)
    k_one = np.zeros((1, FOX_W), np.float32)
    q_one = np.zeros((1, LANES), np.float32)
    for h in range(N_FOX):
        p, hh = divmod(h, HEADS_PER_BLOCK)
        for j in range(3):
            k_map[j, h, p * LANES + 3 * hh + j] = -1.0
            q_map[j, h, FOX_FEAT_ROWS * h + 6 + j] = 1.0
            k_one[0, p * LANES + 6 + j] = 1.0
            q_one[0, FOX_FEAT_ROWS * h + 3 * hh + j] = 1.0
    return (jnp.asarray(k_map, BF16), jnp.asarray(q_map, BF16), jnp.asarray(k_one), jnp.asarray(q_one))


def _fox_cumsum_kernel(ps_ref, b_ref, kmap_ref, qmap_ref, kone_ref, qone_ref, kf_ref, qf_ref, carry_ref):
    @pl.when(pl.program_id(1) == 0)
    def _():
        carry_ref[...] = jnp.zeros_like(carry_ref)

    tc = ps_ref.shape[0]
    z = ps_ref[...] + b_ref[...]
    log_f = jnp.minimum(z, 0.0) - _log1p_exp_neg_abs(z)
    row = lax.broadcasted_iota(I32, (tc, tc), 0)
    col = lax.broadcasted_iota(I32, (tc, tc), 1)
    tri = jnp.where(col <= row, 1.0, 0.0).astype(BF16)
    hi, mid, lo = _split3(log_f)
    cs = _dot(tri, hi) + _dot(tri, mid) + _dot(tri, lo) + carry_ref[...]
    carry_ref[...] = cs[tc - 1:tc, :]
    terms = _split3(cs)
    kf = kone_ref[...]
    qf = qone_ref[...]
    for j in range(3):
        kf = kf + _dot(terms[j], kmap_ref[j])
        qf = qf + _dot(terms[j], qmap_ref[j])
    kf_ref[...] = kf.astype(BF16)
    qf_ref[...] = qf.astype(BF16)


def _fox_cumsum(ps, bias_row, batch, seq):
    tc = min(256, seq)
    nj = seq // tc
    m = batch * seq
    tok = lambda w: pl.BlockSpec((tc, w), lambda b, j: (b * nj + j, 0))
    k_side, q_tok = pl.pallas_call(
        _fox_cumsum_kernel,
        grid=(batch, nj),
        in_specs=[tok(LANES), _resident((1, LANES)), _resident((3, LANES, FOX_W)), _resident((3, LANES, LANES)),
                  _resident((1, FOX_W)), _resident((1, LANES))],
        out_specs=[tok(FOX_W), tok(LANES)],
        out_shape=[jax.ShapeDtypeStruct((m, FOX_W), BF16), jax.ShapeDtypeStruct((m, LANES), BF16)],
        scratch_shapes=[pltpu.VMEM((1, LANES), F32)],
        compiler_params=_params(2),
        name="fox_cumsum",
    )(ps, bias_row, *_fox_feature_maps())
    q_side = q_tok[:, :N_FOX * FOX_FEAT_ROWS].reshape(batch, seq, N_FOX, FOX_FEAT_ROWS).transpose(0, 2, 3, 1)
    return k_side, q_side


def _head_lane_mask(hh):
    lane = lax.broadcasted_iota(I32, (1, LANES), 1)
    return (lane < HEAD_DIM) if hh == 0 else (lane >= HEAD_DIM)


def _head_row_mask(hh):
    row = lax.broadcasted_iota(I32, (LANES, 1), 0)
    return (row < HEAD_DIM) if hh == 0 else (row >= HEAD_DIM)


def _transpose_bf16(x):
    return x.astype(F32).T.astype(BF16)


def _fill_transposed(src_ref, dst_ref):
    n_blocks, _, tk = dst_ref.shape
    for j in range(n_blocks):
        dst_ref[j] = _transpose_bf16(src_ref[j * tk:(j + 1) * tk, :])


V_ROWS = HEAD_DIM + 16


def _fill_transposed_with_ones(src_ref, dst_ref, n_heads):
    n_blocks, _, tk = dst_ref.shape
    ones_pad = jnp.where(lax.broadcasted_iota(I32, (V_ROWS - HEAD_DIM, tk), 0) == 0, 1.0, 0.0).astype(BF16)
    for j in range(n_blocks):
        v_t = _transpose_bf16(src_ref[j * tk:(j + 1) * tk, :])
        for h in range(n_heads):
            dst_ref[j, h * V_ROWS:h * V_ROWS + HEAD_DIM, :] = v_t[h * HEAD_DIM:(h + 1) * HEAD_DIM, :]
            dst_ref[j, h * V_ROWS + HEAD_DIM:(h + 1) * V_ROWS, :] = ones_pad


def _softmax_steps(carry, scores, values_t):
    stats = []
    for (m, _), s in zip(carry, scores):
        m_new = jnp.maximum(m, jnp.max(s, axis=0, keepdims=True))
        stats.append((m_new, jnp.exp(m - m_new), jnp.exp(s - m_new).astype(BF16)))
    return tuple((m_new, alpha * acc + _dot(v_t, p))
                 for (m_new, alpha, p), (_, acc), v_t in zip(stats, carry, values_t))


def _softmax_init(tq):
    return (jnp.full((1, tq), NEG_BIG, F32), jnp.zeros((V_ROWS, tq), F32))


def _softmax_result(state):
    _, acc = state
    return acc[:HEAD_DIM, :] / acc[HEAD_DIM:HEAD_DIM + 1, :]


def _flash_blocks(n, score_fn, value_fn, carry, s_ref):
    n_heads = len(carry)

    def load(slot):
        return [s_ref[slot, h] for h in range(n_heads)]

    def store(slot, kb):
        for h, s in enumerate(score_fn(kb)):
            s_ref[slot, h] = s

    odd = n % 2
    carry = lax.cond(odd == 1, lambda c: _softmax_steps(c, score_fn(0), value_fn(0)), lambda c: c, carry)

    @pl.when(n >= 2)
    def _():
        store(0, odd)

    def body(j, c):
        kb = odd + 2 * j
        store(1, kb + 1)
        c = _softmax_steps(c, load(0), value_fn(kb))
        store(0, jnp.minimum(kb + 2, n - 1))
        return _softmax_steps(c, load(1), value_fn(kb + 1))

    return lax.fori_loop(0, n // 2, body, carry)


def _heads_to_token_rows(out_t):
    return jnp.concatenate(out_t, axis=0).T.astype(BF16)


def _fox_kernel(q_ref, k_ref, v_ref, kf_ref, qf_ref, o_ref, vt_ref, s_ref):
    t = q_ref.shape[0]
    i = pl.program_id(1)

    @pl.when(i == 0)
    def _():
        _fill_transposed_with_ones(v_ref, vt_ref, N_FOX)

    row = lax.broadcasted_iota(I32, (t, t), 0)
    col = lax.broadcasted_iota(I32, (t, t), 1)
    causal = row <= col
    pad = jnp.zeros((LANES - FOX_FEAT_ROWS, t), BF16)
    qx = []
    for p in range(FOX_W // LANES):
        q_t = q_ref[:, p * LANES:(p + 1) * LANES].astype(F32).T
        for hh in range(HEADS_PER_BLOCK):
            qx.append(jnp.concatenate([jnp.where(_head_row_mask(hh), q_t, 0.0).astype(BF16),
                                       qf_ref[0, p * HEADS_PER_BLOCK + hh], pad], axis=0))

    def scores(kb):
        k0 = pl.multiple_of(kb * t, t)
        out = []
        for h in range(N_FOX):
            p = h // HEADS_PER_BLOCK
            kx = jnp.concatenate([k_ref[pl.ds(k0, t), p * LANES:(p + 1) * LANES],
                                  kf_ref[pl.ds(k0, t), p * LANES:(p + 1) * LANES]], axis=1)
            out.append(_dot(kx, qx[h]))
        return out

    def values_t(kb):
        return [vt_ref[kb, h * V_ROWS:(h + 1) * V_ROWS, :] for h in range(N_FOX)]

    carry = _flash_blocks(i, scores, values_t, tuple(_softmax_init(t) for _ in range(N_FOX)), s_ref)
    carry = _softmax_steps(carry, [jnp.where(causal, s, NEG_BIG) for s in scores(i)], values_t(i))
    for p in range(FOX_W // LANES):
        o_ref[:, p * LANES:(p + 1) * LANES] = _heads_to_token_rows(
            [_softmax_result(st) for st in carry[p * HEADS_PER_BLOCK:(p + 1) * HEADS_PER_BLOCK]])


def _fox_attention(pb, k_side, q_side, batch, seq):
    t = min(256, seq)
    nq = seq // t
    m = batch * seq
    return pl.pallas_call(
        _fox_kernel,
        grid=(batch, nq),
        in_specs=[pl.BlockSpec((t, FOX_W), lambda b, i: (b * nq + i, BLK_FOX_Q // 3)),
                  pl.BlockSpec((seq, FOX_W), lambda b, i: (b, BLK_FOX_K // 3)),
                  pl.BlockSpec((seq, FOX_W), lambda b, i: (b, BLK_FOX_V // 3)),
                  pl.BlockSpec((seq, FOX_W), lambda b, i: (b, 0)),
                  pl.BlockSpec((1, N_FOX, FOX_FEAT_ROWS, t), lambda b, i: (b, 0, 0, i))],
        out_specs=pl.BlockSpec((t, FOX_W), lambda b, i: (b * nq + i, 0)),
        out_shape=jax.ShapeDtypeStruct((m, FOX_W), BF16),
        scratch_shapes=[pltpu.VMEM((nq, N_FOX * V_ROWS, t), BF16), pltpu.VMEM((2, N_FOX, t, t), F32)],
        compiler_params=_params(2),
        name="fox_attention",
    )(pb, pb, pb, k_side, q_side)


def _sb_kernel(q_ref, k_ref, v_ref, o_ref, vt_ref):
    tq = q_ref.shape[0]
    tk = vt_ref.shape[2]
    i = pl.program_id(1)

    @pl.when(i == 0)
    def _():
        _fill_transposed(v_ref, vt_ref)

    heads = range(N_SB)
    q_t = [q_ref[:, p * LANES:(p + 1) * LANES].astype(F32).T for p in range(PAIR_W // LANES)]
    qh = [jnp.where(_head_row_mask(h % HEADS_PER_BLOCK), q_t[h // HEADS_PER_BLOCK], 0.0).astype(BF16) for h in heads]
    r = lax.broadcasted_iota(I32, (tk, tk), 0)
    c = lax.broadcasted_iota(I32, (tk, tk), 1)
    upper = jnp.where(c > r, 1.0, 0.0).astype(BF16)
    key_pos = lax.broadcasted_iota(I32, (tk, tq), 0)
    qry_pos = lax.broadcasted_iota(I32, (tk, tq), 1) + i * tq

    def process(kb, state, masked):
        k0 = pl.multiple_of(kb * tk, tk)
        strict = key_pos + k0 < qry_pos
        z = [_dot(k_ref[pl.ds(k0, tk), (h // HEADS_PER_BLOCK) * LANES:(h // HEADS_PER_BLOCK + 1) * LANES], qh[h])
             for h in heads]
        log_beta, log_1m = [], []
        for zh in z:
            sp = _log1p_exp_neg_abs(zh)
            log_beta.append(jnp.minimum(zh, 0.0) - sp)
            l1m = jnp.minimum(-zh, 0.0) - sp
            log_1m.append(jnp.where(strict, l1m, 0.0) if masked else l1m)
        after = []
        for h in heads:
            hi, mid, lo = _split3(log_1m[h])
            after.append(_dot(upper, hi) + _dot(upper, mid) + _dot(upper, lo) + state[h][0])
        weights = []
        for h in heads:
            a = jnp.exp(log_beta[h] + after[h])
            weights.append((jnp.where(strict, a, 0.0) if masked else a).astype(BF16))
        return tuple((state[h][0] + jnp.sum(log_1m[h], axis=0, keepdims=True),
                      state[h][1] + _dot(vt_ref[kb, h * HEAD_DIM:(h + 1) * HEAD_DIM, :], weights[h]))
                     for h in heads)

    def run_max(state):
        m = state[0][0]
        for run, _ in state[1:]:
            m = jnp.maximum(m, run)
        return jnp.max(m)

    state = tuple((jnp.zeros((1, tq), F32), jnp.zeros((HEAD_DIM, tq), F32)) for _ in heads)
    n_diag = tq // tk
    for d in range(n_diag):
        state = process((i + 1) * n_diag - 1 - d, state, True)

    def cond(carry):
        kb, worst, _ = carry
        return jnp.logical_and(kb >= 0, worst > SB_CUTOFF)

    def body(carry):
        kb, _, st = carry
        st = process(kb, st, False)
        return kb - 1, run_max(st), st

    _, _, state = lax.while_loop(cond, body, (i * n_diag - 1, run_max(state), state))
    for p in range(PAIR_W // LANES):
        outs = [state[h][1] for h in heads if h // HEADS_PER_BLOCK == p]
        outs += [jnp.zeros((HEAD_DIM, tq), F32)] * (HEADS_PER_BLOCK - len(outs))
        o_ref[:, p * LANES:(p + 1) * LANES] = _heads_to_token_rows(outs)


def _sb_attention(pb, batch, seq):
    tq = min(256, seq)
    tk = min(128, seq)
    nq = seq // tq
    m = batch * seq
    return pl.pallas_call(
        _sb_kernel,
        grid=(batch, nq),
        in_specs=[pl.BlockSpec((tq, PAIR_W), lambda b, i: (b * nq + i, BLK_SB_Q // 3)),
                  pl.BlockSpec((seq, PAIR_W), lambda b, i: (b, BLK_SB_K // 3)),
                  pl.BlockSpec((seq, PAIR_W), lambda b, i: (b, BLK_SB_V // 3))],
        out_specs=pl.BlockSpec((tq, PAIR_W), lambda b, i: (b * nq + i, 0)),
        out_shape=jax.ShapeDtypeStruct((m, PAIR_W), BF16),
        scratch_shapes=[pltpu.VMEM((seq // tk, PAIR_W, tk), BF16)],
        compiler_params=_params(2),
        name="sb_attention",
    )(pb, pb, pb)


HALF_BITS = 16
HALF_OFFSET = 1 << (HALF_BITS - 1)


def _dsa_kernel(q_ref, iq_ref, w_ref, k_ref, v_ref, ka_ref, kb_ref, o_ref, key_ref, hi_ref, lo_ref, bias_ref,
                x_ref, vt_ref, s_ref, *, topk, seq):
    t = q_ref.shape[0]
    i = pl.program_id(1)

    @pl.when(i == 0)
    def _():
        _fill_transposed_with_ones(v_ref, vt_ref, N_DSA)

    n_kb = i + 1
    kf = float(topk)
    imin = jnp.int32(INT_MIN)
    key_pos = lax.broadcasted_iota(I32, (t, t), 0)
    qry_pos = lax.broadcasted_iota(I32, (t, t), 1) + i * t

    iq_t = [_transpose_bf16(iq_ref[:, g * LANES:(g + 1) * LANES]) for g in range(N_IDX_HEADS // 2)]
    w_t = (w_ref[...] * (N_IDX_HEADS ** -0.5) * (IDX_DIM ** -0.5)).T

    def index_block(kb, _):
        k0 = pl.multiple_of(kb * t, t)
        ka = ka_ref[pl.ds(k0, t), :]
        kb_ = kb_ref[pl.ds(k0, t), :]
        score = jnp.zeros((t, t), F32)
        for g in range(N_IDX_HEADS // 2):
            lo = SMALL_IDXW_LANE + 2 * g
            score = score + jnp.maximum(_dot(ka, iq_t[g]), 0.0) * w_t[lo:lo + 1, :]
            score = score + jnp.maximum(_dot(kb_, iq_t[g]), 0.0) * w_t[lo + 1:lo + 2, :]
        bits = lax.bitcast_convert_type(score + 0.0, I32)
        key = jnp.where(bits < 0, bits ^ jnp.int32(0x7FFFFFFF), bits)
        key = jnp.where(key_pos + k0 <= qry_pos, key, imin)
        key_ref[pl.ds(k0, t), :] = key
        hi_ref[pl.ds(k0, t), :] = lax.shift_right_arithmetic(key, HALF_BITS).astype(jnp.int16)
        lo_ref[pl.ds(k0, t), :] = ((key & (2 * HALF_OFFSET - 1)) - HALF_OFFSET).astype(jnp.int16)
        return 0

    lax.fori_loop(0, n_kb, index_block, 0)

    def count16(ref, pred):
        rows = 16
        def body(kb, acc):
            k0 = pl.multiple_of(kb * t, t)
            ind = pred(ref[pl.ds(k0, t), :])
            parts = [ind[r:r + rows, :] for r in range(0, t, rows)]
            while len(parts) > 1:
                parts = [parts[j] + parts[j + 1] for j in range(0, len(parts), 2)]
            return acc + parts[0]
        acc = lax.fori_loop(0, n_kb, body, jnp.zeros((rows, t), jnp.int16))
        return jnp.sum(acc.astype(F32), axis=0, keepdims=True)

    one16, zero16 = jnp.int16(1), jnp.int16(0)

    def kth_half(ref, need):
        def bit_step(it, prefix):
            cand_u = prefix | lax.shift_left(jnp.int32(1), HALF_BITS - 1 - it)
            cand = (cand_u - HALF_OFFSET).astype(jnp.int16)
            cnt = count16(ref, lambda blk: jnp.where(blk >= cand, one16, zero16))
            return jnp.where(cnt >= need, cand_u, prefix)
        return lax.fori_loop(0, HALF_BITS, bit_step, jnp.zeros((1, t), I32)) - HALF_OFFSET

    def count(pred):
        def body(kb, acc):
            k0 = pl.multiple_of(kb * t, t)
            ind = pred(key_ref[pl.ds(k0, t), :], k0)
            parts = [ind[r:r + 8, :] for r in range(0, t, 8)]
            while len(parts) > 1:
                parts = [parts[j] + parts[j + 1] for j in range(0, len(parts), 2)]
            return acc + parts[0]
        acc = lax.fori_loop(0, n_kb, body, jnp.zeros((8, t), F32))
        return jnp.sum(acc, axis=0, keepdims=True)

    kth_hi = kth_half(hi_ref, kf)
    kth_hi16 = kth_hi.astype(jnp.int16)
    need_lo = kf - count16(hi_ref, lambda blk: jnp.where(blk > kth_hi16, one16, zero16))

    def keep_bucket(kb, _):
        k0 = pl.multiple_of(kb * t, t)
        lo_ref[pl.ds(k0, t), :] = jnp.where(hi_ref[pl.ds(k0, t), :] == kth_hi16, lo_ref[pl.ds(k0, t), :],
                                            jnp.int16(-HALF_OFFSET))
        return 0

    lax.fori_loop(0, n_kb, keep_bucket, 0)
    kth_lo = kth_half(lo_ref, need_lo)
    kth = lax.shift_left(kth_hi, HALF_BITS) + (kth_lo + HALF_OFFSET)
    n_gt = count(lambda kblk, k0: jnp.where(kblk > kth, 1.0, 0.0))
    n_ge = count(lambda kblk, k0: jnp.where(kblk >= kth, 1.0, 0.0))
    need = kf - n_gt

    x_ref[...] = jnp.full((1, t), seq, I32)

    @pl.when(jnp.max(n_ge) > kf)
    def _():
        n_bits = max(1, (seq - 1).bit_length())

        def tie_step(it, x):
            cand = x | lax.shift_left(jnp.int32(1), n_bits - 1 - it)
            g = count(lambda kblk, k0: jnp.where(
                kblk == kth, jnp.where(key_pos + k0 < cand, 1.0, 0.0), 0.0))
            return jnp.where(g < need, cand, x)

        x_ref[...] = lax.fori_loop(0, n_bits, tie_step, jnp.zeros((1, t), I32))

    x = jnp.where(kth == imin, -1, x_ref[...])

    def select(kb, _):
        k0 = pl.multiple_of(kb * t, t)
        kblk = key_ref[pl.ds(k0, t), :]
        tie = jnp.where(kblk == kth, jnp.where(key_pos + k0 <= x, 0.0, NEG_BIG), NEG_BIG)
        bias_ref[pl.ds(k0, t), :] = jnp.where(kblk > kth, 0.0, tie)
        return 0

    lax.fori_loop(0, n_kb, select, 0)

    q_t = [q_ref[:, p * LANES:(p + 1) * LANES].astype(F32).T for p in range(PAIR_W // LANES)]
    qh = [jnp.where(_head_row_mask(h % HEADS_PER_BLOCK), q_t[h // HEADS_PER_BLOCK], 0.0).astype(BF16)
          for h in range(N_DSA)]

    def scores(kb):
        k0 = pl.multiple_of(kb * t, t)
        bias = bias_ref[pl.ds(k0, t), :]
        return [_dot(k_ref[pl.ds(k0, t), (h // HEADS_PER_BLOCK) * LANES:(h // HEADS_PER_BLOCK + 1) * LANES], qh[h])
                + bias for h in range(N_DSA)]

    def values_t(kb):
        return [vt_ref[kb, h * V_ROWS:(h + 1) * V_ROWS, :] for h in range(N_DSA)]

    carry = _flash_blocks(n_kb, scores, values_t, tuple(_softmax_init(t) for _ in range(N_DSA)), s_ref)
    outs = [_softmax_result(st) for st in carry]
    outs += [jnp.zeros((HEAD_DIM, t), F32)] * (PAIR_W // HEAD_DIM - N_DSA)
    for p in range(PAIR_W // LANES):
        o_ref[:, p * LANES:(p + 1) * LANES] = _heads_to_token_rows(
            outs[p * HEADS_PER_BLOCK:(p + 1) * HEADS_PER_BLOCK])


def _dsa_attention(pb, ps, batch, seq):
    t = min(256, seq)
    nq = seq // t
    m = batch * seq
    topk = min(TOPK_MAX, seq // 4)
    return pl.pallas_call(
        functools.partial(_dsa_kernel, topk=topk, seq=seq),
        grid=(batch, nq),
        in_specs=[pl.BlockSpec((t, PAIR_W), lambda b, i: (b * nq + i, BLK_DSA_Q // 3)),
                  pl.BlockSpec((t, 4 * LANES), lambda b, i: (b * nq + i, BLK_IDX_Q // 4)),
                  pl.BlockSpec((t, LANES), lambda b, i: (b * nq + i, 0)),
                  pl.BlockSpec((seq, PAIR_W), lambda b, i: (b, BLK_DSA_K // 3)),
                  pl.BlockSpec((seq, PAIR_W), lambda b, i: (b, BLK_DSA_V // 3)),
                  pl.BlockSpec((seq, LANES), lambda b, i: (b, BLK_IDX_KA)),
                  pl.BlockSpec((seq, LANES), lambda b, i: (b, BLK_IDX_KB))],
        out_specs=pl.BlockSpec((t, PAIR_W), lambda b, i: (b * nq + i, 0)),
        out_shape=jax.ShapeDtypeStruct((m, PAIR_W), BF16),
        scratch_shapes=[pltpu.VMEM((seq, t), I32), pltpu.VMEM((seq, t), jnp.int16), pltpu.VMEM((seq, t), jnp.int16),
                        pltpu.VMEM((seq, t), F32), pltpu.VMEM((1, t), I32),
                        pltpu.VMEM((nq, N_DSA * V_ROWS, t), BF16), pltpu.VMEM((2, N_DSA, t, t), F32)],
        compiler_params=_params(2),
        name="dsa_attention",
    )(pb, pb, ps, pb, pb, pb, pb)


def _mixer_out_kernel(x_ref, of_ref, od_ref, os_ref, g_ref, wf_ref, wd_ref, ws_ref, wo_ref, o_ref):
    d = x_ref.shape[1]
    merged = jax.nn.sigmoid(g_ref[:, 0:d]) * _dot(of_ref[...], wf_ref[...])
    merged = merged + jax.nn.sigmoid(g_ref[:, d:2 * d]) * _dot(od_ref[...], wd_ref[...])
    merged = merged + jax.nn.sigmoid(g_ref[:, 2 * d:3 * d]) * _dot(os_ref[...], ws_ref[...])
    o_ref[...] = x_ref[...] + _dot(merged.astype(BF16), wo_ref[...])


def _mixer_out(x2, o_fox, o_dsa, o_sb, gates, wf, wd, ws, wo):
    m, d = x2.shape
    tm = min(512, m)
    row = lambda w: pl.BlockSpec((tm, w), lambda i: (i, 0))
    return pl.pallas_call(
        _mixer_out_kernel,
        grid=(m // tm,),
        in_specs=[row(d), row(FOX_W), row(PAIR_W), row(PAIR_W), row(N_BRANCH * d),
                  _resident((FOX_W, d)), _resident((PAIR_W, d)), _resident((PAIR_W, d)), _resident((d, d))],
        out_specs=row(d),
        out_shape=jax.ShapeDtypeStruct((m, d), F32),
        compiler_params=_params(1),
        name="mixer_out",
    )(x2, o_fox, o_dsa, o_sb, gates, wf, wd, ws, wo)


def _norm_matmul_kernel(x_ref, g_ref, w_ref, o_ref):
    o_ref[...] = _dot(_rms(x_ref[...], g_ref[...]).astype(BF16), w_ref[...]).astype(o_ref.dtype)


def _norm_matmul(x2, gain, w, out_dtype):
    m, d = x2.shape
    n = w.shape[1]
    tm = min(512, m)
    return pl.pallas_call(
        _norm_matmul_kernel,
        grid=(m // tm,),
        in_specs=[pl.BlockSpec((tm, d), lambda i: (i, 0)), _resident((1, d)), _resident((d, n))],
        out_specs=pl.BlockSpec((tm, n), lambda i: (i, 0)),
        out_shape=jax.ShapeDtypeStruct((m, n), out_dtype),
        compiler_params=_params(1),
        name="norm_matmul",
    )(x2, gain.reshape(1, d), w)


def _cross_kernel(x_ref, g_ref, wq_ref, kv_ref, wo_ref, o_ref):
    x = x_ref[...]
    h = _rms(x, g_ref[...]).astype(BF16)
    q = _dot(h, wq_ref[...])
    scale = CA_HEAD_DIM ** -0.5
    outs = []
    for hd in range(N_CA_HEADS):
        qh = q[:, hd * CA_HEAD_DIM:(hd + 1) * CA_HEAD_DIM].astype(BF16)
        logits = _dot_nt(qh, kv_ref[:, hd * CA_HEAD_DIM:(hd + 1) * CA_HEAD_DIM]) * scale
        e = jnp.exp(logits - jnp.max(logits, axis=1, keepdims=True))
        pv = _dot(e.astype(BF16), kv_ref[:, CA_W + hd * CA_HEAD_DIM:CA_W + (hd + 1) * CA_HEAD_DIM])
        outs.append(pv / jnp.sum(e, axis=1, keepdims=True))
    o = jnp.concatenate(outs, axis=1).astype(BF16)
    o_ref[...] = x + _dot(o, wo_ref[...])


def _cross_attention(x2, gain, wq, kv, wo, batch, seq):
    m, d = x2.shape
    tm = min(512, seq)
    n_mem = kv.shape[0] // batch
    per_batch = seq // tm
    return pl.pallas_call(
        _cross_kernel,
        grid=(m // tm,),
        in_specs=[pl.BlockSpec((tm, d), lambda i: (i, 0)), _resident((1, d)), _resident((d, CA_W)),
                  pl.BlockSpec((n_mem, 2 * CA_W), lambda i: (i // per_batch, 0)), _resident((CA_W, d))],
        out_specs=pl.BlockSpec((tm, d), lambda i: (i, 0)),
        out_shape=jax.ShapeDtypeStruct((m, d), F32),
        compiler_params=_params(1),
        name="cross_attention",
    )(x2, gain.reshape(1, d), wq, kv, wo)


def _pad_cols(w, n):
    return jnp.pad(w, ((0, 0), (0, n - w.shape[1])))


def _pad_rows(w, n):
    return jnp.pad(w, ((0, n - w.shape[0]), (0, 0)))


def _mixer_weights(w_in):
    offs = np.cumsum((0,) + IN_SPLITS)
    part = lambda j: w_in[:, offs[j]:offs[j + 1]]
    thirds = lambda w: jnp.split(w, 3, axis=1)
    scale = HEAD_DIM ** -0.5
    fq, fk, fv = thirds(part(0))
    dq, dk, dv = thirds(part(2))
    sq, sk, sv = thirds(part(6))
    iq, ik, iw = part(3), part(4), part(5)
    zk = jnp.zeros_like(ik)
    wb = jnp.concatenate(
        [fq * scale, fk, fv, _pad_cols(dv, PAIR_W),
         _pad_cols(sq * scale, PAIR_W), _pad_cols(sk, PAIR_W), _pad_cols(sv, PAIR_W),
         _pad_cols(dq * scale, PAIR_W), _pad_cols(dk, PAIR_W),
         jnp.concatenate([ik, zk], axis=1), iq, jnp.concatenate([zk, ik], axis=1)], axis=1).astype(BF16)
    small = jnp.concatenate([part(1), jnp.zeros((w_in.shape[0], SMALL_IDXW_LANE - N_FOX), w_in.dtype), iw], axis=1)
    return wb, part(7).astype(BF16), _pad_cols(small, LANES).astype(BF16)


def _hybrid_mixer(x2, pos2, invf, batch, seq, gain, w_in, b_fgate, w_fox_out, w_dsa_out, w_sb_out, w_out):
    wb, wg, ws = _mixer_weights(w_in)
    pb, gates, ps = _mixer_in(x2, gain, pos2, invf, wb, wg, ws)

    bias_row = _pad_cols(b_fgate.reshape(1, N_FOX).astype(F32), LANES)
    k_side, q_side = _fox_cumsum(ps, bias_row, batch, seq)

    o_fox = _fox_attention(pb, k_side, q_side, batch, seq)
    o_dsa = _dsa_attention(pb, ps, batch, seq)
    o_sb = _sb_attention(pb, batch, seq)
    return _mixer_out(x2, o_fox, o_dsa, o_sb, gates, w_fox_out.astype(BF16),
                      _pad_rows(w_dsa_out, PAIR_W).astype(BF16), _pad_rows(w_sb_out, PAIR_W).astype(BF16),
                      w_out.astype(BF16))


def kernel(x, mem, positions, ffn1_norm, ffn1_w_gu, ffn1_w_down, mix_norm, w_in, b_fgate, w_fox_out, w_dsa_out,
           w_sb_out, w_out, ca_norm, mem_norm, ca_w_q, ca_w_kv, ca_w_o, ffn2_norm, ffn2_w_gu, ffn2_w_down,
           final_norm):
    batch, seq, d = x.shape
    depth = ffn1_norm.shape[0]
    x2 = x.reshape(batch * seq, d)
    mem2 = mem.reshape(-1, d)
    pos2 = positions.reshape(batch * seq, 1).astype(I32)
    half = HEAD_DIM // 2
    inv_freq = jnp.power(ROPE_THETA, -jnp.arange(half, dtype=F32) * (2.0 / HEAD_DIM))
    invf = jnp.tile(inv_freq, LANES // half).reshape(1, LANES)
    for l in range(depth):
        x2 = _ffn(x2, ffn1_norm[l], ffn1_w_gu[l], ffn1_w_down[l])
        x2 = _hybrid_mixer(x2, pos2, invf, batch, seq, mix_norm[l], w_in[l], b_fgate[l], w_fox_out[l],
                           w_dsa_out[l], w_sb_out[l], w_out[l])
        kv = _norm_matmul(mem2, mem_norm[l], ca_w_kv[l].astype(BF16), BF16)
        x2 = _cross_attention(x2, ca_norm[l], ca_w_q[l].astype(BF16), kv, ca_w_o[l].astype(BF16), batch, seq)
        x2 = _ffn(x2, ffn2_norm[l], ffn2_w_gu[l], ffn2_w_down[l],
                  final_gain=final_norm if l == depth - 1 else None)
    return x2.reshape(batch, seq, d)
```

```python
import functools

import jax
import jax.numpy as jnp
import numpy as np
from jax import lax
from jax.experimental import pallas as pl
from jax.experimental.pallas import tpu as pltpu

F32 = jnp.float32
BF16 = jnp.bfloat16
I32 = jnp.int32

D_MODEL = 1024
HEAD_DIM = 64
N_FOX = 6
N_DSA = 5
N_SB = 5
N_IDX_HEADS = 8
IDX_DIM = 64
TOPK_MAX = 256
N_CA_HEADS = 4
CA_HEAD_DIM = 128
D_FF = 2816
ROPE_THETA = 10000.0
NORM_EPS = 1e-6
N_BRANCH = 3
HALF_STEP = 0.5
FOX_W = N_FOX * HEAD_DIM
DSA_W = N_DSA * HEAD_DIM
SB_W = N_SB * HEAD_DIM
CA_W = N_CA_HEADS * CA_HEAD_DIM
IN_SPLITS = (3 * FOX_W, N_FOX, 3 * DSA_W, N_IDX_HEADS * IDX_DIM, IDX_DIM, N_IDX_HEADS, 3 * SB_W,
             N_BRANCH * D_MODEL)

LANES = 128
HEADS_PER_BLOCK = LANES // HEAD_DIM
PAIR_W = 3 * LANES
VMEM_LIMIT = 56 * 2**20

BLK_DSA_Q, BLK_DSA_K = 0, 3
BLK_IDX_KA = 6
BLK_IDX_KB = 7
BLK_IDX_Q = 8
N_ROPE_BLOCKS = 12
BLK_FOX_Q, BLK_FOX_K, BLK_FOX_V = 12, 15, 18
BLK_DSA_V = 21
BLK_SB_Q, BLK_SB_K, BLK_SB_V = 24, 27, 30
N_PROJ_BLOCKS = 33
PROJ_CHUNK = 4 * LANES
SMALL_FGATE_LANE = 0
SMALL_IDXW_LANE = 8

NEG_BIG = -1e30
SB_CUTOFF = -110.0
INT_MIN = -2**31


def _params(n_grid):
    return pltpu.CompilerParams(dimension_semantics=("arbitrary",) * n_grid, vmem_limit_bytes=VMEM_LIMIT)


def _resident(shape):
    nd = len(shape)
    return pl.BlockSpec(shape, lambda *_: (0,) * nd, pipeline_mode=pl.Buffered(1))


def _rms(x, g):
    return x * lax.rsqrt(jnp.mean(x * x, axis=-1, keepdims=True) + NORM_EPS) * g


def _dot(a, b):
    return jnp.dot(a, b, preferred_element_type=F32)


def _dot_nt(a, b):
    return lax.dot_general(a, b, (((1,), (1,)), ((), ())), preferred_element_type=F32)


def _split3(x):
    hi = x.astype(BF16)
    r = x - hi.astype(F32)
    mid = r.astype(BF16)
    lo = (r - mid.astype(F32)).astype(BF16)
    return hi, mid, lo


def _log1p_exp_neg_abs(z):
    return jnp.log1p(jnp.exp(-jnp.abs(z)))


def _ffn_chunks():
    out, c = [], 0
    while c < D_FF:
        w = min(512, D_FF - c)
        out.append((c, w))
        c += w
    return tuple(out)


def _ffn_kernel(*refs, final):
    if final:
        x_ref, g_ref, wgu_ref, wd_ref, fn_ref, o_ref, a_ref = refs
    else:
        x_ref, g_ref, wgu_ref, wd_ref, o_ref, a_ref = refs
    x = x_ref[...]
    h = _rms(x, g_ref[...]).astype(BF16)
    for c0, w in _ffn_chunks():
        g = _dot(h, wgu_ref[:, c0:c0 + w])
        u = _dot(h, wgu_ref[:, D_FF + c0:D_FF + c0 + w])
        a_ref[:, c0:c0 + w] = (g * jax.nn.sigmoid(g) * u).astype(BF16)
    y = x + HALF_STEP * _dot(a_ref[...], wd_ref[...])
    if final:
        y = _rms(y, fn_ref[...])
    o_ref[...] = y


def _ffn(x2, gain, w_gu, w_down, final_gain=None):
    m, d = x2.shape
    tm = min(512, m)
    final = final_gain is not None
    in_specs = [pl.BlockSpec((tm, d), lambda i: (i, 0)), _resident((1, d)),
                _resident((d, 2 * D_FF)), _resident((D_FF, d))]
    args = [x2, gain.reshape(1, d), w_gu.astype(BF16), w_down.astype(BF16)]
    if final:
        in_specs.append(_resident((1, d)))
        args.append(final_gain.reshape(1, d))
    return pl.pallas_call(
        functools.partial(_ffn_kernel, final=final),
        grid=(m // tm,),
        in_specs=in_specs,
        out_specs=pl.BlockSpec((tm, d), lambda i: (i, 0)),
        out_shape=jax.ShapeDtypeStruct((m, d), F32),
        scratch_shapes=[pltpu.VMEM((tm, D_FF), BF16)],
        compiler_params=_params(1),
        name="ffn",
    )(*args)


def _first_half_lanes():
    lane = lax.broadcasted_iota(I32, (1, LANES), 1)
    return (lane % HEAD_DIM) < (HEAD_DIM // 2)


def _rope_tables_kernel(pos_ref, invf_ref, cos_ref, sin_ref):
    ang = pos_ref[...].astype(F32) * invf_ref[...]
    sin = jnp.sin(ang)
    cos_ref[...] = jnp.cos(ang)
    sin_ref[...] = jnp.where(_first_half_lanes(), -sin, sin)


def _rope_tables(pos2, invf):
    m = pos2.shape[0]
    tm = min(512, m)
    tok = pl.BlockSpec((tm, LANES), lambda i: (i, 0))
    return pl.pallas_call(
        _rope_tables_kernel,
        grid=(m // tm,),
        in_specs=[pl.BlockSpec((tm, 1), lambda i: (i, 0)), _resident((1, LANES))],
        out_specs=[tok, tok],
        out_shape=[jax.ShapeDtypeStruct((m, LANES), F32)] * 2,
        compiler_params=_params(1),
        name="rope_tables",
    )(pos2, invf)


def _mixer_in_kernel(x_ref, g_ref, cos_ref, sin_ref, wb_ref, wg_ref, pb_ref, pg_ref, ps_ref):
    h = _rms(x_ref[...], g_ref[...]).astype(BF16)
    cos = cos_ref[...]
    sin_signed = sin_ref[...]
    first_half = _first_half_lanes()
    n_cols = (N_PROJ_BLOCKS + 1) * LANES
    for c0 in range(0, n_cols, PROJ_CHUNK):
        c1 = min(c0 + PROJ_CHUNK, n_cols)
        res = _dot(h, wb_ref[:, c0:c1])
        if c1 == n_cols:
            ps_ref[...] = res[:, c1 - c0 - LANES:]
            c1 -= LANES
            res = res[:, :c1 - c0]
        if c0 < N_ROPE_BLOCKS * LANES:
            parts = []
            for b in range((c1 - c0) // LANES):
                xb = res[:, b * LANES:(b + 1) * LANES]
                partner = jnp.where(first_half, pltpu.roll(xb, LANES - HEAD_DIM // 2, 1),
                                    pltpu.roll(xb, HEAD_DIM // 2, 1))
                parts.append(xb * cos + partner * sin_signed)
            res = jnp.concatenate(parts, axis=1)
        pb_ref[:, c0:c1] = res.astype(BF16)
    for c0 in range(0, N_BRANCH * D_MODEL, PROJ_CHUNK):
        pg_ref[:, c0:c0 + PROJ_CHUNK] = _dot(h, wg_ref[:, c0:c0 + PROJ_CHUNK])


def _mixer_in(x2, gain, cos, sin_signed, wb, wg):
    m, d = x2.shape
    tm = min(512, m)
    nb, ng = N_PROJ_BLOCKS * LANES, wg.shape[1]
    return pl.pallas_call(
        _mixer_in_kernel,
        grid=(m // tm,),
        in_specs=[pl.BlockSpec((tm, d), lambda i: (i, 0)), _resident((1, d)),
                  pl.BlockSpec((tm, LANES), lambda i: (i, 0)), pl.BlockSpec((tm, LANES), lambda i: (i, 0)),
                  _resident((d, nb + LANES)), _resident((d, ng))],
        out_specs=[pl.BlockSpec((tm, nb), lambda i: (i, 0)), pl.BlockSpec((tm, ng), lambda i: (i, 0)),
                   pl.BlockSpec((tm, LANES), lambda i: (i, 0))],
        out_shape=[jax.ShapeDtypeStruct((m, nb), BF16), jax.ShapeDtypeStruct((m, ng), F32),
                   jax.ShapeDtypeStruct((m, LANES), F32)],
        compiler_params=_params(1),
        name="mixer_in",
    )(x2, gain.reshape(1, d), cos, sin_signed, wb, wg)


FOX_FEAT_ROWS = 16


def _fox_feature_maps():
    k_map = np.zeros((3, LANES, FOX_W), np.float32)
    q_map = np.zeros((3, LANES, LANES), np.float32)
    k_one = np.zeros((1, FOX_W), np.float32)
    q_one = np.zeros((1, LANES), np.float32)
    for h in range(N_FOX):
        p, hh = divmod(h, HEADS_PER_BLOCK)
        for j in range(3):
            k_map[j, h, p * LANES + 3 * hh + j] = -1.0
            q_map[j, h, FOX_FEAT_ROWS * h + 6 + j] = 1.0
            k_one[0, p * LANES + 6 + j] = 1.0
            q_one[0, FOX_FEAT_ROWS * h + 3 * hh + j] = 1.0
    return (jnp.asarray(k_map, BF16), jnp.asarray(q_map, BF16), jnp.asarray(k_one), jnp.asarray(q_one))


def _fox_cumsum_kernel(ps_ref, b_ref, kmap_ref, qmap_ref, kone_ref, qone_ref, kf_ref, qf_ref, carry_ref):
    @pl.when(pl.program_id(1) == 0)
    def _():
        carry_ref[...] = jnp.zeros_like(carry_ref)

    tc = ps_ref.shape[0]
    z = ps_ref[...] + b_ref[...]
    log_f = jnp.minimum(z, 0.0) - _log1p_exp_neg_abs(z)
    row = lax.broadcasted_iota(I32, (tc, tc), 0)
    col = lax.broadcasted_iota(I32, (tc, tc), 1)
    tri = jnp.where(col <= row, 1.0, 0.0).astype(BF16)
    hi, mid, lo = _split3(log_f)
    cs = _dot(tri, hi) + _dot(tri, mid) + _dot(tri, lo) + carry_ref[...]
    carry_ref[...] = cs[tc - 1:tc, :]
    terms = _split3(cs)
    kf = kone_ref[...]
    qf = qone_ref[...]
    for j in range(3):
        kf = kf + _dot(terms[j], kmap_ref[j])
        qf = qf + _dot(terms[j], qmap_ref[j])
    kf_ref[...] = kf.astype(BF16)
    qf_ref[...] = qf.astype(BF16)


def _fox_cumsum(ps, bias_row, batch, seq):
    tc = min(256, seq)
    nj = seq // tc
    m = batch * seq
    tok = lambda w: pl.BlockSpec((tc, w), lambda b, j: (b * nj + j, 0))
    k_side, q_tok = pl.pallas_call(
        _fox_cumsum_kernel,
        grid=(batch, nj),
        in_specs=[tok(LANES), _resident((1, LANES)), _resident((3, LANES, FOX_W)), _resident((3, LANES, LANES)),
                  _resident((1, FOX_W)), _resident((1, LANES))],
        out_specs=[tok(FOX_W), tok(LANES)],
        out_shape=[jax.ShapeDtypeStruct((m, FOX_W), BF16), jax.ShapeDtypeStruct((m, LANES), BF16)],
        scratch_shapes=[pltpu.VMEM((1, LANES), F32)],
        compiler_params=_params(2),
        name="fox_cumsum",
    )(ps, bias_row, *_fox_feature_maps())
    q_side = q_tok[:, :N_FOX * FOX_FEAT_ROWS].reshape(batch, seq, N_FOX, FOX_FEAT_ROWS).transpose(0, 2, 3, 1)
    return k_side, q_side


def _head_lane_mask(hh):
    lane = lax.broadcasted_iota(I32, (1, LANES), 1)
    return (lane < HEAD_DIM) if hh == 0 else (lane >= HEAD_DIM)


def _head_row_mask(hh):
    row = lax.broadcasted_iota(I32, (LANES, 1), 0)
    return (row < HEAD_DIM) if hh == 0 else (row >= HEAD_DIM)


def _transpose_bf16(x):
    return x.astype(F32).T.astype(BF16)


def _fill_transposed(src_ref, dst_ref):
    n_blocks, _, tk = dst_ref.shape
    for j in range(n_blocks):
        dst_ref[j] = _transpose_bf16(src_ref[j * tk:(j + 1) * tk, :])


V_ROWS = HEAD_DIM + 16


def _fill_transposed_with_ones(src_ref, dst_ref, n_heads):
    n_blocks, _, tk = dst_ref.shape
    ones_pad = jnp.where(lax.broadcasted_iota(I32, (V_ROWS - HEAD_DIM, tk), 0) == 0, 1.0, 0.0).astype(BF16)
    for j in range(n_blocks):
        v_t = _transpose_bf16(src_ref[j * tk:(j + 1) * tk, :])
        for h in range(n_heads):
            dst_ref[j, h * V_ROWS:h * V_ROWS + HEAD_DIM, :] = v_t[h * HEAD_DIM:(h + 1) * HEAD_DIM, :]
            dst_ref[j, h * V_ROWS + HEAD_DIM:(h + 1) * V_ROWS, :] = ones_pad


def _softmax_steps(carry, scores, values_t):
    stats = []
    for (m, _), s in zip(carry, scores):
        m_new = jnp.maximum(m, jnp.max(s, axis=0, keepdims=True))
        stats.append((m_new, jnp.exp(m - m_new), jnp.exp(s - m_new).astype(BF16)))
    return tuple((m_new, alpha * acc + _dot(v_t, p))
                 for (m_new, alpha, p), (_, acc), v_t in zip(stats, carry, values_t))


def _softmax_init(tq):
    return (jnp.full((1, tq), NEG_BIG, F32), jnp.zeros((V_ROWS, tq), F32))


def _softmax_result(state):
    _, acc = state
    return acc[:HEAD_DIM, :] / acc[HEAD_DIM:HEAD_DIM + 1, :]


def _flash_blocks(n, score_fn, value_fn, carry, s_ref):
    n_heads = len(carry)

    def load(slot):
        return [s_ref[slot, h] for h in range(n_heads)]

    def store(slot, kb):
        for h, s in enumerate(score_fn(kb)):
            s_ref[slot, h] = s

    odd = n % 2
    carry = lax.cond(odd == 1, lambda c: _softmax_steps(c, score_fn(0), value_fn(0)), lambda c: c, carry)

    @pl.when(n >= 2)
    def _():
        store(0, odd)

    def body(j, c):
        kb = odd + 2 * j
        store(1, kb + 1)
        c = _softmax_steps(c, load(0), value_fn(kb))
        store(0, jnp.minimum(kb + 2, n - 1))
        return _softmax_steps(c, load(1), value_fn(kb + 1))

    return lax.fori_loop(0, n // 2, body, carry)


def _heads_to_token_rows(out_t):
    return jnp.concatenate(out_t, axis=0).T.astype(BF16)


def _fox_kernel(q_ref, k_ref, v_ref, kf_ref, qf_ref, o_ref, vt_ref, s_ref):
    tq = q_ref.shape[0]
    tk = vt_ref.shape[2]
    i = pl.program_id(1)

    @pl.when(i == 0)
    def _():
        _fill_transposed_with_ones(v_ref, vt_ref, N_FOX)

    key_pos = lax.broadcasted_iota(I32, (tk, tq), 0)
    qry_pos = lax.broadcasted_iota(I32, (tk, tq), 1)
    causal = [key_pos + d * tk <= qry_pos for d in range(2)]
    pad = jnp.zeros((LANES - FOX_FEAT_ROWS, tq), BF16)
    qx = []
    for p in range(FOX_W // LANES):
        q_t = q_ref[:, p * LANES:(p + 1) * LANES].astype(F32).T
        for hh in range(HEADS_PER_BLOCK):
            qx.append(jnp.concatenate([jnp.where(_head_row_mask(hh), q_t, 0.0).astype(BF16),
                                       qf_ref[0, p * HEADS_PER_BLOCK + hh], pad], axis=0))

    def store(slot, kb):
        k0 = pl.multiple_of(kb * tk, tk)
        for h in range(N_FOX):
            p = h // HEADS_PER_BLOCK
            kx = jnp.concatenate([k_ref[pl.ds(k0, tk), p * LANES:(p + 1) * LANES],
                                  kf_ref[pl.ds(k0, tk), p * LANES:(p + 1) * LANES]], axis=1)
            s_ref[slot, h] = _dot(kx, qx[h])

    def step(carry, slot, kb, mask=None):
        scores = [s_ref[slot, h] for h in range(N_FOX)]
        if mask is not None:
            scores = [jnp.where(mask, s, NEG_BIG) for s in scores]
        return _softmax_steps(carry, scores, [vt_ref[kb, h * V_ROWS:(h + 1) * V_ROWS, :] for h in range(N_FOX)])

    def trip(j, carry):
        kb = 2 * j
        store(1, kb + 1)
        carry = step(carry, 0, kb)
        store(0, kb + 2)
        return step(carry, 1, kb + 1)

    store(0, 0)
    carry = lax.fori_loop(0, i, trip, tuple(_softmax_init(tq) for _ in range(N_FOX)))
    kb = 2 * i
    store(1, kb + 1)
    carry = step(carry, 0, kb, causal[0])
    carry = step(carry, 1, kb + 1, causal[1])
    for p in range(FOX_W // LANES):
        o_ref[:, p * LANES:(p + 1) * LANES] = _heads_to_token_rows(
            [_softmax_result(st) for st in carry[p * HEADS_PER_BLOCK:(p + 1) * HEADS_PER_BLOCK]])


def _fox_attention(pb, k_side, q_side, batch, seq):
    tq = min(512, seq)
    tk = tq // 2
    nq = seq // tq
    m = batch * seq
    return pl.pallas_call(
        _fox_kernel,
        grid=(batch, nq),
        in_specs=[pl.BlockSpec((tq, FOX_W), lambda b, i: (b * nq + i, BLK_FOX_Q // 3)),
                  pl.BlockSpec((seq, FOX_W), lambda b, i: (b, BLK_FOX_K // 3)),
                  pl.BlockSpec((seq, FOX_W), lambda b, i: (b, BLK_FOX_V // 3)),
                  pl.BlockSpec((seq, FOX_W), lambda b, i: (b, 0)),
                  pl.BlockSpec((1, N_FOX, FOX_FEAT_ROWS, tq), lambda b, i: (b, 0, 0, i))],
        out_specs=pl.BlockSpec((tq, FOX_W), lambda b, i: (b * nq + i, 0)),
        out_shape=jax.ShapeDtypeStruct((m, FOX_W), BF16),
        scratch_shapes=[pltpu.VMEM((seq // tk, N_FOX * V_ROWS, tk), BF16), pltpu.VMEM((2, N_FOX, tk, tq), F32)],
        compiler_params=_params(2),
        name="fox_attention",
    )(pb, pb, pb, k_side, q_side)


def _sb_kernel(q_ref, k_ref, v_ref, o_ref, vt_ref):
    tq = q_ref.shape[0]
    tk = vt_ref.shape[2]
    i = pl.program_id(1)

    @pl.when(i == 0)
    def _():
        _fill_transposed(v_ref, vt_ref)

    heads = range(N_SB)
    q_t = [q_ref[:, p * LANES:(p + 1) * LANES].astype(F32).T for p in range(PAIR_W // LANES)]
    qh = [jnp.where(_head_row_mask(h % HEADS_PER_BLOCK), q_t[h // HEADS_PER_BLOCK], 0.0).astype(BF16) for h in heads]
    r = lax.broadcasted_iota(I32, (tk, tk), 0)
    c = lax.broadcasted_iota(I32, (tk, tk), 1)
    upper = jnp.where(c > r, 1.0, 0.0).astype(BF16)
    key_pos = lax.broadcasted_iota(I32, (tk, tq), 0)
    qry_pos = lax.broadcasted_iota(I32, (tk, tq), 1) + i * tq

    def process(kb, state, masked):
        k0 = pl.multiple_of(kb * tk, tk)
        strict = key_pos + k0 < qry_pos
        z = [_dot(k_ref[pl.ds(k0, tk), (h // HEADS_PER_BLOCK) * LANES:(h // HEADS_PER_BLOCK + 1) * LANES], qh[h])
             for h in heads]
        log_beta, log_1m = [], []
        for zh in z:
            sp = _log1p_exp_neg_abs(zh)
            log_beta.append(jnp.minimum(zh, 0.0) - sp)
            l1m = jnp.minimum(-zh, 0.0) - sp
            log_1m.append(jnp.where(strict, l1m, 0.0) if masked else l1m)
        after = []
        for h in heads:
            hi, mid, lo = _split3(log_1m[h])
            after.append(_dot(upper, hi) + _dot(upper, mid) + _dot(upper, lo) + state[h][0])
        weights = []
        for h in heads:
            a = jnp.exp(log_beta[h] + after[h])
            weights.append((jnp.where(strict, a, 0.0) if masked else a).astype(BF16))
        return tuple((state[h][0] + jnp.sum(log_1m[h], axis=0, keepdims=True),
                      state[h][1] + _dot(vt_ref[kb, h * HEAD_DIM:(h + 1) * HEAD_DIM, :], weights[h]))
                     for h in heads)

    def run_max(state):
        m = state[0][0]
        for run, _ in state[1:]:
            m = jnp.maximum(m, run)
        return jnp.max(m)

    state = tuple((jnp.zeros((1, tq), F32), jnp.zeros((HEAD_DIM, tq), F32)) for _ in heads)
    n_diag = tq // tk
    for d in range(n_diag):
        state = process((i + 1) * n_diag - 1 - d, state, True)

    def cond(carry):
        kb, worst, _ = carry
        return jnp.logical_and(kb >= 0, worst > SB_CUTOFF)

    def body(carry):
        kb, _, st = carry
        st = process(kb, st, False)
        return kb - 1, run_max(st), st

    _, _, state = lax.while_loop(cond, body, (i * n_diag - 1, run_max(state), state))
    for p in range(PAIR_W // LANES):
        outs = [state[h][1] for h in heads if h // HEADS_PER_BLOCK == p]
        outs += [jnp.zeros((HEAD_DIM, tq), F32)] * (HEADS_PER_BLOCK - len(outs))
        o_ref[:, p * LANES:(p + 1) * LANES] = _heads_to_token_rows(outs)


def _sb_attention(pb, batch, seq):
    tq = min(256, seq)
    tk = min(128, seq)
    nq = seq // tq
    m = batch * seq
    return pl.pallas_call(
        _sb_kernel,
        grid=(batch, nq),
        in_specs=[pl.BlockSpec((tq, PAIR_W), lambda b, i: (b * nq + i, BLK_SB_Q // 3)),
                  pl.BlockSpec((seq, PAIR_W), lambda b, i: (b, BLK_SB_K // 3)),
                  pl.BlockSpec((seq, PAIR_W), lambda b, i: (b, BLK_SB_V // 3))],
        out_specs=pl.BlockSpec((tq, PAIR_W), lambda b, i: (b * nq + i, 0)),
        out_shape=jax.ShapeDtypeStruct((m, PAIR_W), BF16),
        scratch_shapes=[pltpu.VMEM((seq // tk, PAIR_W, tk), BF16)],
        compiler_params=_params(2),
        name="sb_attention",
    )(pb, pb, pb)


HALF_BITS = 16
HALF_OFFSET = 1 << (HALF_BITS - 1)


def _dsa_kernel(q_ref, iq_ref, w_ref, k_ref, v_ref, ka_ref, kb_ref, o_ref, key_ref, hi_ref, lo_ref, bias_ref,
                x_ref, vt_ref, s_ref, *, topk, seq):
    t = q_ref.shape[0]
    i = pl.program_id(1)

    @pl.when(i == 0)
    def _():
        _fill_transposed_with_ones(v_ref, vt_ref, N_DSA)

    n_kb = i + 1
    kf = float(topk)
    imin = jnp.int32(INT_MIN)
    key_pos = lax.broadcasted_iota(I32, (t, t), 0)
    qry_pos = lax.broadcasted_iota(I32, (t, t), 1) + i * t

    iq_t = [_transpose_bf16(iq_ref[:, g * LANES:(g + 1) * LANES]) for g in range(N_IDX_HEADS // 2)]
    w_t = (w_ref[...] * (N_IDX_HEADS ** -0.5) * (IDX_DIM ** -0.5)).T

    def index_block(kb, _):
        k0 = pl.multiple_of(kb * t, t)
        ka = ka_ref[pl.ds(k0, t), :]
        kb_ = kb_ref[pl.ds(k0, t), :]
        score = jnp.zeros((t, t), F32)
        for g in range(N_IDX_HEADS // 2):
            lo = SMALL_IDXW_LANE + 2 * g
            score = score + jnp.maximum(_dot(ka, iq_t[g]), 0.0) * w_t[lo:lo + 1, :]
            score = score + jnp.maximum(_dot(kb_, iq_t[g]), 0.0) * w_t[lo + 1:lo + 2, :]
        bits = lax.bitcast_convert_type(score + 0.0, I32)
        key = jnp.where(bits < 0, bits ^ jnp.int32(0x7FFFFFFF), bits)
        key = jnp.where(key_pos + k0 <= qry_pos, key, imin)
        key_ref[pl.ds(k0, t), :] = key
        hi_ref[pl.ds(k0, t), :] = lax.shift_right_arithmetic(key, HALF_BITS).astype(jnp.int16)
        lo_ref[pl.ds(k0, t), :] = ((key & (2 * HALF_OFFSET - 1)) - HALF_OFFSET).astype(jnp.int16)
        return 0

    lax.fori_loop(0, n_kb, index_block, 0)

    def count16(ref, pred):
        rows = 16
        def body(kb, acc):
            k0 = pl.multiple_of(kb * t, t)
            ind = pred(ref[pl.ds(k0, t), :])
            parts = [ind[r:r + rows, :] for r in range(0, t, rows)]
            while len(parts) > 1:
                parts = [parts[j] + parts[j + 1] for j in range(0, len(parts), 2)]
            return acc + parts[0]
        acc = lax.fori_loop(0, n_kb, body, jnp.zeros((rows, t), jnp.int16))
        return jnp.sum(acc.astype(F32), axis=0, keepdims=True)

    one16, zero16 = jnp.int16(1), jnp.int16(0)

    def kth_half(ref, need):
        def bit_step(it, prefix):
            cand_u = prefix | lax.shift_left(jnp.int32(1), HALF_BITS - 1 - it)
            cand = (cand_u - HALF_OFFSET).astype(jnp.int16)
            cnt = count16(ref, lambda blk: jnp.where(blk >= cand, one16, zero16))
            return jnp.where(cnt >= need, cand_u, prefix)
        return lax.fori_loop(0, HALF_BITS, bit_step, jnp.zeros((1, t), I32)) - HALF_OFFSET

    def count(pred):
        def body(kb, acc):
            k0 = pl.multiple_of(kb * t, t)
            ind = pred(key_ref[pl.ds(k0, t), :], k0)
            parts = [ind[r:r + 8, :] for r in range(0, t, 8)]
            while len(parts) > 1:
                parts = [parts[j] + parts[j + 1] for j in range(0, len(parts), 2)]
            return acc + parts[0]
        acc = lax.fori_loop(0, n_kb, body, jnp.zeros((8, t), F32))
        return jnp.sum(acc, axis=0, keepdims=True)

    kth_hi = kth_half(hi_ref, kf)
    kth_hi16 = kth_hi.astype(jnp.int16)
    need_lo = kf - count16(hi_ref, lambda blk: jnp.where(blk > kth_hi16, one16, zero16))

    def keep_bucket(kb, _):
        k0 = pl.multiple_of(kb * t, t)
        lo_ref[pl.ds(k0, t), :] = jnp.where(hi_ref[pl.ds(k0, t), :] == kth_hi16, lo_ref[pl.ds(k0, t), :],
                                            jnp.int16(-HALF_OFFSET))
        return 0

    lax.fori_loop(0, n_kb, keep_bucket, 0)
    kth_lo = kth_half(lo_ref, need_lo)
    kth = lax.shift_left(kth_hi, HALF_BITS) + (kth_lo + HALF_OFFSET)
    n_gt = count(lambda kblk, k0: jnp.where(kblk > kth, 1.0, 0.0))
    n_ge = count(lambda kblk, k0: jnp.where(kblk >= kth, 1.0, 0.0))
    need = kf - n_gt

    x_ref[...] = jnp.full((1, t), seq, I32)

    @pl.when(jnp.max(n_ge) > kf)
    def _():
        n_bits = max(1, (seq - 1).bit_length())

        def tie_step(it, x):
            cand = x | lax.shift_left(jnp.int32(1), n_bits - 1 - it)
            g = count(lambda kblk, k0: jnp.where(
                kblk == kth, jnp.where(key_pos + k0 < cand, 1.0, 0.0), 0.0))
            return jnp.where(g < need, cand, x)

        x_ref[...] = lax.fori_loop(0, n_bits, tie_step, jnp.zeros((1, t), I32))

    x = jnp.where(kth == imin, -1, x_ref[...])

    def select(kb, _):
        k0 = pl.multiple_of(kb * t, t)
        kblk = key_ref[pl.ds(k0, t), :]
        tie = jnp.where(kblk == kth, jnp.where(key_pos + k0 <= x, 0.0, NEG_BIG), NEG_BIG)
        bias_ref[pl.ds(k0, t), :] = jnp.where(kblk > kth, 0.0, tie)
        return 0

    lax.fori_loop(0, n_kb, select, 0)

    q_t = [q_ref[:, p * LANES:(p + 1) * LANES].astype(F32).T for p in range(PAIR_W // LANES)]
    qh = [jnp.where(_head_row_mask(h % HEADS_PER_BLOCK), q_t[h // HEADS_PER_BLOCK], 0.0).astype(BF16)
          for h in range(N_DSA)]

    def scores(kb):
        k0 = pl.multiple_of(kb * t, t)
        bias = bias_ref[pl.ds(k0, t), :]
        return [_dot(k_ref[pl.ds(k0, t), (h // HEADS_PER_BLOCK) * LANES:(h // HEADS_PER_BLOCK + 1) * LANES], qh[h])
                + bias for h in range(N_DSA)]

    def values_t(kb):
        return [vt_ref[kb, h * V_ROWS:(h + 1) * V_ROWS, :] for h in range(N_DSA)]

    carry = _flash_blocks(n_kb, scores, values_t, tuple(_softmax_init(t) for _ in range(N_DSA)), s_ref)
    outs = [_softmax_result(st) for st in carry]
    outs += [jnp.zeros((HEAD_DIM, t), F32)] * (PAIR_W // HEAD_DIM - N_DSA)
    for p in range(PAIR_W // LANES):
        o_ref[:, p * LANES:(p + 1) * LANES] = _heads_to_token_rows(
            outs[p * HEADS_PER_BLOCK:(p + 1) * HEADS_PER_BLOCK])


def _dsa_attention(pb, ps, batch, seq):
    t = min(256, seq)
    nq = seq // t
    m = batch * seq
    topk = min(TOPK_MAX, seq // 4)
    return pl.pallas_call(
        functools.partial(_dsa_kernel, topk=topk, seq=seq),
        grid=(batch, nq),
        in_specs=[pl.BlockSpec((t, PAIR_W), lambda b, i: (b * nq + i, BLK_DSA_Q // 3)),
                  pl.BlockSpec((t, 4 * LANES), lambda b, i: (b * nq + i, BLK_IDX_Q // 4)),
                  pl.BlockSpec((t, LANES), lambda b, i: (b * nq + i, 0)),
                  pl.BlockSpec((seq, PAIR_W), lambda b, i: (b, BLK_DSA_K // 3)),
                  pl.BlockSpec((seq, PAIR_W), lambda b, i: (b, BLK_DSA_V // 3)),
                  pl.BlockSpec((seq, LANES), lambda b, i: (b, BLK_IDX_KA)),
                  pl.BlockSpec((seq, LANES), lambda b, i: (b, BLK_IDX_KB))],
        out_specs=pl.BlockSpec((t, PAIR_W), lambda b, i: (b * nq + i, 0)),
        out_shape=jax.ShapeDtypeStruct((m, PAIR_W), BF16),
        scratch_shapes=[pltpu.VMEM((seq, t), I32), pltpu.VMEM((seq, t), jnp.int16), pltpu.VMEM((seq, t), jnp.int16),
                        pltpu.VMEM((seq, t), F32), pltpu.VMEM((1, t), I32),
                        pltpu.VMEM((nq, N_DSA * V_ROWS, t), BF16), pltpu.VMEM((2, N_DSA, t, t), F32)],
        compiler_params=_params(2),
        name="dsa_attention",
    )(pb, pb, ps, pb, pb, pb, pb)


def _mixer_out_kernel(x_ref, of_ref, od_ref, os_ref, g_ref, wf_ref, wd_ref, ws_ref, wo_ref, o_ref):
    d = x_ref.shape[1]
    merged = jax.nn.sigmoid(g_ref[:, 0:d]) * _dot(of_ref[...], wf_ref[...])
    merged = merged + jax.nn.sigmoid(g_ref[:, d:2 * d]) * _dot(od_ref[...], wd_ref[...])
    merged = merged + jax.nn.sigmoid(g_ref[:, 2 * d:3 * d]) * _dot(os_ref[...], ws_ref[...])
    o_ref[...] = x_ref[...] + _dot(merged.astype(BF16), wo_ref[...])


def _mixer_out(x2, o_fox, o_dsa, o_sb, gates, wf, wd, ws, wo):
    m, d = x2.shape
    tm = min(512, m)
    row = lambda w: pl.BlockSpec((tm, w), lambda i: (i, 0))
    return pl.pallas_call(
        _mixer_out_kernel,
        grid=(m // tm,),
        in_specs=[row(d), row(FOX_W), row(PAIR_W), row(PAIR_W), row(N_BRANCH * d),
                  _resident((FOX_W, d)), _resident((PAIR_W, d)), _resident((PAIR_W, d)), _resident((d, d))],
        out_specs=row(d),
        out_shape=jax.ShapeDtypeStruct((m, d), F32),
        compiler_params=_params(1),
        name="mixer_out",
    )(x2, o_fox, o_dsa, o_sb, gates, wf, wd, ws, wo)


def _norm_matmul_kernel(x_ref, g_ref, w_ref, o_ref):
    o_ref[...] = _dot(_rms(x_ref[...], g_ref[...]).astype(BF16), w_ref[...]).astype(o_ref.dtype)


def _norm_matmul(x2, gain, w, out_dtype):
    m, d = x2.shape
    n = w.shape[1]
    tm = min(512, m)
    return pl.pallas_call(
        _norm_matmul_kernel,
        grid=(m // tm,),
        in_specs=[pl.BlockSpec((tm, d), lambda i: (i, 0)), _resident((1, d)), _resident((d, n))],
        out_specs=pl.BlockSpec((tm, n), lambda i: (i, 0)),
        out_shape=jax.ShapeDtypeStruct((m, n), out_dtype),
        compiler_params=_params(1),
        name="norm_matmul",
    )(x2, gain.reshape(1, d), w)


def _cross_kernel(x_ref, g_ref, wq_ref, kv_ref, wo_ref, o_ref):
    x = x_ref[...]
    h = _rms(x, g_ref[...]).astype(BF16)
    q = _dot(h, wq_ref[...])
    scale = CA_HEAD_DIM ** -0.5
    outs = []
    for hd in range(N_CA_HEADS):
        qh = q[:, hd * CA_HEAD_DIM:(hd + 1) * CA_HEAD_DIM].astype(BF16)
        logits = _dot_nt(qh, kv_ref[:, hd * CA_HEAD_DIM:(hd + 1) * CA_HEAD_DIM]) * scale
        e = jnp.exp(logits - jnp.max(logits, axis=1, keepdims=True))
        pv = _dot(e.astype(BF16), kv_ref[:, CA_W + hd * CA_HEAD_DIM:CA_W + (hd + 1) * CA_HEAD_DIM])
        outs.append(pv / jnp.sum(e, axis=1, keepdims=True))
    o = jnp.concatenate(outs, axis=1).astype(BF16)
    o_ref[...] = x + _dot(o, wo_ref[...])


def _cross_attention(x2, gain, wq, kv, wo, batch, seq):
    m, d = x2.shape
    tm = min(512, seq)
    n_mem = kv.shape[0] // batch
    per_batch = seq // tm
    return pl.pallas_call(
        _cross_kernel,
        grid=(m // tm,),
        in_specs=[pl.BlockSpec((tm, d), lambda i: (i, 0)), _resident((1, d)), _resident((d, CA_W)),
                  pl.BlockSpec((n_mem, 2 * CA_W), lambda i: (i // per_batch, 0)), _resident((CA_W, d))],
        out_specs=pl.BlockSpec((tm, d), lambda i: (i, 0)),
        out_shape=jax.ShapeDtypeStruct((m, d), F32),
        compiler_params=_params(1),
        name="cross_attention",
    )(x2, gain.reshape(1, d), wq, kv, wo)


def _pad_cols(w, n):
    return jnp.pad(w, ((0, 0), (0, n - w.shape[1])))


def _pad_rows(w, n):
    return jnp.pad(w, ((0, n - w.shape[0]), (0, 0)))


def _mixer_weights(w_in):
    offs = np.cumsum((0,) + IN_SPLITS)
    part = lambda j: w_in[:, offs[j]:offs[j + 1]]
    thirds = lambda w: jnp.split(w, 3, axis=1)
    scale = HEAD_DIM ** -0.5
    fq, fk, fv = thirds(part(0))
    dq, dk, dv = thirds(part(2))
    sq, sk, sv = thirds(part(6))
    iq, ik, iw = part(3), part(4), part(5)
    zk = jnp.zeros_like(ik)
    small = jnp.concatenate([part(1), jnp.zeros((w_in.shape[0], SMALL_IDXW_LANE - N_FOX), w_in.dtype), iw], axis=1)
    wb = jnp.concatenate(
        [_pad_cols(dq * scale, PAIR_W), _pad_cols(dk, PAIR_W),
         jnp.concatenate([ik, zk], axis=1), jnp.concatenate([zk, ik], axis=1), iq,
         fq * scale, fk, fv, _pad_cols(dv, PAIR_W),
         _pad_cols(sq * scale, PAIR_W), _pad_cols(sk, PAIR_W), _pad_cols(sv, PAIR_W),
         _pad_cols(small, LANES)], axis=1).astype(BF16)
    return wb, part(7).astype(BF16)


def _hybrid_mixer(x2, cos, sin_signed, batch, seq, gain, w_in, b_fgate, w_fox_out, w_dsa_out, w_sb_out, w_out):
    wb, wg = _mixer_weights(w_in)
    pb, gates, ps = _mixer_in(x2, gain, cos, sin_signed, wb, wg)

    bias_row = _pad_cols(b_fgate.reshape(1, N_FOX).astype(F32), LANES)
    k_side, q_side = _fox_cumsum(ps, bias_row, batch, seq)

    o_fox = _fox_attention(pb, k_side, q_side, batch, seq)
    o_dsa = _dsa_attention(pb, ps, batch, seq)
    o_sb = _sb_attention(pb, batch, seq)
    return _mixer_out(x2, o_fox, o_dsa, o_sb, gates, w_fox_out.astype(BF16),
                      _pad_rows(w_dsa_out, PAIR_W).astype(BF16), _pad_rows(w_sb_out, PAIR_W).astype(BF16),
                      w_out.astype(BF16))


def kernel(x, mem, positions, ffn1_norm, ffn1_w_gu, ffn1_w_down, mix_norm, w_in, b_fgate, w_fox_out, w_dsa_out,
           w_sb_out, w_out, ca_norm, mem_norm, ca_w_q, ca_w_kv, ca_w_o, ffn2_norm, ffn2_w_gu, ffn2_w_down,
           final_norm):
    batch, seq, d = x.shape
    depth = ffn1_norm.shape[0]
    x2 = x.reshape(batch * seq, d)
    mem2 = mem.reshape(-1, d)
    pos2 = positions.reshape(batch * seq, 1).astype(I32)
    half = HEAD_DIM // 2
    inv_freq = jnp.power(ROPE_THETA, -jnp.arange(half, dtype=F32) * (2.0 / HEAD_DIM))
    invf = jnp.tile(inv_freq, LANES // half).reshape(1, LANES)
    cos, sin_signed = _rope_tables(pos2, invf)
    for l in range(depth):
        x2 = _ffn(x2, ffn1_norm[l], ffn1_w_gu[l], ffn1_w_down[l])
        x2 = _hybrid_mixer(x2, cos, sin_signed, batch, seq, mix_norm[l], w_in[l], b_fgate[l], w_fox_out[l],
                           w_dsa_out[l], w_sb_out[l], w_out[l])
        kv = _norm_matmul(mem2, mem_norm[l], ca_w_kv[l].astype(BF16), BF16)
        x2 = _cross_attention(x2, ca_norm[l], ca_w_q[l].astype(BF16), kv, ca_w_o[l].astype(BF16), batch, seq)
        x2 = _ffn(x2, ffn2_norm[l], ffn2_w_gu[l], ffn2_w_down[l],
                  final_gain=final_norm if l == depth - 1 else None)
    return x2.reshape(batch, seq, d)
```

```python
import functools

import jax
import jax.numpy as jnp
import numpy as np
from jax import lax
from jax.experimental import pallas as pl
from jax.experimental.pallas import tpu as pltpu

F32 = jnp.float32
BF16 = jnp.bfloat16
I32 = jnp.int32

D_MODEL = 1024
HEAD_DIM = 64
N_FOX = 6
N_DSA = 5
N_SB = 5
N_IDX_HEADS = 8
IDX_DIM = 64
TOPK_MAX = 256
N_CA_HEADS = 4
CA_HEAD_DIM = 128
D_FF = 2816
ROPE_THETA = 10000.0
NORM_EPS = 1e-6
N_BRANCH = 3
HALF_STEP = 0.5
FOX_W = N_FOX * HEAD_DIM
DSA_W = N_DSA * HEAD_DIM
SB_W = N_SB * HEAD_DIM
CA_W = N_CA_HEADS * CA_HEAD_DIM
IN_SPLITS = (3 * FOX_W, N_FOX, 3 * DSA_W, N_IDX_HEADS * IDX_DIM, IDX_DIM, N_IDX_HEADS, 3 * SB_W,
             N_BRANCH * D_MODEL)

LANES = 128
HEADS_PER_BLOCK = LANES // HEAD_DIM
PAIR_W = 3 * LANES
VMEM_LIMIT = 56 * 2**20

BLK_DSA_Q, BLK_DSA_K = 0, 3
BLK_IDX_KA = 6
BLK_IDX_KB = 7
BLK_IDX_Q = 8
N_ROPE_BLOCKS = 12
BLK_FOX_Q, BLK_FOX_K, BLK_FOX_V = 12, 15, 18
BLK_DSA_V = 21
BLK_SB_Q, BLK_SB_K, BLK_SB_V = 24, 27, 30
N_PROJ_BLOCKS = 33
PROJ_CHUNK = 4 * LANES
SMALL_FGATE_LANE = 0
SMALL_IDXW_LANE = 8

NEG_BIG = -1e30
SB_CUTOFF = -110.0
INT_MIN = -2**31


def _params(n_grid):
    return pltpu.CompilerParams(dimension_semantics=("arbitrary",) * n_grid, vmem_limit_bytes=VMEM_LIMIT)


def _resident(shape):
    nd = len(shape)
    return pl.BlockSpec(shape, lambda *_: (0,) * nd, pipeline_mode=pl.Buffered(1))


def _rms(x, g):
    return x * lax.rsqrt(jnp.mean(x * x, axis=-1, keepdims=True) + NORM_EPS) * g


def _dot(a, b):
    return jnp.dot(a, b, preferred_element_type=F32)


def _dot_nt(a, b):
    return lax.dot_general(a, b, (((1,), (1,)), ((), ())), preferred_element_type=F32)


def _split3(x):
    hi = x.astype(BF16)
    r = x - hi.astype(F32)
    mid = r.astype(BF16)
    lo = (r - mid.astype(F32)).astype(BF16)
    return hi, mid, lo


def _log1p_exp_neg_abs(z):
    return jnp.log1p(jnp.exp(-jnp.abs(z)))


def _ffn_chunks():
    out, c = [], 0
    while c < D_FF:
        w = min(512, D_FF - c)
        out.append((c, w))
        c += w
    return tuple(out)


def _ffn_kernel(*refs, final):
    if final:
        x_ref, g_ref, wgu_ref, wd_ref, fn_ref, o_ref, a_ref = refs
    else:
        x_ref, g_ref, wgu_ref, wd_ref, o_ref, a_ref = refs
    x = x_ref[...]
    h = _rms(x, g_ref[...]).astype(BF16)
    for c0, w in _ffn_chunks():
        g = _dot(h, wgu_ref[:, c0:c0 + w])
        u = _dot(h, wgu_ref[:, D_FF + c0:D_FF + c0 + w])
        a_ref[:, c0:c0 + w] = (g * jax.nn.sigmoid(g) * u).astype(BF16)
    y = x + HALF_STEP * _dot(a_ref[...], wd_ref[...])
    if final:
        y = _rms(y, fn_ref[...])
    o_ref[...] = y


def _ffn(x2, gain, w_gu, w_down, final_gain=None):
    m, d = x2.shape
    tm = min(512, m)
    final = final_gain is not None
    in_specs = [pl.BlockSpec((tm, d), lambda i: (i, 0)), _resident((1, d)),
                _resident((d, 2 * D_FF)), _resident((D_FF, d))]
    args = [x2, gain.reshape(1, d), w_gu.astype(BF16), w_down.astype(BF16)]
    if final:
        in_specs.append(_resident((1, d)))
        args.append(final_gain.reshape(1, d))
    return pl.pallas_call(
        functools.partial(_ffn_kernel, final=final),
        grid=(m // tm,),
        in_specs=in_specs,
        out_specs=pl.BlockSpec((tm, d), lambda i: (i, 0)),
        out_shape=jax.ShapeDtypeStruct((m, d), F32),
        scratch_shapes=[pltpu.VMEM((tm, D_FF), BF16)],
        compiler_params=_params(1),
        name="ffn",
    )(*args)


def _first_half_lanes():
    lane = lax.broadcasted_iota(I32, (1, LANES), 1)
    return (lane % HEAD_DIM) < (HEAD_DIM // 2)


def _rope_tables_kernel(pos_ref, invf_ref, cos_ref, sin_ref):
    ang = pos_ref[...].astype(F32) * invf_ref[...]
    sin = jnp.sin(ang)
    cos_ref[...] = jnp.cos(ang)
    sin_ref[...] = jnp.where(_first_half_lanes(), -sin, sin)


def _rope_tables(pos2, invf):
    m = pos2.shape[0]
    tm = min(512, m)
    tok = pl.BlockSpec((tm, LANES), lambda i: (i, 0))
    return pl.pallas_call(
        _rope_tables_kernel,
        grid=(m // tm,),
        in_specs=[pl.BlockSpec((tm, 1), lambda i: (i, 0)), _resident((1, LANES))],
        out_specs=[tok, tok],
        out_shape=[jax.ShapeDtypeStruct((m, LANES), F32)] * 2,
        compiler_params=_params(1),
        name="rope_tables",
    )(pos2, invf)


def _mixer_in_kernel(x_ref, g_ref, cos_ref, sin_ref, wb_ref, pb_ref, ps_ref):
    h = _rms(x_ref[...], g_ref[...]).astype(BF16)
    cos = cos_ref[...]
    sin_signed = sin_ref[...]
    first_half = _first_half_lanes()
    n_cols = (N_PROJ_BLOCKS + 1) * LANES
    for c0 in range(0, n_cols, PROJ_CHUNK):
        c1 = min(c0 + PROJ_CHUNK, n_cols)
        res = _dot(h, wb_ref[:, c0:c1])
        if c1 == n_cols:
            ps_ref[...] = res[:, c1 - c0 - LANES:]
            c1 -= LANES
            res = res[:, :c1 - c0]
        if c0 < N_ROPE_BLOCKS * LANES:
            parts = []
            for b in range((c1 - c0) // LANES):
                xb = res[:, b * LANES:(b + 1) * LANES]
                partner = jnp.where(first_half, pltpu.roll(xb, LANES - HEAD_DIM // 2, 1),
                                    pltpu.roll(xb, HEAD_DIM // 2, 1))
                parts.append(xb * cos + partner * sin_signed)
            res = jnp.concatenate(parts, axis=1)
        pb_ref[:, c0:c1] = res.astype(BF16)


def _mixer_in(x2, gain, cos, sin_signed, wb):
    m, d = x2.shape
    tm = min(512, m)
    nb = N_PROJ_BLOCKS * LANES
    return pl.pallas_call(
        _mixer_in_kernel,
        grid=(m // tm,),
        in_specs=[pl.BlockSpec((tm, d), lambda i: (i, 0)), _resident((1, d)),
                  pl.BlockSpec((tm, LANES), lambda i: (i, 0)), pl.BlockSpec((tm, LANES), lambda i: (i, 0)),
                  _resident((d, nb + LANES))],
        out_specs=[pl.BlockSpec((tm, nb), lambda i: (i, 0)), pl.BlockSpec((tm, LANES), lambda i: (i, 0))],
        out_shape=[jax.ShapeDtypeStruct((m, nb), BF16), jax.ShapeDtypeStruct((m, LANES), F32)],
        compiler_params=_params(1),
        name="mixer_in",
    )(x2, gain.reshape(1, d), cos, sin_signed, wb)


FOX_FEAT_ROWS = 16


def _fox_feature_maps():
    k_map = np.zeros((3, LANES, FOX_W), np.float32)
    q_map = np.zeros((3, LANES, LANES), np.float32)
    k_one = np.zeros((1, FOX_W), np.float32)
    q_one = np.zeros((1, LANES), np.float32)
    for h in range(N_FOX):
        p, hh = divmod(h, HEADS_PER_BLOCK)
        for j in range(3):
            k_map[j, h, p * LANES + 3 * hh + j] = -1.0
            q_map[j, h, FOX_FEAT_ROWS * h + 6 + j] = 1.0
            k_one[0, p * LANES + 6 + j] = 1.0
            q_one[0, FOX_FEAT_ROWS * h + 3 * hh + j] = 1.0
    return (jnp.asarray(k_map, BF16), jnp.asarray(q_map, BF16), jnp.asarray(k_one), jnp.asarray(q_one))


def _fox_cumsum_kernel(ps_ref, b_ref, kmap_ref, qmap_ref, kone_ref, qone_ref, kf_ref, qf_ref, carry_ref):
    @pl.when(pl.program_id(1) == 0)
    def _():
        carry_ref[...] = jnp.zeros_like(carry_ref)

    tc = ps_ref.shape[0]
    z = ps_ref[...] + b_ref[...]
    log_f = jnp.minimum(z, 0.0) - _log1p_exp_neg_abs(z)
    row = lax.broadcasted_iota(I32, (tc, tc), 0)
    col = lax.broadcasted_iota(I32, (tc, tc), 1)
    tri = jnp.where(col <= row, 1.0, 0.0).astype(BF16)
    hi, mid, lo = _split3(log_f)
    cs = _dot(tri, hi) + _dot(tri, mid) + _dot(tri, lo) + carry_ref[...]
    carry_ref[...] = cs[tc - 1:tc, :]
    terms = _split3(cs)
    kf = kone_ref[...]
    qf = qone_ref[...]
    for j in range(3):
        kf = kf + _dot(terms[j], kmap_ref[j])
        qf = qf + _dot(terms[j], qmap_ref[j])
    kf_ref[...] = kf.astype(BF16)
    qf_ref[...] = qf.astype(BF16)


def _fox_cumsum(ps, bias_row, batch, seq):
    tc = min(512, seq)
    nj = seq // tc
    m = batch * seq
    tok = lambda w: pl.BlockSpec((tc, w), lambda b, j: (b * nj + j, 0))
    k_side, q_tok = pl.pallas_call(
        _fox_cumsum_kernel,
        grid=(batch, nj),
        in_specs=[tok(LANES), _resident((1, LANES)), _resident((3, LANES, FOX_W)), _resident((3, LANES, LANES)),
                  _resident((1, FOX_W)), _resident((1, LANES))],
        out_specs=[tok(FOX_W), tok(LANES)],
        out_shape=[jax.ShapeDtypeStruct((m, FOX_W), BF16), jax.ShapeDtypeStruct((m, LANES), BF16)],
        scratch_shapes=[pltpu.VMEM((1, LANES), F32)],
        compiler_params=_params(2),
        name="fox_cumsum",
    )(ps, bias_row, *_fox_feature_maps())
    q_side = q_tok[:, :N_FOX * FOX_FEAT_ROWS].reshape(batch, seq, N_FOX, FOX_FEAT_ROWS).transpose(0, 2, 3, 1)
    return k_side, q_side


def _head_lane_mask(hh):
    lane = lax.broadcasted_iota(I32, (1, LANES), 1)
    return (lane < HEAD_DIM) if hh == 0 else (lane >= HEAD_DIM)


def _head_row_mask(hh):
    row = lax.broadcasted_iota(I32, (LANES, 1), 0)
    return (row < HEAD_DIM) if hh == 0 else (row >= HEAD_DIM)


def _transpose_bf16(x):
    return x.astype(F32).T.astype(BF16)


def _fill_transposed(src_ref, dst_ref):
    n_blocks, _, tk = dst_ref.shape
    for j in range(n_blocks):
        dst_ref[j] = _transpose_bf16(src_ref[j * tk:(j + 1) * tk, :])


V_ROWS = HEAD_DIM + 16


def _fill_transposed_with_ones(src_ref, dst_ref, n_heads):
    n_blocks, _, tk = dst_ref.shape
    ones_pad = jnp.where(lax.broadcasted_iota(I32, (V_ROWS - HEAD_DIM, tk), 0) == 0, 1.0, 0.0).astype(BF16)
    for j in range(n_blocks):
        v_t = _transpose_bf16(src_ref[j * tk:(j + 1) * tk, :])
        for h in range(n_heads):
            dst_ref[j, h * V_ROWS:h * V_ROWS + HEAD_DIM, :] = v_t[h * HEAD_DIM:(h + 1) * HEAD_DIM, :]
            dst_ref[j, h * V_ROWS + HEAD_DIM:(h + 1) * V_ROWS, :] = ones_pad


def _softmax_steps(carry, scores, values_t):
    stats = []
    for (m, _), s in zip(carry, scores):
        m_new = jnp.maximum(m, jnp.max(s, axis=0, keepdims=True))
        stats.append((m_new, jnp.exp(m - m_new), jnp.exp(s - m_new).astype(BF16)))
    return tuple((m_new, alpha * acc + _dot(v_t, p))
                 for (m_new, alpha, p), (_, acc), v_t in zip(stats, carry, values_t))


def _softmax_init(tq):
    return (jnp.full((1, tq), NEG_BIG, F32), jnp.zeros((V_ROWS, tq), F32))


def _softmax_result(state):
    _, acc = state
    return acc[:HEAD_DIM, :] / acc[HEAD_DIM:HEAD_DIM + 1, :]


def _flash_blocks(n, score_fn, value_fn, carry, s_ref):
    n_heads = len(carry)

    def load(slot):
        return [s_ref[slot, h] for h in range(n_heads)]

    def store(slot, kb):
        for h, s in enumerate(score_fn(kb)):
            s_ref[slot, h] = s

    odd = n % 2
    carry = lax.cond(odd == 1, lambda c: _softmax_steps(c, score_fn(0), value_fn(0)), lambda c: c, carry)

    @pl.when(n >= 2)
    def _():
        store(0, odd)

    def body(j, c):
        kb = odd + 2 * j
        store(1, kb + 1)
        c = _softmax_steps(c, load(0), value_fn(kb))
        store(0, jnp.minimum(kb + 2, n - 1))
        return _softmax_steps(c, load(1), value_fn(kb + 1))

    return lax.fori_loop(0, n // 2, body, carry)


def _heads_to_token_rows(out_t):
    return jnp.concatenate(out_t, axis=0).T.astype(BF16)


def _fox_kernel(q_ref, k_ref, v_ref, kf_ref, qf_ref, o_ref, vt_ref, s_ref):
    tq = q_ref.shape[0]
    tk = vt_ref.shape[2]
    i = pl.program_id(1)

    @pl.when(i == 0)
    def _():
        _fill_transposed_with_ones(v_ref, vt_ref, N_FOX)

    key_pos = lax.broadcasted_iota(I32, (tk, tq), 0)
    qry_pos = lax.broadcasted_iota(I32, (tk, tq), 1)
    causal = [key_pos + d * tk <= qry_pos for d in range(2)]
    pad = jnp.zeros((LANES - FOX_FEAT_ROWS, tq), BF16)
    qx = []
    for p in range(FOX_W // LANES):
        q_t = q_ref[:, p * LANES:(p + 1) * LANES].astype(F32).T
        for hh in range(HEADS_PER_BLOCK):
            qx.append(jnp.concatenate([jnp.where(_head_row_mask(hh), q_t, 0.0).astype(BF16),
                                       qf_ref[0, p * HEADS_PER_BLOCK + hh], pad], axis=0))

    def store(slot, kb):
        k0 = pl.multiple_of(kb * tk, tk)
        for h in range(N_FOX):
            p = h // HEADS_PER_BLOCK
            kx = jnp.concatenate([k_ref[pl.ds(k0, tk), p * LANES:(p + 1) * LANES],
                                  kf_ref[pl.ds(k0, tk), p * LANES:(p + 1) * LANES]], axis=1)
            s_ref[slot, h] = _dot(kx, qx[h])

    def step(carry, slot, kb, mask=None):
        scores = [s_ref[slot, h] for h in range(N_FOX)]
        if mask is not None:
            scores = [jnp.where(mask, s, NEG_BIG) for s in scores]
        return _softmax_steps(carry, scores, [vt_ref[kb, h * V_ROWS:(h + 1) * V_ROWS, :] for h in range(N_FOX)])

    def trip(j, carry):
        kb = 2 * j
        store(1, kb + 1)
        carry = step(carry, 0, kb)
        store(0, kb + 2)
        return step(carry, 1, kb + 1)

    store(0, 0)
    carry = lax.fori_loop(0, i, trip, tuple(_softmax_init(tq) for _ in range(N_FOX)))
    kb = 2 * i
    store(1, kb + 1)
    carry = step(carry, 0, kb, causal[0])
    carry = step(carry, 1, kb + 1, causal[1])
    for p in range(FOX_W // LANES):
        o_ref[:, p * LANES:(p + 1) * LANES] = _heads_to_token_rows(
            [_softmax_result(st) for st in carry[p * HEADS_PER_BLOCK:(p + 1) * HEADS_PER_BLOCK]])


def _fox_attention(pb, k_side, q_side, batch, seq):
    tq = min(512, seq)
    tk = tq // 2
    nq = seq // tq
    m = batch * seq
    return pl.pallas_call(
        _fox_kernel,
        grid=(batch, nq),
        in_specs=[pl.BlockSpec((tq, FOX_W), lambda b, i: (b * nq + i, BLK_FOX_Q // 3)),
                  pl.BlockSpec((seq, FOX_W), lambda b, i: (b, BLK_FOX_K // 3)),
                  pl.BlockSpec((seq, FOX_W), lambda b, i: (b, BLK_FOX_V // 3)),
                  pl.BlockSpec((seq, FOX_W), lambda b, i: (b, 0)),
                  pl.BlockSpec((1, N_FOX, FOX_FEAT_ROWS, tq), lambda b, i: (b, 0, 0, i))],
        out_specs=pl.BlockSpec((tq, FOX_W), lambda b, i: (b * nq + i, 0)),
        out_shape=jax.ShapeDtypeStruct((m, FOX_W), BF16),
        scratch_shapes=[pltpu.VMEM((seq // tk, N_FOX * V_ROWS, tk), BF16), pltpu.VMEM((2, N_FOX, tk, tq), F32)],
        compiler_params=_params(2),
        name="fox_attention",
    )(pb, pb, pb, k_side, q_side)


def _sb_kernel(q_ref, k_ref, v_ref, o_ref, vt_ref):
    tq = q_ref.shape[0]
    tk = vt_ref.shape[2]
    i = pl.program_id(1)

    @pl.when(i == 0)
    def _():
        _fill_transposed(v_ref, vt_ref)

    heads = range(N_SB)
    q_t = [q_ref[:, p * LANES:(p + 1) * LANES].astype(F32).T for p in range(PAIR_W // LANES)]
    qh = [jnp.where(_head_row_mask(h % HEADS_PER_BLOCK), q_t[h // HEADS_PER_BLOCK], 0.0).astype(BF16) for h in heads]
    r = lax.broadcasted_iota(I32, (tk, tk), 0)
    c = lax.broadcasted_iota(I32, (tk, tk), 1)
    upper = jnp.where(c > r, 1.0, 0.0).astype(BF16)
    key_pos = lax.broadcasted_iota(I32, (tk, tq), 0)
    qry_pos = lax.broadcasted_iota(I32, (tk, tq), 1) + i * tq

    def process(kb, state, masked):
        k0 = pl.multiple_of(kb * tk, tk)
        strict = key_pos + k0 < qry_pos
        z = [_dot(k_ref[pl.ds(k0, tk), (h // HEADS_PER_BLOCK) * LANES:(h // HEADS_PER_BLOCK + 1) * LANES], qh[h])
             for h in heads]
        log_beta, log_1m = [], []
        for zh in z:
            l1m = jnp.minimum(-zh, 0.0) - jnp.log(1.0 + jnp.exp(-jnp.abs(zh)))
            log_beta.append(l1m + zh)
            log_1m.append(jnp.where(strict, l1m, 0.0) if masked else l1m)
        after = []
        for h in heads:
            hi, mid, lo = _split3(log_1m[h])
            after.append(_dot(upper, hi) + _dot(upper, mid) + _dot(upper, lo) + state[h][0])
        weights = []
        for h in heads:
            a = jnp.exp(log_beta[h] + after[h])
            weights.append((jnp.where(strict, a, 0.0) if masked else a).astype(BF16))
        return tuple((state[h][0] + jnp.sum(log_1m[h], axis=0, keepdims=True),
                      state[h][1] + _dot(vt_ref[kb, h * HEAD_DIM:(h + 1) * HEAD_DIM, :], weights[h]))
                     for h in heads)

    def run_max(state):
        m = state[0][0]
        for run, _ in state[1:]:
            m = jnp.maximum(m, run)
        return jnp.max(m)

    state = tuple((jnp.zeros((1, tq), F32), jnp.zeros((HEAD_DIM, tq), F32)) for _ in heads)
    n_diag = tq // tk
    for d in range(n_diag):
        state = process((i + 1) * n_diag - 1 - d, state, True)

    def cond(carry):
        kb, worst, _ = carry
        return jnp.logical_and(kb >= 0, worst > SB_CUTOFF)

    def body(carry):
        kb, _, st = carry
        st = process(kb, st, False)
        return kb - 1, run_max(st), st

    _, _, state = lax.while_loop(cond, body, (i * n_diag - 1, run_max(state), state))
    for p in range(PAIR_W // LANES):
        outs = [state[h][1] for h in heads if h // HEADS_PER_BLOCK == p]
        outs += [jnp.zeros((HEAD_DIM, tq), F32)] * (HEADS_PER_BLOCK - len(outs))
        o_ref[:, p * LANES:(p + 1) * LANES] = _heads_to_token_rows(outs)


def _sb_attention(pb, batch, seq):
    tq = min(256, seq)
    tk = min(128, seq)
    nq = seq // tq
    m = batch * seq
    return pl.pallas_call(
        _sb_kernel,
        grid=(batch, nq),
        in_specs=[pl.BlockSpec((tq, PAIR_W), lambda b, i: (b * nq + i, BLK_SB_Q // 3)),
                  pl.BlockSpec((seq, PAIR_W), lambda b, i: (b, BLK_SB_K // 3)),
                  pl.BlockSpec((seq, PAIR_W), lambda b, i: (b, BLK_SB_V // 3))],
        out_specs=pl.BlockSpec((tq, PAIR_W), lambda b, i: (b * nq + i, 0)),
        out_shape=jax.ShapeDtypeStruct((m, PAIR_W), BF16),
        scratch_shapes=[pltpu.VMEM((seq // tk, PAIR_W, tk), BF16)],
        compiler_params=_params(2),
        name="sb_attention",
    )(pb, pb, pb)


HALF_BITS = 16
HALF_OFFSET = 1 << (HALF_BITS - 1)


def _dsa_kernel(q_ref, iq_ref, w_ref, k_ref, v_ref, ka_ref, kb_ref, o_ref, key_ref, hi_ref, lo_ref, bias_ref,
                x_ref, vt_ref, s_ref, *, topk, seq):
    t = q_ref.shape[0]
    i = pl.program_id(1)

    @pl.when(i == 0)
    def _():
        _fill_transposed_with_ones(v_ref, vt_ref, N_DSA)

    n_kb = i + 1
    kf = float(topk)
    imin = jnp.int32(INT_MIN)
    key_pos = lax.broadcasted_iota(I32, (t, t), 0)
    qry_pos = lax.broadcasted_iota(I32, (t, t), 1) + i * t

    iq_t = [_transpose_bf16(iq_ref[:, g * LANES:(g + 1) * LANES]) for g in range(N_IDX_HEADS // 2)]
    w_t = (w_ref[...] * (N_IDX_HEADS ** -0.5) * (IDX_DIM ** -0.5)).T

    def index_block(kb):
        k0 = pl.multiple_of(kb * t, t)
        ka = ka_ref[pl.ds(k0, t), :]
        kb_ = kb_ref[pl.ds(k0, t), :]
        score = jnp.zeros((t, t), F32)
        for g in range(N_IDX_HEADS // 2):
            lo = SMALL_IDXW_LANE + 2 * g
            score = score + jnp.maximum(_dot(ka, iq_t[g]), 0.0) * w_t[lo:lo + 1, :]
            score = score + jnp.maximum(_dot(kb_, iq_t[g]), 0.0) * w_t[lo + 1:lo + 2, :]
        bits = lax.bitcast_convert_type(score + 0.0, I32)
        key = jnp.where(bits < 0, bits ^ jnp.int32(0x7FFFFFFF), bits)
        key = jnp.where(key_pos + k0 <= qry_pos, key, imin)
        key_ref[pl.ds(k0, t), :] = key
        hi_ref[pl.ds(k0, t), :] = lax.shift_right_arithmetic(key, HALF_BITS).astype(jnp.int16)
        lo_ref[pl.ds(k0, t), :] = ((key & (2 * HALF_OFFSET - 1)) - HALF_OFFSET).astype(jnp.int16)

    odd = n_kb % 2

    @pl.when(odd == 1)
    def _():
        index_block(0)

    def index_pair(j, _):
        index_block(odd + 2 * j)
        index_block(odd + 2 * j + 1)
        return 0

    lax.fori_loop(0, n_kb // 2, index_pair, 0)

    def count16(ref, pred):
        rows = 16
        def body(kb, acc):
            k0 = pl.multiple_of(kb * t, t)
            ind = pred(ref[pl.ds(k0, t), :])
            parts = [ind[r:r + rows, :] for r in range(0, t, rows)]
            while len(parts) > 1:
                parts = [parts[j] + parts[j + 1] for j in range(0, len(parts), 2)]
            return acc + parts[0]
        acc = lax.fori_loop(0, n_kb, body, jnp.zeros((rows, t), jnp.int16))
        return jnp.sum(acc.astype(F32), axis=0, keepdims=True)

    one16, zero16 = jnp.int16(1), jnp.int16(0)

    def kth_half(ref, need):
        def bit_step(it, prefix):
            cand_u = prefix | lax.shift_left(jnp.int32(1), HALF_BITS - 1 - it)
            cand = (cand_u - HALF_OFFSET).astype(jnp.int16)
            cnt = count16(ref, lambda blk: jnp.where(blk >= cand, one16, zero16))
            return jnp.where(cnt >= need, cand_u, prefix)
        return lax.fori_loop(0, HALF_BITS, bit_step, jnp.zeros((1, t), I32)) - HALF_OFFSET

    def count(pred):
        def body(kb, acc):
            k0 = pl.multiple_of(kb * t, t)
            ind = pred(key_ref[pl.ds(k0, t), :], k0)
            parts = [ind[r:r + 8, :] for r in range(0, t, 8)]
            while len(parts) > 1:
                parts = [parts[j] + parts[j + 1] for j in range(0, len(parts), 2)]
            return acc + parts[0]
        acc = lax.fori_loop(0, n_kb, body, jnp.zeros((8, t), F32))
        return jnp.sum(acc, axis=0, keepdims=True)

    kth_hi = kth_half(hi_ref, kf)
    kth_hi16 = kth_hi.astype(jnp.int16)
    need_lo = kf - count16(hi_ref, lambda blk: jnp.where(blk > kth_hi16, one16, zero16))

    def keep_bucket(kb, _):
        k0 = pl.multiple_of(kb * t, t)
        lo_ref[pl.ds(k0, t), :] = jnp.where(hi_ref[pl.ds(k0, t), :] == kth_hi16, lo_ref[pl.ds(k0, t), :],
                                            jnp.int16(-HALF_OFFSET))
        return 0

    lax.fori_loop(0, n_kb, keep_bucket, 0)
    kth_lo = kth_half(lo_ref, need_lo)
    kth = lax.shift_left(kth_hi, HALF_BITS) + (kth_lo + HALF_OFFSET)
    n_gt = count(lambda kblk, k0: jnp.where(kblk > kth, 1.0, 0.0))
    n_ge = count(lambda kblk, k0: jnp.where(kblk >= kth, 1.0, 0.0))
    need = kf - n_gt

    x_ref[...] = jnp.full((1, t), seq, I32)

    @pl.when(jnp.max(n_ge) > kf)
    def _():
        n_bits = max(1, (seq - 1).bit_length())

        def tie_step(it, x):
            cand = x | lax.shift_left(jnp.int32(1), n_bits - 1 - it)
            g = count(lambda kblk, k0: jnp.where(
                kblk == kth, jnp.where(key_pos + k0 < cand, 1.0, 0.0), 0.0))
            return jnp.where(g < need, cand, x)

        x_ref[...] = lax.fori_loop(0, n_bits, tie_step, jnp.zeros((1, t), I32))

    x = jnp.where(kth == imin, -1, x_ref[...])

    def select(kb, _):
        k0 = pl.multiple_of(kb * t, t)
        kblk = key_ref[pl.ds(k0, t), :]
        tie = jnp.where(kblk == kth, jnp.where(key_pos + k0 <= x, 0.0, NEG_BIG), NEG_BIG)
        bias_ref[pl.ds(k0, t), :] = jnp.where(kblk > kth, 0.0, tie)
        return 0

    lax.fori_loop(0, n_kb, select, 0)

    q_t = [q_ref[:, p * LANES:(p + 1) * LANES].astype(F32).T for p in range(PAIR_W // LANES)]
    qh = [jnp.where(_head_row_mask(h % HEADS_PER_BLOCK), q_t[h // HEADS_PER_BLOCK], 0.0).astype(BF16)
          for h in range(N_DSA)]

    def scores(kb):
        k0 = pl.multiple_of(kb * t, t)
        bias = bias_ref[pl.ds(k0, t), :]
        return [_dot(k_ref[pl.ds(k0, t), (h // HEADS_PER_BLOCK) * LANES:(h // HEADS_PER_BLOCK + 1) * LANES], qh[h])
                + bias for h in range(N_DSA)]

    def values_t(kb):
        return [vt_ref[kb, h * V_ROWS:(h + 1) * V_ROWS, :] for h in range(N_DSA)]

    carry = _flash_blocks(n_kb, scores, values_t, tuple(_softmax_init(t) for _ in range(N_DSA)), s_ref)
    outs = [_softmax_result(st) for st in carry]
    outs += [jnp.zeros((HEAD_DIM, t), F32)] * (PAIR_W // HEAD_DIM - N_DSA)
    for p in range(PAIR_W // LANES):
        o_ref[:, p * LANES:(p + 1) * LANES] = _heads_to_token_rows(
            outs[p * HEADS_PER_BLOCK:(p + 1) * HEADS_PER_BLOCK])


def _dsa_attention(pb, ps, batch, seq):
    t = min(256, seq)
    nq = seq // t
    m = batch * seq
    topk = min(TOPK_MAX, seq // 4)
    return pl.pallas_call(
        functools.partial(_dsa_kernel, topk=topk, seq=seq),
        grid=(batch, nq),
        in_specs=[pl.BlockSpec((t, PAIR_W), lambda b, i: (b * nq + i, BLK_DSA_Q // 3)),
                  pl.BlockSpec((t, 4 * LANES), lambda b, i: (b * nq + i, BLK_IDX_Q // 4)),
                  pl.BlockSpec((t, LANES), lambda b, i: (b * nq + i, 0)),
                  pl.BlockSpec((seq, PAIR_W), lambda b, i: (b, BLK_DSA_K // 3)),
                  pl.BlockSpec((seq, PAIR_W), lambda b, i: (b, BLK_DSA_V // 3)),
                  pl.BlockSpec((seq, LANES), lambda b, i: (b, BLK_IDX_KA)),
                  pl.BlockSpec((seq, LANES), lambda b, i: (b, BLK_IDX_KB))],
        out_specs=pl.BlockSpec((t, PAIR_W), lambda b, i: (b * nq + i, 0)),
        out_shape=jax.ShapeDtypeStruct((m, PAIR_W), BF16),
        scratch_shapes=[pltpu.VMEM((seq, t), I32), pltpu.VMEM((seq, t), jnp.int16), pltpu.VMEM((seq, t), jnp.int16),
                        pltpu.VMEM((seq, t), F32), pltpu.VMEM((1, t), I32),
                        pltpu.VMEM((nq, N_DSA * V_ROWS, t), BF16), pltpu.VMEM((2, N_DSA, t, t), F32)],
        compiler_params=_params(2),
        name="dsa_attention",
    )(pb, pb, ps, pb, pb, pb, pb)


def _mixer_out_kernel(x_ref, gain_ref, of_ref, od_ref, os_ref, wg_ref, wf_ref, wd_ref, ws_ref, wo_ref, o_ref):
    d = x_ref.shape[1]
    x = x_ref[...]
    h = _rms(x, gain_ref[...]).astype(BF16)
    merged = None
    for b, (o_b, w_b) in enumerate(((of_ref, wf_ref), (od_ref, wd_ref), (os_ref, ws_ref))):
        term = jax.nn.sigmoid(_dot(h, wg_ref[:, b * d:(b + 1) * d])) * _dot(o_b[...], w_b[...])
        merged = term if merged is None else merged + term
    o_ref[...] = x + _dot(merged.astype(BF16), wo_ref[...])


def _mixer_out(x2, gain, o_fox, o_dsa, o_sb, wg, wf, wd, ws, wo):
    m, d = x2.shape
    tm = min(512, m)
    row = lambda w: pl.BlockSpec((tm, w), lambda i: (i, 0))
    return pl.pallas_call(
        _mixer_out_kernel,
        grid=(m // tm,),
        in_specs=[row(d), _resident((1, d)), row(FOX_W), row(PAIR_W), row(PAIR_W), _resident((d, N_BRANCH * d)),
                  _resident((FOX_W, d)), _resident((PAIR_W, d)), _resident((PAIR_W, d)), _resident((d, d))],
        out_specs=row(d),
        out_shape=jax.ShapeDtypeStruct((m, d), F32),
        compiler_params=_params(1),
        name="mixer_out",
    )(x2, gain.reshape(1, d), o_fox, o_dsa, o_sb, wg, wf, wd, ws, wo)


def _norm_matmul_kernel(x_ref, g_ref, w_ref, o_ref):
    o_ref[...] = _dot(_rms(x_ref[...], g_ref[...]).astype(BF16), w_ref[...]).astype(o_ref.dtype)


def _norm_matmul(x2, gain, w, out_dtype):
    m, d = x2.shape
    n = w.shape[1]
    tm = min(512, m)
    return pl.pallas_call(
        _norm_matmul_kernel,
        grid=(m // tm,),
        in_specs=[pl.BlockSpec((tm, d), lambda i: (i, 0)), _resident((1, d)), _resident((d, n))],
        out_specs=pl.BlockSpec((tm, n), lambda i: (i, 0)),
        out_shape=jax.ShapeDtypeStruct((m, n), out_dtype),
        compiler_params=_params(1),
        name="norm_matmul",
    )(x2, gain.reshape(1, d), w)


def _cross_kernel(x_ref, g_ref, wq_ref, kv_ref, wo_ref, o_ref):
    x = x_ref[...]
    h = _rms(x, g_ref[...]).astype(BF16)
    q = _dot(h, wq_ref[...])
    scale = CA_HEAD_DIM ** -0.5
    outs = []
    for hd in range(N_CA_HEADS):
        qh = q[:, hd * CA_HEAD_DIM:(hd + 1) * CA_HEAD_DIM].astype(BF16)
        logits = _dot_nt(qh, kv_ref[:, hd * CA_HEAD_DIM:(hd + 1) * CA_HEAD_DIM]) * scale
        e = jnp.exp(logits - jnp.max(logits, axis=1, keepdims=True))
        pv = _dot(e.astype(BF16), kv_ref[:, CA_W + hd * CA_HEAD_DIM:CA_W + (hd + 1) * CA_HEAD_DIM])
        outs.append(pv / jnp.sum(e, axis=1, keepdims=True))
    o = jnp.concatenate(outs, axis=1).astype(BF16)
    o_ref[...] = x + _dot(o, wo_ref[...])


def _cross_attention(x2, gain, wq, kv, wo, batch, seq):
    m, d = x2.shape
    tm = min(512, seq)
    n_mem = kv.shape[0] // batch
    per_batch = seq // tm
    return pl.pallas_call(
        _cross_kernel,
        grid=(m // tm,),
        in_specs=[pl.BlockSpec((tm, d), lambda i: (i, 0)), _resident((1, d)), _resident((d, CA_W)),
                  pl.BlockSpec((n_mem, 2 * CA_W), lambda i: (i // per_batch, 0)), _resident((CA_W, d))],
        out_specs=pl.BlockSpec((tm, d), lambda i: (i, 0)),
        out_shape=jax.ShapeDtypeStruct((m, d), F32),
        compiler_params=_params(1),
        name="cross_attention",
    )(x2, gain.reshape(1, d), wq, kv, wo)


def _pad_cols(w, n):
    return jnp.pad(w, ((0, 0), (0, n - w.shape[1])))


def _pad_rows(w, n):
    return jnp.pad(w, ((0, n - w.shape[0]), (0, 0)))


def _mixer_weights(w_in):
    offs = np.cumsum((0,) + IN_SPLITS)
    part = lambda j: w_in[:, offs[j]:offs[j + 1]]
    thirds = lambda w: jnp.split(w, 3, axis=1)
    scale = HEAD_DIM ** -0.5
    fq, fk, fv = thirds(part(0))
    dq, dk, dv = thirds(part(2))
    sq, sk, sv = thirds(part(6))
    iq, ik, iw = part(3), part(4), part(5)
    zk = jnp.zeros_like(ik)
    small = jnp.concatenate([part(1), jnp.zeros((w_in.shape[0], SMALL_IDXW_LANE - N_FOX), w_in.dtype), iw], axis=1)
    wb = jnp.concatenate(
        [_pad_cols(dq * scale, PAIR_W), _pad_cols(dk, PAIR_W),
         jnp.concatenate([ik, zk], axis=1), jnp.concatenate([zk, ik], axis=1), iq,
         fq * scale, fk, fv, _pad_cols(dv, PAIR_W),
         _pad_cols(sq * scale, PAIR_W), _pad_cols(sk, PAIR_W), _pad_cols(sv, PAIR_W),
         _pad_cols(small, LANES)], axis=1).astype(BF16)
    return wb, part(7).astype(BF16)


def _hybrid_mixer(x2, cos, sin_signed, batch, seq, gain, w_in, b_fgate, w_fox_out, w_dsa_out, w_sb_out, w_out):
    wb, wg = _mixer_weights(w_in)
    pb, ps = _mixer_in(x2, gain, cos, sin_signed, wb)

    bias_row = _pad_cols(b_fgate.reshape(1, N_FOX).astype(F32), LANES)
    k_side, q_side = _fox_cumsum(ps, bias_row, batch, seq)

    o_fox = _fox_attention(pb, k_side, q_side, batch, seq)
    o_dsa = _dsa_attention(pb, ps, batch, seq)
    o_sb = _sb_attention(pb, batch, seq)
    return _mixer_out(x2, gain, o_fox, o_dsa, o_sb, wg, w_fox_out.astype(BF16),
                      _pad_rows(w_dsa_out, PAIR_W).astype(BF16), _pad_rows(w_sb_out, PAIR_W).astype(BF16),
                      w_out.astype(BF16))


def kernel(x, mem, positions, ffn1_norm, ffn1_w_gu, ffn1_w_down, mix_norm, w_in, b_fgate, w_fox_out, w_dsa_out,
           w_sb_out, w_out, ca_norm, mem_norm, ca_w_q, ca_w_kv, ca_w_o, ffn2_norm, ffn2_w_gu, ffn2_w_down,
           final_norm):
    batch, seq, d = x.shape
    depth = ffn1_norm.shape[0]
    x2 = x.reshape(batch * seq, d)
    mem2 = mem.reshape(-1, d)
    pos2 = positions.reshape(batch * seq, 1).astype(I32)
    half = HEAD_DIM // 2
    inv_freq = jnp.power(ROPE_THETA, -jnp.arange(half, dtype=F32) * (2.0 / HEAD_DIM))
    invf = jnp.tile(inv_freq, LANES // half).reshape(1, LANES)
    cos, sin_signed = _rope_tables(pos2, invf)
    for l in range(depth):
        x2 = _ffn(x2, ffn1_norm[l], ffn1_w_gu[l], ffn1_w_down[l])
        x2 = _hybrid_mixer(x2, cos, sin_signed, batch, seq, mix_norm[l], w_in[l], b_fgate[l], w_fox_out[l],
                           w_dsa_out[l], w_sb_out[l], w_out[l])
        kv = _norm_matmul(mem2, mem_norm[l], ca_w_kv[l].astype(BF16), BF16)
        x2 = _cross_attention(x2, ca_norm[l], ca_w_q[l].astype(BF16), kv, ca_w_o[l].astype(BF16), batch, seq)
        x2 = _ffn(x2, ffn2_norm[l], ffn2_w_gu[l], ffn2_w_down[l],
                  final_gain=final_norm if l == depth - 1 else None)
    return x2.reshape(batch, seq, d)
```

```python
import functools

import jax
import jax.numpy as jnp
import numpy as np
from jax import lax
from jax.experimental import pallas as pl
from jax.experimental.pallas import tpu as pltpu

F32 = jnp.float32
BF16 = jnp.bfloat16
I32 = jnp.int32

D_MODEL = 1024
HEAD_DIM = 64
N_FOX = 6
N_DSA = 5
N_SB = 5
N_IDX_HEADS = 8
IDX_DIM = 64
TOPK_MAX = 256
N_CA_HEADS = 4
CA_HEAD_DIM = 128
D_FF = 2816
ROPE_THETA = 10000.0
NORM_EPS = 1e-6
N_BRANCH = 3
HALF_STEP = 0.5
FOX_W = N_FOX * HEAD_DIM
DSA_W = N_DSA * HEAD_DIM
SB_W = N_SB * HEAD_DIM
CA_W = N_CA_HEADS * CA_HEAD_DIM
IN_SPLITS = (3 * FOX_W, N_FOX, 3 * DSA_W, N_IDX_HEADS * IDX_DIM, IDX_DIM, N_IDX_HEADS, 3 * SB_W,
             N_BRANCH * D_MODEL)

LANES = 128
HEADS_PER_BLOCK = LANES // HEAD_DIM
PAIR_W = 3 * LANES
VMEM_LIMIT = 56 * 2**20

BLK_DSA_Q, BLK_DSA_K = 0, 3
BLK_IDX_KA = 6
BLK_IDX_KB = 7
BLK_IDX_Q = 8
N_ROPE_BLOCKS = 12
BLK_FOX_Q, BLK_FOX_K, BLK_FOX_V = 12, 15, 18
BLK_DSA_V = 21
BLK_SB_Q, BLK_SB_K, BLK_SB_V = 24, 27, 30
N_PROJ_BLOCKS = 33
PROJ_CHUNK = 4 * LANES
SMALL_FGATE_LANE = 0
SMALL_IDXW_LANE = 8

NEG_BIG = -1e30
SB_CUTOFF = -110.0
INT_MIN = -2**31


def _params(n_grid):
    return pltpu.CompilerParams(dimension_semantics=("arbitrary",) * n_grid, vmem_limit_bytes=VMEM_LIMIT)


def _resident(shape):
    nd = len(shape)
    return pl.BlockSpec(shape, lambda *_: (0,) * nd, pipeline_mode=pl.Buffered(1))


def _rms(x, g):
    return x * lax.rsqrt(jnp.mean(x * x, axis=-1, keepdims=True) + NORM_EPS) * g


def _dot(a, b):
    return jnp.dot(a, b, preferred_element_type=F32)


def _dot_nt(a, b):
    return lax.dot_general(a, b, (((1,), (1,)), ((), ())), preferred_element_type=F32)


def _split3(x):
    hi = x.astype(BF16)
    r = x - hi.astype(F32)
    mid = r.astype(BF16)
    lo = (r - mid.astype(F32)).astype(BF16)
    return hi, mid, lo


def _log1p_exp_neg_abs(z):
    return jnp.log1p(jnp.exp(-jnp.abs(z)))


def _ffn_chunks():
    out, c = [], 0
    while c < D_FF:
        w = min(512, D_FF - c)
        out.append((c, w))
        c += w
    return tuple(out)


def _ffn_kernel(*refs, final):
    if final:
        x_ref, g_ref, wgu_ref, wd_ref, fn_ref, o_ref, a_ref = refs
    else:
        x_ref, g_ref, wgu_ref, wd_ref, o_ref, a_ref = refs
    x = x_ref[...]
    h = _rms(x, g_ref[...]).astype(BF16)
    for c0, w in _ffn_chunks():
        g = _dot(h, wgu_ref[:, c0:c0 + w])
        u = _dot(h, wgu_ref[:, D_FF + c0:D_FF + c0 + w])
        a_ref[:, c0:c0 + w] = (g * jax.nn.sigmoid(g) * u).astype(BF16)
    y = x + HALF_STEP * _dot(a_ref[...], wd_ref[...])
    if final:
        y = _rms(y, fn_ref[...])
    o_ref[...] = y


def _ffn(x2, gain, w_gu, w_down, final_gain=None):
    m, d = x2.shape
    tm = min(512, m)
    final = final_gain is not None
    in_specs = [pl.BlockSpec((tm, d), lambda i: (i, 0)), _resident((1, d)),
                _resident((d, 2 * D_FF)), _resident((D_FF, d))]
    args = [x2, gain.reshape(1, d), w_gu.astype(BF16), w_down.astype(BF16)]
    if final:
        in_specs.append(_resident((1, d)))
        args.append(final_gain.reshape(1, d))
    return pl.pallas_call(
        functools.partial(_ffn_kernel, final=final),
        grid=(m // tm,),
        in_specs=in_specs,
        out_specs=pl.BlockSpec((tm, d), lambda i: (i, 0)),
        out_shape=jax.ShapeDtypeStruct((m, d), F32),
        scratch_shapes=[pltpu.VMEM((tm, D_FF), BF16)],
        compiler_params=_params(1),
        name="ffn",
    )(*args)


def _first_half_lanes():
    lane = lax.broadcasted_iota(I32, (1, LANES), 1)
    return (lane % HEAD_DIM) < (HEAD_DIM // 2)


def _rope_tables_kernel(pos_ref, invf_ref, cos_ref, sin_ref):
    ang = pos_ref[...].astype(F32) * invf_ref[...]
    sin = jnp.sin(ang)
    cos_ref[...] = jnp.cos(ang)
    sin_ref[...] = jnp.where(_first_half_lanes(), -sin, sin)


def _rope_tables(pos2, invf):
    m = pos2.shape[0]
    tm = min(512, m)
    tok = pl.BlockSpec((tm, LANES), lambda i: (i, 0))
    return pl.pallas_call(
        _rope_tables_kernel,
        grid=(m // tm,),
        in_specs=[pl.BlockSpec((tm, 1), lambda i: (i, 0)), _resident((1, LANES))],
        out_specs=[tok, tok],
        out_shape=[jax.ShapeDtypeStruct((m, LANES), F32)] * 2,
        compiler_params=_params(1),
        name="rope_tables",
    )(pos2, invf)


def _mixer_in_kernel(x_ref, g_ref, cos_ref, sin_ref, wb_ref, pb_ref, ps_ref):
    h = _rms(x_ref[...], g_ref[...]).astype(BF16)
    cos = cos_ref[...]
    sin_signed = sin_ref[...]
    first_half = _first_half_lanes()
    n_cols = (N_PROJ_BLOCKS + 1) * LANES
    for c0 in range(0, n_cols, PROJ_CHUNK):
        c1 = min(c0 + PROJ_CHUNK, n_cols)
        res = _dot(h, wb_ref[:, c0:c1])
        if c1 == n_cols:
            ps_ref[...] = res[:, c1 - c0 - LANES:]
            c1 -= LANES
            res = res[:, :c1 - c0]
        if c0 < N_ROPE_BLOCKS * LANES:
            parts = []
            for b in range((c1 - c0) // LANES):
                xb = res[:, b * LANES:(b + 1) * LANES]
                partner = jnp.where(first_half, pltpu.roll(xb, LANES - HEAD_DIM // 2, 1),
                                    pltpu.roll(xb, HEAD_DIM // 2, 1))
                parts.append(xb * cos + partner * sin_signed)
            res = jnp.concatenate(parts, axis=1)
        pb_ref[:, c0:c1] = res.astype(BF16)


def _mixer_in(x2, gain, cos, sin_signed, wb):
    m, d = x2.shape
    tm = min(512, m)
    nb = N_PROJ_BLOCKS * LANES
    return pl.pallas_call(
        _mixer_in_kernel,
        grid=(m // tm,),
        in_specs=[pl.BlockSpec((tm, d), lambda i: (i, 0)), _resident((1, d)),
                  pl.BlockSpec((tm, LANES), lambda i: (i, 0)), pl.BlockSpec((tm, LANES), lambda i: (i, 0)),
                  _resident((d, nb + LANES))],
        out_specs=[pl.BlockSpec((tm, nb), lambda i: (i, 0)), pl.BlockSpec((tm, LANES), lambda i: (i, 0))],
        out_shape=[jax.ShapeDtypeStruct((m, nb), BF16), jax.ShapeDtypeStruct((m, LANES), F32)],
        compiler_params=_params(1),
        name="mixer_in",
    )(x2, gain.reshape(1, d), cos, sin_signed, wb)


FOX_FEAT_ROWS = 16


def _fox_feature_maps():
    k_map = np.zeros((3, LANES, FOX_W), np.float32)
    q_map = np.zeros((3, LANES, LANES), np.float32)
    k_one = np.zeros((1, FOX_W), np.float32)
    q_one = np.zeros((1, LANES), np.float32)
    for h in range(N_FOX):
        p, hh = divmod(h, HEADS_PER_BLOCK)
        for j in range(3):
            k_map[j, h, p * LANES + 3 * hh + j] = -1.0
            q_map[j, h, FOX_FEAT_ROWS * h + 6 + j] = 1.0
            k_one[0, p * LANES + 6 + j] = 1.0
            q_one[0, FOX_FEAT_ROWS * h + 3 * hh + j] = 1.0
    return (jnp.asarray(k_map, BF16), jnp.asarray(q_map, BF16), jnp.asarray(k_one), jnp.asarray(q_one))


def _fox_cumsum_kernel(ps_ref, b_ref, kmap_ref, qmap_ref, kone_ref, qone_ref, kf_ref, qf_ref, carry_ref):
    @pl.when(pl.program_id(1) == 0)
    def _():
        carry_ref[...] = jnp.zeros_like(carry_ref)

    tc = ps_ref.shape[0]
    z = ps_ref[...] + b_ref[...]
    log_f = jnp.minimum(z, 0.0) - _log1p_exp_neg_abs(z)
    row = lax.broadcasted_iota(I32, (tc, tc), 0)
    col = lax.broadcasted_iota(I32, (tc, tc), 1)
    tri = jnp.where(col <= row, 1.0, 0.0).astype(BF16)
    hi, mid, lo = _split3(log_f)
    cs = _dot(tri, hi) + _dot(tri, mid) + _dot(tri, lo) + carry_ref[...]
    carry_ref[...] = cs[tc - 1:tc, :]
    terms = _split3(cs)
    kf = kone_ref[...]
    qf = qone_ref[...]
    for j in range(3):
        kf = kf + _dot(terms[j], kmap_ref[j])
        qf = qf + _dot(terms[j], qmap_ref[j])
    kf_ref[...] = kf.astype(BF16)
    qf_ref[...] = qf.astype(BF16)


def _fox_cumsum(ps, bias_row, batch, seq):
    tc = min(512, seq)
    nj = seq // tc
    m = batch * seq
    tok = lambda w: pl.BlockSpec((tc, w), lambda b, j: (b * nj + j, 0))
    k_side, q_tok = pl.pallas_call(
        _fox_cumsum_kernel,
        grid=(batch, nj),
        in_specs=[tok(LANES), _resident((1, LANES)), _resident((3, LANES, FOX_W)), _resident((3, LANES, LANES)),
                  _resident((1, FOX_W)), _resident((1, LANES))],
        out_specs=[tok(FOX_W), tok(LANES)],
        out_shape=[jax.ShapeDtypeStruct((m, FOX_W), BF16), jax.ShapeDtypeStruct((m, LANES), BF16)],
        scratch_shapes=[pltpu.VMEM((1, LANES), F32)],
        compiler_params=_params(2),
        name="fox_cumsum",
    )(ps, bias_row, *_fox_feature_maps())
    q_side = q_tok[:, :N_FOX * FOX_FEAT_ROWS].reshape(batch, seq, N_FOX, FOX_FEAT_ROWS).transpose(0, 2, 3, 1)
    return k_side, q_side


def _head_lane_mask(hh):
    lane = lax.broadcasted_iota(I32, (1, LANES), 1)
    return (lane < HEAD_DIM) if hh == 0 else (lane >= HEAD_DIM)


def _head_row_mask(hh):
    row = lax.broadcasted_iota(I32, (LANES, 1), 0)
    return (row < HEAD_DIM) if hh == 0 else (row >= HEAD_DIM)


def _transpose_bf16(x):
    return x.astype(F32).T.astype(BF16)


def _fill_transposed(src_ref, dst_ref):
    n_blocks, _, tk = dst_ref.shape
    for j in range(n_blocks):
        dst_ref[j] = _transpose_bf16(src_ref[j * tk:(j + 1) * tk, :])


V_ROWS = HEAD_DIM + 16


def _fill_transposed_with_ones(src_ref, dst_ref, n_heads):
    n_blocks, _, tk = dst_ref.shape
    ones_pad = jnp.where(lax.broadcasted_iota(I32, (V_ROWS - HEAD_DIM, tk), 0) == 0, 1.0, 0.0).astype(BF16)
    for j in range(n_blocks):
        v_t = _transpose_bf16(src_ref[j * tk:(j + 1) * tk, :])
        for h in range(n_heads):
            dst_ref[j, h * V_ROWS:h * V_ROWS + HEAD_DIM, :] = v_t[h * HEAD_DIM:(h + 1) * HEAD_DIM, :]
            dst_ref[j, h * V_ROWS + HEAD_DIM:(h + 1) * V_ROWS, :] = ones_pad


def _softmax_steps(carry, scores, values_t):
    stats = []
    for (m, _), s in zip(carry, scores):
        m_new = jnp.maximum(m, jnp.max(s, axis=0, keepdims=True))
        stats.append((m_new, jnp.exp(m - m_new), jnp.exp(s - m_new).astype(BF16)))
    return tuple((m_new, alpha * acc + _dot(v_t, p))
                 for (m_new, alpha, p), (_, acc), v_t in zip(stats, carry, values_t))


def _softmax_init(tq):
    return (jnp.full((1, tq), NEG_BIG, F32), jnp.zeros((V_ROWS, tq), F32))


def _softmax_result(state):
    _, acc = state
    return acc[:HEAD_DIM, :] / acc[HEAD_DIM:HEAD_DIM + 1, :]


def _flash_blocks(n, score_fn, value_fn, carry, s_ref):
    n_heads = len(carry)

    def load(slot):
        return [s_ref[slot, h] for h in range(n_heads)]

    def store(slot, kb):
        for h, s in enumerate(score_fn(kb)):
            s_ref[slot, h] = s

    odd = n % 2
    carry = lax.cond(odd == 1, lambda c: _softmax_steps(c, score_fn(0), value_fn(0)), lambda c: c, carry)

    @pl.when(n >= 2)
    def _():
        store(0, odd)

    def body(j, c):
        kb = odd + 2 * j
        store(1, kb + 1)
        c = _softmax_steps(c, load(0), value_fn(kb))
        store(0, jnp.minimum(kb + 2, n - 1))
        return _softmax_steps(c, load(1), value_fn(kb + 1))

    return lax.fori_loop(0, n // 2, body, carry)


def _heads_to_token_rows(out_t):
    return jnp.concatenate(out_t, axis=0).T.astype(BF16)


def _fox_kernel(q_ref, k_ref, v_ref, kf_ref, qf_ref, o_ref, vt_ref, s_ref):
    tq = q_ref.shape[0]
    tk = vt_ref.shape[2]
    i = pl.program_id(1)

    @pl.when(i == 0)
    def _():
        _fill_transposed_with_ones(v_ref, vt_ref, N_FOX)

    key_pos = lax.broadcasted_iota(I32, (tk, tq), 0)
    qry_pos = lax.broadcasted_iota(I32, (tk, tq), 1)
    causal = [key_pos + d * tk <= qry_pos for d in range(2)]
    pad = jnp.zeros((LANES - FOX_FEAT_ROWS, tq), BF16)
    qx = []
    for p in range(FOX_W // LANES):
        q_t = q_ref[:, p * LANES:(p + 1) * LANES].astype(F32).T
        for hh in range(HEADS_PER_BLOCK):
            qx.append(jnp.concatenate([jnp.where(_head_row_mask(hh), q_t, 0.0).astype(BF16),
                                       qf_ref[0, p * HEADS_PER_BLOCK + hh], pad], axis=0))

    def store(slot, kb):
        k0 = pl.multiple_of(kb * tk, tk)
        for h in range(N_FOX):
            p = h // HEADS_PER_BLOCK
            kx = jnp.concatenate([k_ref[pl.ds(k0, tk), p * LANES:(p + 1) * LANES],
                                  kf_ref[pl.ds(k0, tk), p * LANES:(p + 1) * LANES]], axis=1)
            s_ref[slot, h] = _dot(kx, qx[h])

    def step(carry, slot, kb, mask=None):
        scores = [s_ref[slot, h] for h in range(N_FOX)]
        if mask is not None:
            scores = [jnp.where(mask, s, NEG_BIG) for s in scores]
        return _softmax_steps(carry, scores, [vt_ref[kb, h * V_ROWS:(h + 1) * V_ROWS, :] for h in range(N_FOX)])

    def trip(j, carry):
        kb = 2 * j
        store(1, kb + 1)
        carry = step(carry, 0, kb)
        store(0, kb + 2)
        return step(carry, 1, kb + 1)

    store(0, 0)
    carry = lax.fori_loop(0, i, trip, tuple(_softmax_init(tq) for _ in range(N_FOX)))
    kb = 2 * i
    store(1, kb + 1)
    carry = step(carry, 0, kb, causal[0])
    carry = step(carry, 1, kb + 1, causal[1])
    for p in range(FOX_W // LANES):
        o_ref[:, p * LANES:(p + 1) * LANES] = _heads_to_token_rows(
            [_softmax_result(st) for st in carry[p * HEADS_PER_BLOCK:(p + 1) * HEADS_PER_BLOCK]])


def _fox_attention(pb, k_side, q_side, batch, seq):
    tq = min(512, seq)
    tk = tq // 2
    nq = seq // tq
    m = batch * seq
    return pl.pallas_call(
        _fox_kernel,
        grid=(batch, nq),
        in_specs=[pl.BlockSpec((tq, FOX_W), lambda b, i: (b * nq + i, BLK_FOX_Q // 3)),
                  pl.BlockSpec((seq, FOX_W), lambda b, i: (b, BLK_FOX_K // 3)),
                  pl.BlockSpec((seq, FOX_W), lambda b, i: (b, BLK_FOX_V // 3)),
                  pl.BlockSpec((seq, FOX_W), lambda b, i: (b, 0)),
                  pl.BlockSpec((1, N_FOX, FOX_FEAT_ROWS, tq), lambda b, i: (b, 0, 0, i))],
        out_specs=pl.BlockSpec((tq, FOX_W), lambda b, i: (b * nq + i, 0)),
        out_shape=jax.ShapeDtypeStruct((m, FOX_W), BF16),
        scratch_shapes=[pltpu.VMEM((seq // tk, N_FOX * V_ROWS, tk), BF16), pltpu.VMEM((2, N_FOX, tk, tq), F32)],
        compiler_params=_params(2),
        name="fox_attention",
    )(pb, pb, pb, k_side, q_side)


def _sb_kernel(q_ref, k_ref, v_ref, o_ref, vt_ref):
    tq = q_ref.shape[0]
    tk = vt_ref.shape[2]
    i = pl.program_id(1)

    @pl.when(i == 0)
    def _():
        _fill_transposed(v_ref, vt_ref)

    heads = range(N_SB)
    q_t = [q_ref[:, p * LANES:(p + 1) * LANES].astype(F32).T for p in range(PAIR_W // LANES)]
    qh = [jnp.where(_head_row_mask(h % HEADS_PER_BLOCK), q_t[h // HEADS_PER_BLOCK], 0.0).astype(BF16) for h in heads]
    r = lax.broadcasted_iota(I32, (tk, tk), 0)
    c = lax.broadcasted_iota(I32, (tk, tk), 1)
    upper = jnp.where(c > r, 1.0, 0.0).astype(BF16)
    key_pos = lax.broadcasted_iota(I32, (tk, tq), 0)
    qry_pos = lax.broadcasted_iota(I32, (tk, tq), 1) + i * tq

    def process(kb, state, masked):
        k0 = pl.multiple_of(kb * tk, tk)
        strict = key_pos + k0 < qry_pos
        z = [_dot(k_ref[pl.ds(k0, tk), (h // HEADS_PER_BLOCK) * LANES:(h // HEADS_PER_BLOCK + 1) * LANES], qh[h])
             for h in heads]
        log_beta, log_1m = [], []
        for zh in z:
            l1m = jnp.minimum(-zh, 0.0) - jnp.log(1.0 + jnp.exp(-jnp.abs(zh)))
            log_beta.append(l1m + zh)
            log_1m.append(jnp.where(strict, l1m, 0.0) if masked else l1m)
        after = []
        for h in heads:
            hi, mid, lo = _split3(log_1m[h])
            after.append(_dot(upper, hi) + _dot(upper, mid) + _dot(upper, lo) + state[h][0])
        weights = []
        for h in heads:
            a = jnp.exp(log_beta[h] + after[h])
            weights.append((jnp.where(strict, a, 0.0) if masked else a).astype(BF16))
        return tuple((state[h][0] + jnp.sum(log_1m[h], axis=0, keepdims=True),
                      state[h][1] + _dot(vt_ref[kb, h * HEAD_DIM:(h + 1) * HEAD_DIM, :], weights[h]))
                     for h in heads)

    def run_max(state):
        m = state[0][0]
        for run, _ in state[1:]:
            m = jnp.maximum(m, run)
        return jnp.max(m)

    state = tuple((jnp.zeros((1, tq), F32), jnp.zeros((HEAD_DIM, tq), F32)) for _ in heads)
    n_diag = tq // tk
    for d in range(n_diag):
        state = process((i + 1) * n_diag - 1 - d, state, True)

    def cond(carry):
        kb, worst, _ = carry
        return jnp.logical_and(kb >= 0, worst > SB_CUTOFF)

    def body(carry):
        kb, _, st = carry
        st = process(kb, st, False)
        return kb - 1, run_max(st), st

    _, _, state = lax.while_loop(cond, body, (i * n_diag - 1, run_max(state), state))
    for p in range(PAIR_W // LANES):
        outs = [state[h][1] for h in heads if h // HEADS_PER_BLOCK == p]
        outs += [jnp.zeros((HEAD_DIM, tq), F32)] * (HEADS_PER_BLOCK - len(outs))
        o_ref[:, p * LANES:(p + 1) * LANES] = _heads_to_token_rows(outs)


def _sb_attention(pb, batch, seq):
    tq = min(256, seq)
    tk = min(128, seq)
    nq = seq // tq
    m = batch * seq
    return pl.pallas_call(
        _sb_kernel,
        grid=(batch, nq),
        in_specs=[pl.BlockSpec((tq, PAIR_W), lambda b, i: (b * nq + i, BLK_SB_Q // 3)),
                  pl.BlockSpec((seq, PAIR_W), lambda b, i: (b, BLK_SB_K // 3)),
                  pl.BlockSpec((seq, PAIR_W), lambda b, i: (b, BLK_SB_V // 3))],
        out_specs=pl.BlockSpec((tq, PAIR_W), lambda b, i: (b * nq + i, 0)),
        out_shape=jax.ShapeDtypeStruct((m, PAIR_W), BF16),
        scratch_shapes=[pltpu.VMEM((seq // tk, PAIR_W, tk), BF16)],
        compiler_params=_params(2),
        name="sb_attention",
    )(pb, pb, pb)


HALF_BITS = 16
HALF_OFFSET = 1 << (HALF_BITS - 1)


def _dsa_kernel(q_ref, iq_ref, w_ref, k_ref, v_ref, ka_ref, kb_ref, o_ref, key_ref, hi_ref, lo_ref, bias_ref,
                vt_ref, s_ref, *, topk, seq):
    t = q_ref.shape[0]
    i = pl.program_id(1)

    @pl.when(i == 0)
    def _():
        _fill_transposed_with_ones(v_ref, vt_ref, N_DSA)

    n_kb = i + 1
    kf = float(topk)
    imin = jnp.int32(INT_MIN)
    key_pos = lax.broadcasted_iota(I32, (t, t), 0)
    qry_pos = lax.broadcasted_iota(I32, (t, t), 1) + i * t

    iq_t = [_transpose_bf16(iq_ref[:, g * LANES:(g + 1) * LANES]) for g in range(N_IDX_HEADS // 2)]
    w_t = (w_ref[...] * (N_IDX_HEADS ** -0.5) * (IDX_DIM ** -0.5)).T

    def index_block(kb):
        k0 = pl.multiple_of(kb * t, t)
        ka = ka_ref[pl.ds(k0, t), :]
        kb_ = kb_ref[pl.ds(k0, t), :]
        score = jnp.zeros((t, t), F32)
        for g in range(N_IDX_HEADS // 2):
            lo = SMALL_IDXW_LANE + 2 * g
            score = score + jnp.maximum(_dot(ka, iq_t[g]), 0.0) * w_t[lo:lo + 1, :]
            score = score + jnp.maximum(_dot(kb_, iq_t[g]), 0.0) * w_t[lo + 1:lo + 2, :]
        bits = lax.bitcast_convert_type(score + 0.0, I32)
        key = jnp.where(bits < 0, bits ^ jnp.int32(0x7FFFFFFF), bits)
        key = jnp.where(key_pos + k0 <= qry_pos, key, imin)
        key_ref[pl.ds(k0, t), :] = key
        hi_ref[pl.ds(k0, t), :] = lax.shift_right_arithmetic(key, HALF_BITS).astype(jnp.int16)
        lo_ref[pl.ds(k0, t), :] = ((key & (2 * HALF_OFFSET - 1)) - HALF_OFFSET).astype(jnp.int16)

    odd = n_kb % 2

    @pl.when(odd == 1)
    def _():
        index_block(0)

    def index_pair(j, _):
        index_block(odd + 2 * j)
        index_block(odd + 2 * j + 1)
        return 0

    lax.fori_loop(0, n_kb // 2, index_pair, 0)

    masked16 = jnp.full((t, t), -HALF_OFFSET, jnp.int16)

    @pl.when(odd == 1)
    def _():
        k0 = pl.multiple_of(n_kb * t, t)
        hi_ref[pl.ds(k0, t), :] = masked16
        lo_ref[pl.ds(k0, t), :] = masked16

    n_scan = (n_kb + 1) // 2
    scan_rows = 2 * t

    def count16(ref, pred):
        rows = 16
        def body(trip, acc):
            k0 = pl.multiple_of(trip * scan_rows, scan_rows)
            ind = pred(ref[pl.ds(k0, scan_rows), :])
            parts = [ind[r:r + rows, :] for r in range(0, scan_rows, rows)]
            while len(parts) > 1:
                parts = [parts[j] + parts[j + 1] for j in range(0, len(parts), 2)]
            return acc + parts[0]
        acc = lax.fori_loop(0, n_scan, body, jnp.zeros((rows, t), jnp.int16))
        return jnp.sum(acc.astype(F32), axis=0, keepdims=True)

    one16, zero16 = jnp.int16(1), jnp.int16(0)

    def kth_half(ref, need):
        def bit_step(it, prefix):
            cand_u = prefix | lax.shift_left(jnp.int32(1), HALF_BITS - 1 - it)
            cand = (cand_u - HALF_OFFSET).astype(jnp.int16)
            cnt = count16(ref, lambda blk: jnp.where(blk >= cand, one16, zero16))
            return jnp.where(cnt >= need, cand_u, prefix)
        return lax.fori_loop(0, HALF_BITS, bit_step, jnp.zeros((1, t), I32)) - HALF_OFFSET

    def count(pred):
        def body(kb, acc):
            k0 = pl.multiple_of(kb * t, t)
            ind = pred(key_ref[pl.ds(k0, t), :], k0)
            parts = [ind[r:r + 8, :] for r in range(0, t, 8)]
            while len(parts) > 1:
                parts = [parts[j] + parts[j + 1] for j in range(0, len(parts), 2)]
            return acc + parts[0]
        acc = lax.fori_loop(0, n_kb, body, jnp.zeros((8, t), F32))
        return jnp.sum(acc, axis=0, keepdims=True)

    kth_hi = kth_half(hi_ref, kf)
    kth_hi16 = kth_hi.astype(jnp.int16)
    need_lo = kf - count16(hi_ref, lambda blk: jnp.where(blk > kth_hi16, one16, zero16))

    def keep_bucket(trip, _):
        k0 = pl.multiple_of(trip * scan_rows, scan_rows)
        lo_ref[pl.ds(k0, scan_rows), :] = jnp.where(hi_ref[pl.ds(k0, scan_rows), :] == kth_hi16,
                                                    lo_ref[pl.ds(k0, scan_rows), :], jnp.int16(-HALF_OFFSET))
        return 0

    lax.fori_loop(0, n_scan, keep_bucket, 0)
    kth_lo = kth_half(lo_ref, need_lo)
    kth = lax.shift_left(kth_hi, HALF_BITS) + (kth_lo + HALF_OFFSET)
    kth_lo16 = kth_lo.astype(jnp.int16)
    n_ge = (kf - need_lo) + count16(lo_ref, lambda blk: jnp.where(blk >= kth_lo16, one16, zero16))
    exact_path = jnp.max(n_ge) > kf

    @pl.when(jnp.logical_not(exact_path))
    def _():
        def select(kb, _):
            k0 = pl.multiple_of(kb * t, t)
            bias_ref[pl.ds(k0, t), :] = jnp.where(key_ref[pl.ds(k0, t), :] >= kth, 0.0, NEG_BIG)
            return 0

        lax.fori_loop(0, n_kb, select, 0)

    @pl.when(exact_path)
    def _():
        need = kf - count(lambda kblk, k0: jnp.where(kblk > kth, 1.0, 0.0))
        n_bits = max(1, (seq - 1).bit_length())

        def tie_step(it, x):
            cand = x | lax.shift_left(jnp.int32(1), n_bits - 1 - it)
            g = count(lambda kblk, k0: jnp.where(
                kblk == kth, jnp.where(key_pos + k0 < cand, 1.0, 0.0), 0.0))
            return jnp.where(g < need, cand, x)

        x = lax.fori_loop(0, n_bits, tie_step, jnp.zeros((1, t), I32))
        x = jnp.where(kth == imin, -1, x)

        def select(kb, _):
            k0 = pl.multiple_of(kb * t, t)
            kblk = key_ref[pl.ds(k0, t), :]
            tie = jnp.where(kblk == kth, jnp.where(key_pos + k0 <= x, 0.0, NEG_BIG), NEG_BIG)
            bias_ref[pl.ds(k0, t), :] = jnp.where(kblk > kth, 0.0, tie)
            return 0

        lax.fori_loop(0, n_kb, select, 0)

    q_t = [q_ref[:, p * LANES:(p + 1) * LANES].astype(F32).T for p in range(PAIR_W // LANES)]
    qh = [jnp.where(_head_row_mask(h % HEADS_PER_BLOCK), q_t[h // HEADS_PER_BLOCK], 0.0).astype(BF16)
          for h in range(N_DSA)]

    def scores(kb):
        k0 = pl.multiple_of(kb * t, t)
        bias = bias_ref[pl.ds(k0, t), :]
        return [_dot(k_ref[pl.ds(k0, t), (h // HEADS_PER_BLOCK) * LANES:(h // HEADS_PER_BLOCK + 1) * LANES], qh[h])
                + bias for h in range(N_DSA)]

    def values_t(kb):
        return [vt_ref[kb, h * V_ROWS:(h + 1) * V_ROWS, :] for h in range(N_DSA)]

    carry = _flash_blocks(n_kb, scores, values_t, tuple(_softmax_init(t) for _ in range(N_DSA)), s_ref)
    outs = [_softmax_result(st) for st in carry]
    outs += [jnp.zeros((HEAD_DIM, t), F32)] * (PAIR_W // HEAD_DIM - N_DSA)
    for p in range(PAIR_W // LANES):
        o_ref[:, p * LANES:(p + 1) * LANES] = _heads_to_token_rows(
            outs[p * HEADS_PER_BLOCK:(p + 1) * HEADS_PER_BLOCK])


def _dsa_attention(pb, ps, batch, seq):
    t = min(256, seq)
    nq = seq // t
    m = batch * seq
    topk = min(TOPK_MAX, seq // 4)
    return pl.pallas_call(
        functools.partial(_dsa_kernel, topk=topk, seq=seq),
        grid=(batch, nq),
        in_specs=[pl.BlockSpec((t, PAIR_W), lambda b, i: (b * nq + i, BLK_DSA_Q // 3)),
                  pl.BlockSpec((t, 4 * LANES), lambda b, i: (b * nq + i, BLK_IDX_Q // 4)),
                  pl.BlockSpec((t, LANES), lambda b, i: (b * nq + i, 0)),
                  pl.BlockSpec((seq, PAIR_W), lambda b, i: (b, BLK_DSA_K // 3)),
                  pl.BlockSpec((seq, PAIR_W), lambda b, i: (b, BLK_DSA_V // 3)),
                  pl.BlockSpec((seq, LANES), lambda b, i: (b, BLK_IDX_KA)),
                  pl.BlockSpec((seq, LANES), lambda b, i: (b, BLK_IDX_KB))],
        out_specs=pl.BlockSpec((t, PAIR_W), lambda b, i: (b * nq + i, 0)),
        out_shape=jax.ShapeDtypeStruct((m, PAIR_W), BF16),
        scratch_shapes=[pltpu.VMEM((seq, t), I32), pltpu.VMEM((seq + t, t), jnp.int16),
                        pltpu.VMEM((seq + t, t), jnp.int16),
                        pltpu.VMEM((seq, t), F32),
                        pltpu.VMEM((nq, N_DSA * V_ROWS, t), BF16), pltpu.VMEM((2, N_DSA, t, t), F32)],
        compiler_params=_params(2),
        name="dsa_attention",
    )(pb, pb, ps, pb, pb, pb, pb)


def _mixer_out_kernel(x_ref, gain_ref, of_ref, od_ref, os_ref, wg_ref, wf_ref, wd_ref, ws_ref, wo_ref, o_ref):
    d = x_ref.shape[1]
    x = x_ref[...]
    h = _rms(x, gain_ref[...]).astype(BF16)
    merged = None
    for b, (o_b, w_b) in enumerate(((of_ref, wf_ref), (od_ref, wd_ref), (os_ref, ws_ref))):
        term = jax.nn.sigmoid(_dot(h, wg_ref[:, b * d:(b + 1) * d])) * _dot(o_b[...], w_b[...])
        merged = term if merged is None else merged + term
    o_ref[...] = x + _dot(merged.astype(BF16), wo_ref[...])


def _mixer_out(x2, gain, o_fox, o_dsa, o_sb, wg, wf, wd, ws, wo):
    m, d = x2.shape
    tm = min(512, m)
    row = lambda w: pl.BlockSpec((tm, w), lambda i: (i, 0))
    return pl.pallas_call(
        _mixer_out_kernel,
        grid=(m // tm,),
        in_specs=[row(d), _resident((1, d)), row(FOX_W), row(PAIR_W), row(PAIR_W), _resident((d, N_BRANCH * d)),
                  _resident((FOX_W, d)), _resident((PAIR_W, d)), _resident((PAIR_W, d)), _resident((d, d))],
        out_specs=row(d),
        out_shape=jax.ShapeDtypeStruct((m, d), F32),
        compiler_params=_params(1),
        name="mixer_out",
    )(x2, gain.reshape(1, d), o_fox, o_dsa, o_sb, wg, wf, wd, ws, wo)


def _norm_matmul_kernel(x_ref, g_ref, w_ref, o_ref):
    o_ref[...] = _dot(_rms(x_ref[...], g_ref[...]).astype(BF16), w_ref[...]).astype(o_ref.dtype)


def _norm_matmul(x2, gain, w, out_dtype):
    m, d = x2.shape
    n = w.shape[1]
    tm = min(512, m)
    return pl.pallas_call(
        _norm_matmul_kernel,
        grid=(m // tm,),
        in_specs=[pl.BlockSpec((tm, d), lambda i: (i, 0)), _resident((1, d)), _resident((d, n))],
        out_specs=pl.BlockSpec((tm, n), lambda i: (i, 0)),
        out_shape=jax.ShapeDtypeStruct((m, n), out_dtype),
        compiler_params=_params(1),
        name="norm_matmul",
    )(x2, gain.reshape(1, d), w)


def _cross_kernel(x_ref, g_ref, wq_ref, kv_ref, wo_ref, o_ref):
    x = x_ref[...]
    h = _rms(x, g_ref[...]).astype(BF16)
    q = _dot(h, wq_ref[...])
    scale = CA_HEAD_DIM ** -0.5
    outs = []
    for hd in range(N_CA_HEADS):
        qh = q[:, hd * CA_HEAD_DIM:(hd + 1) * CA_HEAD_DIM].astype(BF16)
        logits = _dot_nt(qh, kv_ref[:, hd * CA_HEAD_DIM:(hd + 1) * CA_HEAD_DIM]) * scale
        e = jnp.exp(logits - jnp.max(logits, axis=1, keepdims=True))
        pv = _dot(e.astype(BF16), kv_ref[:, CA_W + hd * CA_HEAD_DIM:CA_W + (hd + 1) * CA_HEAD_DIM])
        outs.append(pv / jnp.sum(e, axis=1, keepdims=True))
    o = jnp.concatenate(outs, axis=1).astype(BF16)
    o_ref[...] = x + _dot(o, wo_ref[...])


def _cross_attention(x2, gain, wq, kv, wo, batch, seq):
    m, d = x2.shape
    tm = min(512, seq)
    n_mem = kv.shape[0] // batch
    per_batch = seq // tm
    return pl.pallas_call(
        _cross_kernel,
        grid=(m // tm,),
        in_specs=[pl.BlockSpec((tm, d), lambda i: (i, 0)), _resident((1, d)), _resident((d, CA_W)),
                  pl.BlockSpec((n_mem, 2 * CA_W), lambda i: (i // per_batch, 0)), _resident((CA_W, d))],
        out_specs=pl.BlockSpec((tm, d), lambda i: (i, 0)),
        out_shape=jax.ShapeDtypeStruct((m, d), F32),
        compiler_params=_params(1),
        name="cross_attention",
    )(x2, gain.reshape(1, d), wq, kv, wo)


def _pad_cols(w, n):
    return jnp.pad(w, ((0, 0), (0, n - w.shape[1])))


def _pad_rows(w, n):
    return jnp.pad(w, ((0, n - w.shape[0]), (0, 0)))


def _mixer_weights(w_in):
    offs = np.cumsum((0,) + IN_SPLITS)
    part = lambda j: w_in[:, offs[j]:offs[j + 1]]
    thirds = lambda w: jnp.split(w, 3, axis=1)
    scale = HEAD_DIM ** -0.5
    fq, fk, fv = thirds(part(0))
    dq, dk, dv = thirds(part(2))
    sq, sk, sv = thirds(part(6))
    iq, ik, iw = part(3), part(4), part(5)
    zk = jnp.zeros_like(ik)
    small = jnp.concatenate([part(1), jnp.zeros((w_in.shape[0], SMALL_IDXW_LANE - N_FOX), w_in.dtype), iw], axis=1)
    wb = jnp.concatenate(
        [_pad_cols(dq * scale, PAIR_W), _pad_cols(dk, PAIR_W),
         jnp.concatenate([ik, zk], axis=1), jnp.concatenate([zk, ik], axis=1), iq,
         fq * scale, fk, fv, _pad_cols(dv, PAIR_W),
         _pad_cols(sq * scale, PAIR_W), _pad_cols(sk, PAIR_W), _pad_cols(sv, PAIR_W),
         _pad_cols(small, LANES)], axis=1).astype(BF16)
    return wb, part(7).astype(BF16)


def _hybrid_mixer(x2, cos, sin_signed, batch, seq, gain, w_in, b_fgate, w_fox_out, w_dsa_out, w_sb_out, w_out):
    wb, wg = _mixer_weights(w_in)
    pb, ps = _mixer_in(x2, gain, cos, sin_signed, wb)

    bias_row = _pad_cols(b_fgate.reshape(1, N_FOX).astype(F32), LANES)
    k_side, q_side = _fox_cumsum(ps, bias_row, batch, seq)

    o_fox = _fox_attention(pb, k_side, q_side, batch, seq)
    o_dsa = _dsa_attention(pb, ps, batch, seq)
    o_sb = _sb_attention(pb, batch, seq)
    return _mixer_out(x2, gain, o_fox, o_dsa, o_sb, wg, w_fox_out.astype(BF16),
                      _pad_rows(w_dsa_out, PAIR_W).astype(BF16), _pad_rows(w_sb_out, PAIR_W).astype(BF16),
                      w_out.astype(BF16))


def kernel(x, mem, positions, ffn1_norm, ffn1_w_gu, ffn1_w_down, mix_norm, w_in, b_fgate, w_fox_out, w_dsa_out,
           w_sb_out, w_out, ca_norm, mem_norm, ca_w_q, ca_w_kv, ca_w_o, ffn2_norm, ffn2_w_gu, ffn2_w_down,
           final_norm):
    batch, seq, d = x.shape
    depth = ffn1_norm.shape[0]
    x2 = x.reshape(batch * seq, d)
    mem2 = mem.reshape(-1, d)
    pos2 = positions.reshape(batch * seq, 1).astype(I32)
    half = HEAD_DIM // 2
    inv_freq = jnp.power(ROPE_THETA, -jnp.arange(half, dtype=F32) * (2.0 / HEAD_DIM))
    invf = jnp.tile(inv_freq, LANES // half).reshape(1, LANES)
    cos, sin_signed = _rope_tables(pos2, invf)
    for l in range(depth):
        x2 = _ffn(x2, ffn1_norm[l], ffn1_w_gu[l], ffn1_w_down[l])
        x2 = _hybrid_mixer(x2, cos, sin_signed, batch, seq, mix_norm[l], w_in[l], b_fgate[l], w_fox_out[l],
                           w_dsa_out[l], w_sb_out[l], w_out[l])
        kv = _norm_matmul(mem2, mem_norm[l], ca_w_kv[l].astype(BF16), BF16)
        x2 = _cross_attention(x2, ca_norm[l], ca_w_q[l].astype(BF16), kv, ca_w_o[l].astype(BF16), batch, seq)
        x2 = _ffn(x2, ffn2_norm[l], ffn2_w_gu[l], ffn2_w_down[l],
                  final_gain=final_norm if l == depth - 1 else None)
    return x2.reshape(batch, seq, d)
```

```python
import functools

import jax
import jax.numpy as jnp
import numpy as np
from jax import lax
from jax.experimental import pallas as pl
from jax.experimental.pallas import tpu as pltpu

F32 = jnp.float32
BF16 = jnp.bfloat16
I32 = jnp.int32

D_MODEL = 1024
HEAD_DIM = 64
N_FOX = 6
N_DSA = 5
N_SB = 5
N_IDX_HEADS = 8
IDX_DIM = 64
TOPK_MAX = 256
N_CA_HEADS = 4
CA_HEAD_DIM = 128
D_FF = 2816
ROPE_THETA = 10000.0
NORM_EPS = 1e-6
N_BRANCH = 3
HALF_STEP = 0.5
FOX_W = N_FOX * HEAD_DIM
DSA_W = N_DSA * HEAD_DIM
SB_W = N_SB * HEAD_DIM
CA_W = N_CA_HEADS * CA_HEAD_DIM
IN_SPLITS = (3 * FOX_W, N_FOX, 3 * DSA_W, N_IDX_HEADS * IDX_DIM, IDX_DIM, N_IDX_HEADS, 3 * SB_W,
             N_BRANCH * D_MODEL)

LANES = 128
HEADS_PER_BLOCK = LANES // HEAD_DIM
PAIR_W = 3 * LANES
VMEM_LIMIT = 56 * 2**20

BLK_DSA_Q, BLK_DSA_K = 0, 3
BLK_IDX_KA = 6
BLK_IDX_KB = 7
BLK_IDX_Q = 8
N_ROPE_BLOCKS = 12
BLK_FOX_Q, BLK_FOX_K, BLK_FOX_V = 12, 15, 18
BLK_DSA_V = 21
BLK_SB_Q, BLK_SB_K, BLK_SB_V = 24, 27, 30
N_PROJ_BLOCKS = 33
PROJ_CHUNK = 4 * LANES
SMALL_FGATE_LANE = 0
SMALL_IDXW_LANE = 8

NEG_BIG = -1e30
SB_CUTOFF = -110.0
INT_MIN = -2**31


def _params(n_grid):
    return pltpu.CompilerParams(dimension_semantics=("arbitrary",) * n_grid, vmem_limit_bytes=VMEM_LIMIT)


def _resident(shape):
    nd = len(shape)
    return pl.BlockSpec(shape, lambda *_: (0,) * nd, pipeline_mode=pl.Buffered(1))


def _rms(x, g):
    return x * lax.rsqrt(jnp.mean(x * x, axis=-1, keepdims=True) + NORM_EPS) * g


def _dot(a, b):
    return jnp.dot(a, b, preferred_element_type=F32)


def _dot_nt(a, b):
    return lax.dot_general(a, b, (((1,), (1,)), ((), ())), preferred_element_type=F32)


def _split3(x):
    hi = x.astype(BF16)
    r = x - hi.astype(F32)
    mid = r.astype(BF16)
    lo = (r - mid.astype(F32)).astype(BF16)
    return hi, mid, lo


def _log1p_exp_neg_abs(z):
    return jnp.log1p(jnp.exp(-jnp.abs(z)))


def _ffn_chunks():
    out, c = [], 0
    while c < D_FF:
        w = min(512, D_FF - c)
        out.append((c, w))
        c += w
    return tuple(out)


def _ffn_kernel(*refs, final):
    if final:
        x_ref, g_ref, wgu_ref, wd_ref, fn_ref, o_ref, a_ref = refs
    else:
        x_ref, g_ref, wgu_ref, wd_ref, o_ref, a_ref = refs
    x = x_ref[...]
    h = _rms(x, g_ref[...]).astype(BF16)
    for c0, w in _ffn_chunks():
        g = _dot(h, wgu_ref[:, c0:c0 + w])
        u = _dot(h, wgu_ref[:, D_FF + c0:D_FF + c0 + w])
        a_ref[:, c0:c0 + w] = (g * jax.nn.sigmoid(g) * u).astype(BF16)
    y = x + HALF_STEP * _dot(a_ref[...], wd_ref[...])
    if final:
        y = _rms(y, fn_ref[...])
    o_ref[...] = y


def _ffn(x2, gain, w_gu, w_down, final_gain=None):
    m, d = x2.shape
    tm = min(512, m)
    final = final_gain is not None
    in_specs = [pl.BlockSpec((tm, d), lambda i: (i, 0)), _resident((1, d)),
                _resident((d, 2 * D_FF)), _resident((D_FF, d))]
    args = [x2, gain.reshape(1, d), w_gu.astype(BF16), w_down.astype(BF16)]
    if final:
        in_specs.append(_resident((1, d)))
        args.append(final_gain.reshape(1, d))
    return pl.pallas_call(
        functools.partial(_ffn_kernel, final=final),
        grid=(m // tm,),
        in_specs=in_specs,
        out_specs=pl.BlockSpec((tm, d), lambda i: (i, 0)),
        out_shape=jax.ShapeDtypeStruct((m, d), F32),
        scratch_shapes=[pltpu.VMEM((tm, D_FF), BF16)],
        compiler_params=_params(1),
        name="ffn",
    )(*args)


def _first_half_lanes():
    lane = lax.broadcasted_iota(I32, (1, LANES), 1)
    return (lane % HEAD_DIM) < (HEAD_DIM // 2)


def _rope_tables_kernel(pos_ref, invf_ref, cos_ref, sin_ref):
    ang = pos_ref[...].astype(F32) * invf_ref[...]
    sin = jnp.sin(ang)
    cos_ref[...] = jnp.cos(ang)
    sin_ref[...] = jnp.where(_first_half_lanes(), -sin, sin)


def _rope_tables(pos2, invf):
    m = pos2.shape[0]
    tm = min(512, m)
    tok = pl.BlockSpec((tm, LANES), lambda i: (i, 0))
    return pl.pallas_call(
        _rope_tables_kernel,
        grid=(m // tm,),
        in_specs=[pl.BlockSpec((tm, 1), lambda i: (i, 0)), _resident((1, LANES))],
        out_specs=[tok, tok],
        out_shape=[jax.ShapeDtypeStruct((m, LANES), F32)] * 2,
        compiler_params=_params(1),
        name="rope_tables",
    )(pos2, invf)


def _mixer_in_kernel(x_ref, g_ref, cos_ref, sin_ref, wb_ref, pb_ref, ps_ref):
    h = _rms(x_ref[...], g_ref[...]).astype(BF16)
    cos = cos_ref[...]
    sin_signed = sin_ref[...]
    first_half = _first_half_lanes()
    n_cols = (N_PROJ_BLOCKS + 1) * LANES
    for c0 in range(0, n_cols, PROJ_CHUNK):
        c1 = min(c0 + PROJ_CHUNK, n_cols)
        res = _dot(h, wb_ref[:, c0:c1])
        if c1 == n_cols:
            ps_ref[...] = res[:, c1 - c0 - LANES:]
            c1 -= LANES
            res = res[:, :c1 - c0]
        if c0 < N_ROPE_BLOCKS * LANES:
            parts = []
            for b in range((c1 - c0) // LANES):
                xb = res[:, b * LANES:(b + 1) * LANES]
                partner = jnp.where(first_half, pltpu.roll(xb, LANES - HEAD_DIM // 2, 1),
                                    pltpu.roll(xb, HEAD_DIM // 2, 1))
                parts.append(xb * cos + partner * sin_signed)
            res = jnp.concatenate(parts, axis=1)
        pb_ref[:, c0:c1] = res.astype(BF16)


def _mixer_in(x2, gain, cos, sin_signed, wb):
    m, d = x2.shape
    tm = min(512, m)
    nb = N_PROJ_BLOCKS * LANES
    return pl.pallas_call(
        _mixer_in_kernel,
        grid=(m // tm,),
        in_specs=[pl.BlockSpec((tm, d), lambda i: (i, 0)), _resident((1, d)),
                  pl.BlockSpec((tm, LANES), lambda i: (i, 0)), pl.BlockSpec((tm, LANES), lambda i: (i, 0)),
                  _resident((d, nb + LANES))],
        out_specs=[pl.BlockSpec((tm, nb), lambda i: (i, 0)), pl.BlockSpec((tm, LANES), lambda i: (i, 0))],
        out_shape=[jax.ShapeDtypeStruct((m, nb), BF16), jax.ShapeDtypeStruct((m, LANES), F32)],
        compiler_params=_params(1),
        name="mixer_in",
    )(x2, gain.reshape(1, d), cos, sin_signed, wb)


FOX_FEAT_ROWS = 16


def _fox_feature_maps():
    k_map = np.zeros((3, LANES, FOX_W), np.float32)
    q_map = np.zeros((3, LANES, LANES), np.float32)
    k_one = np.zeros((1, FOX_W), np.float32)
    q_one = np.zeros((1, LANES), np.float32)
    for h in range(N_FOX):
        p, hh = divmod(h, HEADS_PER_BLOCK)
        for j in range(3):
            k_map[j, h, p * LANES + 3 * hh + j] = -1.0
            q_map[j, h, FOX_FEAT_ROWS * h + 6 + j] = 1.0
            k_one[0, p * LANES + 6 + j] = 1.0
            q_one[0, FOX_FEAT_ROWS * h + 3 * hh + j] = 1.0
    return (jnp.asarray(k_map, BF16), jnp.asarray(q_map, BF16), jnp.asarray(k_one), jnp.asarray(q_one))


def _fox_cumsum_kernel(ps_ref, b_ref, kmap_ref, qmap_ref, kone_ref, qone_ref, kf_ref, qf_ref, carry_ref):
    @pl.when(pl.program_id(1) == 0)
    def _():
        carry_ref[...] = jnp.zeros_like(carry_ref)

    tc = ps_ref.shape[0]
    z = ps_ref[...] + b_ref[...]
    log_f = jnp.minimum(z, 0.0) - _log1p_exp_neg_abs(z)
    row = lax.broadcasted_iota(I32, (tc, tc), 0)
    col = lax.broadcasted_iota(I32, (tc, tc), 1)
    tri = jnp.where(col <= row, 1.0, 0.0).astype(BF16)
    hi, mid, lo = _split3(log_f)
    cs = _dot(tri, hi) + _dot(tri, mid) + _dot(tri, lo) + carry_ref[...]
    carry_ref[...] = cs[tc - 1:tc, :]
    terms = _split3(cs)
    kf = kone_ref[...]
    qf = qone_ref[...]
    for j in range(3):
        kf = kf + _dot(terms[j], kmap_ref[j])
        qf = qf + _dot(terms[j], qmap_ref[j])
    kf_ref[...] = kf.astype(BF16)
    qf_ref[...] = qf.astype(BF16)


def _fox_cumsum(ps, bias_row, batch, seq):
    tc = min(512, seq)
    nj = seq // tc
    m = batch * seq
    tok = lambda w: pl.BlockSpec((tc, w), lambda b, j: (b * nj + j, 0))
    k_side, q_tok = pl.pallas_call(
        _fox_cumsum_kernel,
        grid=(batch, nj),
        in_specs=[tok(LANES), _resident((1, LANES)), _resident((3, LANES, FOX_W)), _resident((3, LANES, LANES)),
                  _resident((1, FOX_W)), _resident((1, LANES))],
        out_specs=[tok(FOX_W), tok(LANES)],
        out_shape=[jax.ShapeDtypeStruct((m, FOX_W), BF16), jax.ShapeDtypeStruct((m, LANES), BF16)],
        scratch_shapes=[pltpu.VMEM((1, LANES), F32)],
        compiler_params=_params(2),
        name="fox_cumsum",
    )(ps, bias_row, *_fox_feature_maps())
    q_side = q_tok[:, :N_FOX * FOX_FEAT_ROWS].reshape(batch, seq, N_FOX, FOX_FEAT_ROWS).transpose(0, 2, 3, 1)
    return k_side, q_side


def _head_lane_mask(hh):
    lane = lax.broadcasted_iota(I32, (1, LANES), 1)
    return (lane < HEAD_DIM) if hh == 0 else (lane >= HEAD_DIM)


def _head_row_mask(hh):
    row = lax.broadcasted_iota(I32, (LANES, 1), 0)
    return (row < HEAD_DIM) if hh == 0 else (row >= HEAD_DIM)


def _transpose_bf16(x):
    return x.astype(F32).T.astype(BF16)


def _fill_transposed(src_ref, dst_ref):
    n_blocks, _, tk = dst_ref.shape
    for j in range(n_blocks):
        dst_ref[j] = _transpose_bf16(src_ref[j * tk:(j + 1) * tk, :])


V_ROWS = HEAD_DIM + 16


def _fill_transposed_with_ones(src_ref, dst_ref, n_heads):
    n_blocks, _, tk = dst_ref.shape
    ones_pad = jnp.where(lax.broadcasted_iota(I32, (V_ROWS - HEAD_DIM, tk), 0) == 0, 1.0, 0.0).astype(BF16)
    for j in range(n_blocks):
        v_t = _transpose_bf16(src_ref[j * tk:(j + 1) * tk, :])
        for h in range(n_heads):
            dst_ref[j, h * V_ROWS:h * V_ROWS + HEAD_DIM, :] = v_t[h * HEAD_DIM:(h + 1) * HEAD_DIM, :]
            dst_ref[j, h * V_ROWS + HEAD_DIM:(h + 1) * V_ROWS, :] = ones_pad


def _softmax_steps(carry, scores, values_t):
    stats = []
    for (m, _), s in zip(carry, scores):
        m_new = jnp.maximum(m, jnp.max(s, axis=0, keepdims=True))
        stats.append((m_new, jnp.exp(m - m_new), jnp.exp(s - m_new).astype(BF16)))
    return tuple((m_new, alpha * acc + _dot(v_t, p))
                 for (m_new, alpha, p), (_, acc), v_t in zip(stats, carry, values_t))


def _softmax_init(tq):
    return (jnp.full((1, tq), NEG_BIG, F32), jnp.zeros((V_ROWS, tq), F32))


def _softmax_result(state):
    _, acc = state
    return acc[:HEAD_DIM, :] / acc[HEAD_DIM:HEAD_DIM + 1, :]


def _flash_blocks(n, score_fn, value_fn, carry, s_ref):
    n_heads = len(carry)

    def load(slot):
        return [s_ref[slot, h] for h in range(n_heads)]

    def store(slot, kb):
        for h, s in enumerate(score_fn(kb)):
            s_ref[slot, h] = s

    odd = n % 2
    carry = lax.cond(odd == 1, lambda c: _softmax_steps(c, score_fn(0), value_fn(0)), lambda c: c, carry)

    @pl.when(n >= 2)
    def _():
        store(0, odd)

    def body(j, c):
        kb = odd + 2 * j
        store(1, kb + 1)
        c = _softmax_steps(c, load(0), value_fn(kb))
        store(0, jnp.minimum(kb + 2, n - 1))
        return _softmax_steps(c, load(1), value_fn(kb + 1))

    return lax.fori_loop(0, n // 2, body, carry)


def _heads_to_token_rows(out_t):
    return jnp.concatenate(out_t, axis=0).T.astype(BF16)


def _fox_kernel(q_ref, k_ref, v_ref, kf_ref, qf_ref, o_ref, vt_ref, s_ref):
    tq = q_ref.shape[0]
    tk = vt_ref.shape[2]
    i = pl.program_id(1)

    @pl.when(i == 0)
    def _():
        _fill_transposed_with_ones(v_ref, vt_ref, N_FOX)

    key_pos = lax.broadcasted_iota(I32, (tk, tq), 0)
    qry_pos = lax.broadcasted_iota(I32, (tk, tq), 1)
    causal = [key_pos + d * tk <= qry_pos for d in range(2)]
    pad = jnp.zeros((LANES - FOX_FEAT_ROWS, tq), BF16)
    qx = []
    for p in range(FOX_W // LANES):
        q_t = q_ref[:, p * LANES:(p + 1) * LANES].astype(F32).T
        for hh in range(HEADS_PER_BLOCK):
            qx.append(jnp.concatenate([jnp.where(_head_row_mask(hh), q_t, 0.0).astype(BF16),
                                       qf_ref[0, p * HEADS_PER_BLOCK + hh], pad], axis=0))

    def store(slot, kb):
        k0 = pl.multiple_of(kb * tk, tk)
        for h in range(N_FOX):
            p = h // HEADS_PER_BLOCK
            kx = jnp.concatenate([k_ref[pl.ds(k0, tk), p * LANES:(p + 1) * LANES],
                                  kf_ref[pl.ds(k0, tk), p * LANES:(p + 1) * LANES]], axis=1)
            s_ref[slot, h] = _dot(kx, qx[h])

    def step(carry, slot, kb, mask=None):
        scores = [s_ref[slot, h] for h in range(N_FOX)]
        if mask is not None:
            scores = [jnp.where(mask, s, NEG_BIG) for s in scores]
        return _softmax_steps(carry, scores, [vt_ref[kb, h * V_ROWS:(h + 1) * V_ROWS, :] for h in range(N_FOX)])

    def trip(j, carry):
        kb = 2 * j
        store(1, kb + 1)
        carry = step(carry, 0, kb)
        store(0, kb + 2)
        return step(carry, 1, kb + 1)

    store(0, 0)
    carry = lax.fori_loop(0, i, trip, tuple(_softmax_init(tq) for _ in range(N_FOX)))
    kb = 2 * i
    store(1, kb + 1)
    carry = step(carry, 0, kb, causal[0])
    carry = step(carry, 1, kb + 1, causal[1])
    for p in range(FOX_W // LANES):
        o_ref[:, p * LANES:(p + 1) * LANES] = _heads_to_token_rows(
            [_softmax_result(st) for st in carry[p * HEADS_PER_BLOCK:(p + 1) * HEADS_PER_BLOCK]])


def _fox_attention(pb, k_side, q_side, batch, seq):
    tq = min(512, seq)
    tk = tq // 2
    nq = seq // tq
    m = batch * seq
    return pl.pallas_call(
        _fox_kernel,
        grid=(batch, nq),
        in_specs=[pl.BlockSpec((tq, FOX_W), lambda b, i: (b * nq + i, BLK_FOX_Q // 3)),
                  pl.BlockSpec((seq, FOX_W), lambda b, i: (b, BLK_FOX_K // 3)),
                  pl.BlockSpec((seq, FOX_W), lambda b, i: (b, BLK_FOX_V // 3)),
                  pl.BlockSpec((seq, FOX_W), lambda b, i: (b, 0)),
                  pl.BlockSpec((1, N_FOX, FOX_FEAT_ROWS, tq), lambda b, i: (b, 0, 0, i))],
        out_specs=pl.BlockSpec((tq, FOX_W), lambda b, i: (b * nq + i, 0)),
        out_shape=jax.ShapeDtypeStruct((m, FOX_W), BF16),
        scratch_shapes=[pltpu.VMEM((seq // tk, N_FOX * V_ROWS, tk), BF16), pltpu.VMEM((2, N_FOX, tk, tq), F32)],
        compiler_params=_params(2),
        name="fox_attention",
    )(pb, pb, pb, k_side, q_side)


def _sb_kernel(q_ref, k_ref, v_ref, o_ref, vt_ref):
    tq = q_ref.shape[0]
    tk = vt_ref.shape[2]
    i = pl.program_id(1)

    @pl.when(i == 0)
    def _():
        _fill_transposed(v_ref, vt_ref)

    heads = range(N_SB)
    q_t = [q_ref[:, p * LANES:(p + 1) * LANES].astype(F32).T for p in range(PAIR_W // LANES)]
    qh = [jnp.where(_head_row_mask(h % HEADS_PER_BLOCK), q_t[h // HEADS_PER_BLOCK], 0.0).astype(BF16) for h in heads]
    r = lax.broadcasted_iota(I32, (tk, tk), 0)
    c = lax.broadcasted_iota(I32, (tk, tk), 1)
    upper = jnp.where(c > r, 1.0, 0.0).astype(BF16)
    key_pos = lax.broadcasted_iota(I32, (tk, tq), 0)
    qry_pos = lax.broadcasted_iota(I32, (tk, tq), 1) + i * tq

    def process(kb, state, masked):
        k0 = pl.multiple_of(kb * tk, tk)
        strict = key_pos + k0 < qry_pos
        z = [_dot(k_ref[pl.ds(k0, tk), (h // HEADS_PER_BLOCK) * LANES:(h // HEADS_PER_BLOCK + 1) * LANES], qh[h])
             for h in heads]
        log_beta, log_1m = [], []
        for zh in z:
            l1m = jnp.minimum(-zh, 0.0) - jnp.log(1.0 + jnp.exp(-jnp.abs(zh)))
            log_beta.append(l1m + zh)
            log_1m.append(jnp.where(strict, l1m, 0.0) if masked else l1m)
        after = []
        for h in heads:
            hi, mid, lo = _split3(log_1m[h])
            after.append(_dot(upper, hi) + _dot(upper, mid) + _dot(upper, lo) + state[h][0])
        weights = []
        for h in heads:
            a = jnp.exp(log_beta[h] + after[h])
            weights.append((jnp.where(strict, a, 0.0) if masked else a).astype(BF16))
        return tuple((state[h][0] + jnp.sum(log_1m[h], axis=0, keepdims=True),
                      state[h][1] + _dot(vt_ref[kb, h * HEAD_DIM:(h + 1) * HEAD_DIM, :], weights[h]))
                     for h in heads)

    def run_max(state):
        m = state[0][0]
        for run, _ in state[1:]:
            m = jnp.maximum(m, run)
        return jnp.max(m)

    state = tuple((jnp.zeros((1, tq), F32), jnp.zeros((HEAD_DIM, tq), F32)) for _ in heads)
    n_diag = tq // tk
    for d in range(n_diag):
        state = process((i + 1) * n_diag - 1 - d, state, True)

    def cond(carry):
        kb, worst, _ = carry
        return jnp.logical_and(kb >= 0, worst > SB_CUTOFF)

    def body(carry):
        kb, _, st = carry
        st = process(kb, st, False)
        return kb - 1, run_max(st), st

    _, _, state = lax.while_loop(cond, body, (i * n_diag - 1, run_max(state), state))
    for p in range(PAIR_W // LANES):
        outs = [state[h][1] for h in heads if h // HEADS_PER_BLOCK == p]
        outs += [jnp.zeros((HEAD_DIM, tq), F32)] * (HEADS_PER_BLOCK - len(outs))
        o_ref[:, p * LANES:(p + 1) * LANES] = _heads_to_token_rows(outs)


def _sb_attention(pb, batch, seq):
    tq = min(256, seq)
    tk = min(128, seq)
    nq = seq // tq
    m = batch * seq
    return pl.pallas_call(
        _sb_kernel,
        grid=(batch, nq),
        in_specs=[pl.BlockSpec((tq, PAIR_W), lambda b, i: (b * nq + i, BLK_SB_Q // 3)),
                  pl.BlockSpec((seq, PAIR_W), lambda b, i: (b, BLK_SB_K // 3)),
                  pl.BlockSpec((seq, PAIR_W), lambda b, i: (b, BLK_SB_V // 3))],
        out_specs=pl.BlockSpec((tq, PAIR_W), lambda b, i: (b * nq + i, 0)),
        out_shape=jax.ShapeDtypeStruct((m, PAIR_W), BF16),
        scratch_shapes=[pltpu.VMEM((seq // tk, PAIR_W, tk), BF16)],
        compiler_params=_params(2),
        name="sb_attention",
    )(pb, pb, pb)


HALF_BITS = 16
HALF_OFFSET = 1 << (HALF_BITS - 1)


def _dsa_kernel(q_ref, iq_ref, w_ref, k_ref, v_ref, ka_ref, kb_ref, o_ref, key_ref, hi_ref, lo_ref, bias_ref,
                vt_ref, s_ref, *, topk, seq):
    t = q_ref.shape[0]
    i = pl.program_id(1)

    @pl.when(i == 0)
    def _():
        _fill_transposed_with_ones(v_ref, vt_ref, N_DSA)

    n_kb = i + 1
    kf = float(topk)
    imin = jnp.int32(INT_MIN)
    key_pos = lax.broadcasted_iota(I32, (t, t), 0)
    qry_pos = lax.broadcasted_iota(I32, (t, t), 1) + i * t

    iq_t = [_transpose_bf16(iq_ref[:, g * LANES:(g + 1) * LANES]) for g in range(N_IDX_HEADS // 2)]
    w_t = (w_ref[...] * (N_IDX_HEADS ** -0.5) * (IDX_DIM ** -0.5)).T

    def index_block(kb):
        k0 = pl.multiple_of(kb * t, t)
        ka = ka_ref[pl.ds(k0, t), :]
        kb_ = kb_ref[pl.ds(k0, t), :]
        score = jnp.zeros((t, t), F32)
        for g in range(N_IDX_HEADS // 2):
            lo = SMALL_IDXW_LANE + 2 * g
            score = score + jnp.maximum(_dot(ka, iq_t[g]), 0.0) * w_t[lo:lo + 1, :]
            score = score + jnp.maximum(_dot(kb_, iq_t[g]), 0.0) * w_t[lo + 1:lo + 2, :]
        bits = lax.bitcast_convert_type(score + 0.0, I32)
        key = jnp.where(bits < 0, bits ^ jnp.int32(0x7FFFFFFF), bits)
        key = jnp.where(key_pos + k0 <= qry_pos, key, imin)
        key_ref[pl.ds(k0, t), :] = key
        hi_ref[pl.ds(k0, t), :] = lax.shift_right_arithmetic(key, HALF_BITS).astype(jnp.int16)
        lo_ref[pl.ds(k0, t), :] = ((key & (2 * HALF_OFFSET - 1)) - HALF_OFFSET).astype(jnp.int16)

    odd = n_kb % 2

    @pl.when(odd == 1)
    def _():
        index_block(0)

    def index_pair(j, _):
        index_block(odd + 2 * j)
        index_block(odd + 2 * j + 1)
        return 0

    lax.fori_loop(0, n_kb // 2, index_pair, 0)

    masked16 = jnp.full((t, t), -HALF_OFFSET, jnp.int16)

    @pl.when(odd == 1)
    def _():
        k0 = pl.multiple_of(n_kb * t, t)
        hi_ref[pl.ds(k0, t), :] = masked16
        lo_ref[pl.ds(k0, t), :] = masked16

    n_scan = (n_kb + 1) // 2
    scan_rows = 2 * t

    def count16(ref, pred):
        rows = 16
        def body(trip, acc):
            k0 = pl.multiple_of(trip * scan_rows, scan_rows)
            ind = pred(ref[pl.ds(k0, scan_rows), :])
            parts = [ind[r:r + rows, :] for r in range(0, scan_rows, rows)]
            while len(parts) > 1:
                parts = [parts[j] + parts[j + 1] for j in range(0, len(parts), 2)]
            return acc + parts[0]
        acc = lax.fori_loop(0, n_scan, body, jnp.zeros((rows, t), jnp.int16))
        return jnp.sum(acc.astype(F32), axis=0, keepdims=True)

    one16, zero16 = jnp.int16(1), jnp.int16(0)

    def kth_half(ref, need, n_all):
        def bit_step(it, state):
            prefix, cnt_prefix = state
            cand_u = prefix | lax.shift_left(jnp.int32(1), HALF_BITS - 1 - it)
            cand = (cand_u - HALF_OFFSET).astype(jnp.int16)
            cnt = count16(ref, lambda blk: jnp.where(blk >= cand, one16, zero16))
            take = cnt >= need
            return jnp.where(take, cand_u, prefix), jnp.where(take, cnt, cnt_prefix)
        prefix, cnt = lax.fori_loop(0, HALF_BITS, bit_step, (jnp.zeros((1, t), I32), n_all))
        return prefix - HALF_OFFSET, cnt


    n_scanned = (n_scan * scan_rows).astype(F32)
    kth_hi, n_ge_hi = kth_half(hi_ref, kf, jnp.full((1, t), n_scanned, F32))
    kth_hi16 = kth_hi.astype(jnp.int16)

    def above_and_bucket(trip, acc):
        rows = 16
        k0 = pl.multiple_of(trip * scan_rows, scan_rows)
        hi = hi_ref[pl.ds(k0, scan_rows), :]
        lo_ref[pl.ds(k0, scan_rows), :] = jnp.where(hi == kth_hi16, lo_ref[pl.ds(k0, scan_rows), :],
                                                    jnp.int16(-HALF_OFFSET))
        ind = jnp.where(hi > kth_hi16, one16, zero16)
        parts = [ind[r:r + rows, :] for r in range(0, scan_rows, rows)]
        while len(parts) > 1:
            parts = [parts[j] + parts[j + 1] for j in range(0, len(parts), 2)]
        return acc + parts[0]

    n_gt_hi = jnp.sum(lax.fori_loop(0, n_scan, above_and_bucket, jnp.zeros((16, t), jnp.int16)).astype(F32),
                      axis=0, keepdims=True)
    need_lo = kf - n_gt_hi
    kth_lo, n_ge_lo = kth_half(lo_ref, need_lo, n_ge_hi - n_gt_hi)
    kth = lax.shift_left(kth_hi, HALF_BITS) + (kth_lo + HALF_OFFSET)

    excess = jnp.where(kth == imin, float(2 * seq), n_ge_lo - need_lo)
    any_excess = jnp.max(excess) > 0.0

    @pl.when(jnp.logical_not(any_excess))
    def _():
        def select(kb, _):
            k0 = pl.multiple_of(kb * t, t)
            bias_ref[pl.ds(k0, t), :] = jnp.where(key_ref[pl.ds(k0, t), :] >= kth, 0.0, NEG_BIG)
            return 0

        lax.fori_loop(0, n_kb, select, 0)

    @pl.when(any_excess)
    def _():
        r = lax.broadcasted_iota(I32, (t, t), 0)
        c = lax.broadcasted_iota(I32, (t, t), 1)
        later = jnp.where(c > r, 1.0, 0.0).astype(BF16)

        def select(step, tied_after):
            kb = n_kb - 1 - step
            k0 = pl.multiple_of(kb * t, t)
            kblk = key_ref[pl.ds(k0, t), :]
            tied = kblk == kth
            tied01 = jnp.where(tied, 1.0, 0.0).astype(BF16)
            after = _dot(later, tied01) + tied_after
            tie_bias = jnp.where(tied, jnp.where(after >= excess, 0.0, NEG_BIG), NEG_BIG)
            bias_ref[pl.ds(k0, t), :] = jnp.where(kblk > kth, 0.0, tie_bias)
            return after[0:1, :] + tied01[0:1, :].astype(F32)

        lax.fori_loop(0, n_kb, select, jnp.zeros((1, t), F32))

    q_t = [q_ref[:, p * LANES:(p + 1) * LANES].astype(F32).T for p in range(PAIR_W // LANES)]
    qh = [jnp.where(_head_row_mask(h % HEADS_PER_BLOCK), q_t[h // HEADS_PER_BLOCK], 0.0).astype(BF16)
          for h in range(N_DSA)]

    def scores(kb):
        k0 = pl.multiple_of(kb * t, t)
        bias = bias_ref[pl.ds(k0, t), :]
        return [_dot(k_ref[pl.ds(k0, t), (h // HEADS_PER_BLOCK) * LANES:(h // HEADS_PER_BLOCK + 1) * LANES], qh[h])
                + bias for h in range(N_DSA)]

    def values_t(kb):
        return [vt_ref[kb, h * V_ROWS:(h + 1) * V_ROWS, :] for h in range(N_DSA)]

    carry = _flash_blocks(n_kb, scores, values_t, tuple(_softmax_init(t) for _ in range(N_DSA)), s_ref)
    outs = [_softmax_result(st) for st in carry]
    outs += [jnp.zeros((HEAD_DIM, t), F32)] * (PAIR_W // HEAD_DIM - N_DSA)
    for p in range(PAIR_W // LANES):
        o_ref[:, p * LANES:(p + 1) * LANES] = _heads_to_token_rows(
            outs[p * HEADS_PER_BLOCK:(p + 1) * HEADS_PER_BLOCK])


def _dsa_attention(pb, ps, batch, seq):
    t = min(256, seq)
    nq = seq // t
    m = batch * seq
    topk = min(TOPK_MAX, seq // 4)
    return pl.pallas_call(
        functools.partial(_dsa_kernel, topk=topk, seq=seq),
        grid=(batch, nq),
        in_specs=[pl.BlockSpec((t, PAIR_W), lambda b, i: (b * nq + i, BLK_DSA_Q // 3)),
                  pl.BlockSpec((t, 4 * LANES), lambda b, i: (b * nq + i, BLK_IDX_Q // 4)),
                  pl.BlockSpec((t, LANES), lambda b, i: (b * nq + i, 0)),
                  pl.BlockSpec((seq, PAIR_W), lambda b, i: (b, BLK_DSA_K // 3)),
                  pl.BlockSpec((seq, PAIR_W), lambda b, i: (b, BLK_DSA_V // 3)),
                  pl.BlockSpec((seq, LANES), lambda b, i: (b, BLK_IDX_KA)),
                  pl.BlockSpec((seq, LANES), lambda b, i: (b, BLK_IDX_KB))],
        out_specs=pl.BlockSpec((t, PAIR_W), lambda b, i: (b * nq + i, 0)),
        out_shape=jax.ShapeDtypeStruct((m, PAIR_W), BF16),
        scratch_shapes=[pltpu.VMEM((seq, t), I32), pltpu.VMEM((seq + t, t), jnp.int16),
                        pltpu.VMEM((seq + t, t), jnp.int16),
                        pltpu.VMEM((seq, t), F32),
                        pltpu.VMEM((nq, N_DSA * V_ROWS, t), BF16), pltpu.VMEM((2, N_DSA, t, t), F32)],
        compiler_params=_params(2),
        name="dsa_attention",
    )(pb, pb, ps, pb, pb, pb, pb)


def _mixer_out_kernel(x_ref, gain_ref, of_ref, od_ref, os_ref, wg_ref, wf_ref, wd_ref, ws_ref, wo_ref, o_ref):
    d = x_ref.shape[1]
    x = x_ref[...]
    h = _rms(x, gain_ref[...]).astype(BF16)
    merged = None
    for b, (o_b, w_b) in enumerate(((of_ref, wf_ref), (od_ref, wd_ref), (os_ref, ws_ref))):
        term = jax.nn.sigmoid(_dot(h, wg_ref[:, b * d:(b + 1) * d])) * _dot(o_b[...], w_b[...])
        merged = term if merged is None else merged + term
    o_ref[...] = x + _dot(merged.astype(BF16), wo_ref[...])


def _mixer_out(x2, gain, o_fox, o_dsa, o_sb, wg, wf, wd, ws, wo):
    m, d = x2.shape
    tm = min(512, m)
    row = lambda w: pl.BlockSpec((tm, w), lambda i: (i, 0))
    return pl.pallas_call(
        _mixer_out_kernel,
        grid=(m // tm,),
        in_specs=[row(d), _resident((1, d)), row(FOX_W), row(PAIR_W), row(PAIR_W), _resident((d, N_BRANCH * d)),
                  _resident((FOX_W, d)), _resident((PAIR_W, d)), _resident((PAIR_W, d)), _resident((d, d))],
        out_specs=row(d),
        out_shape=jax.ShapeDtypeStruct((m, d), F32),
        compiler_params=_params(1),
        name="mixer_out",
    )(x2, gain.reshape(1, d), o_fox, o_dsa, o_sb, wg, wf, wd, ws, wo)


def _norm_matmul_kernel(x_ref, g_ref, w_ref, o_ref):
    o_ref[...] = _dot(_rms(x_ref[...], g_ref[...]).astype(BF16), w_ref[...]).astype(o_ref.dtype)


def _norm_matmul(x2, gain, w, out_dtype):
    m, d = x2.shape
    n = w.shape[1]
    tm = min(512, m)
    return pl.pallas_call(
        _norm_matmul_kernel,
        grid=(m // tm,),
        in_specs=[pl.BlockSpec((tm, d), lambda i: (i, 0)), _resident((1, d)), _resident((d, n))],
        out_specs=pl.BlockSpec((tm, n), lambda i: (i, 0)),
        out_shape=jax.ShapeDtypeStruct((m, n), out_dtype),
        compiler_params=_params(1),
        name="norm_matmul",
    )(x2, gain.reshape(1, d), w)


def _cross_kernel(x_ref, g_ref, wq_ref, kv_ref, wo_ref, o_ref):
    x = x_ref[...]
    h = _rms(x, g_ref[...]).astype(BF16)
    q = _dot(h, wq_ref[...])
    scale = CA_HEAD_DIM ** -0.5
    outs = []
    for hd in range(N_CA_HEADS):
        qh = q[:, hd * CA_HEAD_DIM:(hd + 1) * CA_HEAD_DIM].astype(BF16)
        logits = _dot_nt(qh, kv_ref[:, hd * CA_HEAD_DIM:(hd + 1) * CA_HEAD_DIM]) * scale
        e = jnp.exp(logits - jnp.max(logits, axis=1, keepdims=True))
        pv = _dot(e.astype(BF16), kv_ref[:, CA_W + hd * CA_HEAD_DIM:CA_W + (hd + 1) * CA_HEAD_DIM])
        outs.append(pv / jnp.sum(e, axis=1, keepdims=True))
    o = jnp.concatenate(outs, axis=1).astype(BF16)
    o_ref[...] = x + _dot(o, wo_ref[...])


def _cross_attention(x2, gain, wq, kv, wo, batch, seq):
    m, d = x2.shape
    tm = min(512, seq)
    n_mem = kv.shape[0] // batch
    per_batch = seq // tm
    return pl.pallas_call(
        _cross_kernel,
        grid=(m // tm,),
        in_specs=[pl.BlockSpec((tm, d), lambda i: (i, 0)), _resident((1, d)), _resident((d, CA_W)),
                  pl.BlockSpec((n_mem, 2 * CA_W), lambda i: (i // per_batch, 0)), _resident((CA_W, d))],
        out_specs=pl.BlockSpec((tm, d), lambda i: (i, 0)),
        out_shape=jax.ShapeDtypeStruct((m, d), F32),
        compiler_params=_params(1),
        name="cross_attention",
    )(x2, gain.reshape(1, d), wq, kv, wo)


def _pad_cols(w, n):
    return jnp.pad(w, ((0, 0), (0, n - w.shape[1])))


def _pad_rows(w, n):
    return jnp.pad(w, ((0, n - w.shape[0]), (0, 0)))


def _mixer_weights(w_in):
    offs = np.cumsum((0,) + IN_SPLITS)
    part = lambda j: w_in[:, offs[j]:offs[j + 1]]
    thirds = lambda w: jnp.split(w, 3, axis=1)
    scale = HEAD_DIM ** -0.5
    fq, fk, fv = thirds(part(0))
    dq, dk, dv = thirds(part(2))
    sq, sk, sv = thirds(part(6))
    iq, ik, iw = part(3), part(4), part(5)
    zk = jnp.zeros_like(ik)
    small = jnp.concatenate([part(1), jnp.zeros((w_in.shape[0], SMALL_IDXW_LANE - N_FOX), w_in.dtype), iw], axis=1)
    wb = jnp.concatenate(
        [_pad_cols(dq * scale, PAIR_W), _pad_cols(dk, PAIR_W),
         jnp.concatenate([ik, zk], axis=1), jnp.concatenate([zk, ik], axis=1), iq,
         fq * scale, fk, fv, _pad_cols(dv, PAIR_W),
         _pad_cols(sq * scale, PAIR_W), _pad_cols(sk, PAIR_W), _pad_cols(sv, PAIR_W),
         _pad_cols(small, LANES)], axis=1).astype(BF16)
    return wb, part(7).astype(BF16)


def _hybrid_mixer(x2, cos, sin_signed, batch, seq, gain, w_in, b_fgate, w_fox_out, w_dsa_out, w_sb_out, w_out):
    wb, wg = _mixer_weights(w_in)
    pb, ps = _mixer_in(x2, gain, cos, sin_signed, wb)

    bias_row = _pad_cols(b_fgate.reshape(1, N_FOX).astype(F32), LANES)
    k_side, q_side = _fox_cumsum(ps, bias_row, batch, seq)

    o_fox = _fox_attention(pb, k_side, q_side, batch, seq)
    o_dsa = _dsa_attention(pb, ps, batch, seq)
    o_sb = _sb_attention(pb, batch, seq)
    return _mixer_out(x2, gain, o_fox, o_dsa, o_sb, wg, w_fox_out.astype(BF16),
                      _pad_rows(w_dsa_out, PAIR_W).astype(BF16), _pad_rows(w_sb_out, PAIR_W).astype(BF16),
                      w_out.astype(BF16))


def kernel(x, mem, positions, ffn1_norm, ffn1_w_gu, ffn1_w_down, mix_norm, w_in, b_fgate, w_fox_out, w_dsa_out,
           w_sb_out, w_out, ca_norm, mem_norm, ca_w_q, ca_w_kv, ca_w_o, ffn2_norm, ffn2_w_gu, ffn2_w_down,
           final_norm):
    batch, seq, d = x.shape
    depth = ffn1_norm.shape[0]
    x2 = x.reshape(batch * seq, d)
    mem2 = mem.reshape(-1, d)
    pos2 = positions.reshape(batch * seq, 1).astype(I32)
    half = HEAD_DIM // 2
    inv_freq = jnp.power(ROPE_THETA, -jnp.arange(half, dtype=F32) * (2.0 / HEAD_DIM))
    invf = jnp.tile(inv_freq, LANES // half).reshape(1, LANES)
    cos, sin_signed = _rope_tables(pos2, invf)
    for l in range(depth):
        x2 = _ffn(x2, ffn1_norm[l], ffn1_w_gu[l], ffn1_w_down[l])
        x2 = _hybrid_mixer(x2, cos, sin_signed, batch, seq, mix_norm[l], w_in[l], b_fgate[l], w_fox_out[l],
                           w_dsa_out[l], w_sb_out[l], w_out[l])
        kv = _norm_matmul(mem2, mem_norm[l], ca_w_kv[l].astype(BF16), BF16)
        x2 = _cross_attention(x2, ca_norm[l], ca_w_q[l].astype(BF16), kv, ca_w_o[l].astype(BF16), batch, seq)
        x2 = _ffn(x2, ffn2_norm[l], ffn2_w_gu[l], ffn2_w_down[l],
                  final_gain=final_norm if l == depth - 1 else None)
    return x2.reshape(batch, seq, d)
```

```python
import functools

import jax
import jax.numpy as jnp
import numpy as np
from jax import lax
from jax.experimental import pallas as pl
from jax.experimental.pallas import tpu as pltpu

F32 = jnp.float32
BF16 = jnp.bfloat16
I32 = jnp.int32

D_MODEL = 1024
HEAD_DIM = 64
N_FOX = 6
N_DSA = 5
N_SB = 5
N_IDX_HEADS = 8
IDX_DIM = 64
TOPK_MAX = 256
N_CA_HEADS = 4
CA_HEAD_DIM = 128
D_FF = 2816
ROPE_THETA = 10000.0
NORM_EPS = 1e-6
N_BRANCH = 3
HALF_STEP = 0.5
FOX_W = N_FOX * HEAD_DIM
DSA_W = N_DSA * HEAD_DIM
SB_W = N_SB * HEAD_DIM
CA_W = N_CA_HEADS * CA_HEAD_DIM
IN_SPLITS = (3 * FOX_W, N_FOX, 3 * DSA_W, N_IDX_HEADS * IDX_DIM, IDX_DIM, N_IDX_HEADS, 3 * SB_W,
             N_BRANCH * D_MODEL)

LANES = 128
HEADS_PER_BLOCK = LANES // HEAD_DIM
PAIR_W = 3 * LANES
VMEM_LIMIT = 56 * 2**20

BLK_DSA_Q, BLK_DSA_K = 0, 3
BLK_IDX_KA = 6
BLK_IDX_KB = 7
BLK_IDX_Q = 8
N_ROPE_BLOCKS = 12
BLK_FOX_Q, BLK_FOX_K, BLK_FOX_V = 12, 15, 18
BLK_DSA_V = 21
BLK_SB_Q, BLK_SB_K, BLK_SB_V = 24, 27, 30
N_PROJ_BLOCKS = 33
PROJ_CHUNK = 4 * LANES
SMALL_FGATE_LANE = 0
SMALL_IDXW_LANE = 8

NEG_BIG = -1e30
SB_CUTOFF = -110.0
INT_MIN = -2**31


def _params(n_grid):
    return pltpu.CompilerParams(dimension_semantics=("arbitrary",) * n_grid, vmem_limit_bytes=VMEM_LIMIT)


def _resident(shape):
    nd = len(shape)
    return pl.BlockSpec(shape, lambda *_: (0,) * nd, pipeline_mode=pl.Buffered(1))


def _rms(x, g):
    return x * lax.rsqrt(jnp.mean(x * x, axis=-1, keepdims=True) + NORM_EPS) * g


def _dot(a, b):
    return jnp.dot(a, b, preferred_element_type=F32)


def _dot_nt(a, b):
    return lax.dot_general(a, b, (((1,), (1,)), ((), ())), preferred_element_type=F32)


def _split3(x):
    hi = x.astype(BF16)
    r = x - hi.astype(F32)
    mid = r.astype(BF16)
    lo = (r - mid.astype(F32)).astype(BF16)
    return hi, mid, lo


def _log1p_exp_neg_abs(z):
    return jnp.log1p(jnp.exp(-jnp.abs(z)))


def _ffn_chunks():
    out, c = [], 0
    while c < D_FF:
        w = min(512, D_FF - c)
        out.append((c, w))
        c += w
    return tuple(out)


def _ffn_kernel(*refs, final):
    if final:
        x_ref, g_ref, wgu_ref, wd_ref, fn_ref, o_ref, a_ref = refs
    else:
        x_ref, g_ref, wgu_ref, wd_ref, o_ref, a_ref = refs
    x = x_ref[...]
    h = _rms(x, g_ref[...]).astype(BF16)
    for c0, w in _ffn_chunks():
        g = _dot(h, wgu_ref[:, c0:c0 + w])
        u = _dot(h, wgu_ref[:, D_FF + c0:D_FF + c0 + w])
        a_ref[:, c0:c0 + w] = (g * jax.nn.sigmoid(g) * u).astype(BF16)
    y = x + HALF_STEP * _dot(a_ref[...], wd_ref[...])
    if final:
        y = _rms(y, fn_ref[...])
    o_ref[...] = y


def _ffn(x2, gain, w_gu, w_down, final_gain=None):
    m, d = x2.shape
    tm = min(512, m)
    final = final_gain is not None
    in_specs = [pl.BlockSpec((tm, d), lambda i: (i, 0)), _resident((1, d)),
                _resident((d, 2 * D_FF)), _resident((D_FF, d))]
    args = [x2, gain.reshape(1, d), w_gu.astype(BF16), w_down.astype(BF16)]
    if final:
        in_specs.append(_resident((1, d)))
        args.append(final_gain.reshape(1, d))
    return pl.pallas_call(
        functools.partial(_ffn_kernel, final=final),
        grid=(m // tm,),
        in_specs=in_specs,
        out_specs=pl.BlockSpec((tm, d), lambda i: (i, 0)),
        out_shape=jax.ShapeDtypeStruct((m, d), F32),
        scratch_shapes=[pltpu.VMEM((tm, D_FF), BF16)],
        compiler_params=_params(1),
        name="ffn",
    )(*args)


def _first_half_lanes():
    lane = lax.broadcasted_iota(I32, (1, LANES), 1)
    return (lane % HEAD_DIM) < (HEAD_DIM // 2)


def _rope_tables_kernel(pos_ref, invf_ref, cos_ref, sin_ref):
    ang = pos_ref[...].astype(F32) * invf_ref[...]
    sin = jnp.sin(ang)
    cos_ref[...] = jnp.cos(ang)
    sin_ref[...] = jnp.where(_first_half_lanes(), -sin, sin)


def _rope_tables(pos2, invf):
    m = pos2.shape[0]
    tm = min(512, m)
    tok = pl.BlockSpec((tm, LANES), lambda i: (i, 0))
    return pl.pallas_call(
        _rope_tables_kernel,
        grid=(m // tm,),
        in_specs=[pl.BlockSpec((tm, 1), lambda i: (i, 0)), _resident((1, LANES))],
        out_specs=[tok, tok],
        out_shape=[jax.ShapeDtypeStruct((m, LANES), F32)] * 2,
        compiler_params=_params(1),
        name="rope_tables",
    )(pos2, invf)


FOX_FEAT_ROWS = 16


def _fox_feature_maps():
    k_map = np.zeros((3, LANES, FOX_W), np.float32)
    q_map = np.zeros((3, LANES, LANES), np.float32)
    k_one = np.zeros((1, FOX_W), np.float32)
    q_one = np.zeros((1, LANES), np.float32)
    for h in range(N_FOX):
        p, hh = divmod(h, HEADS_PER_BLOCK)
        for j in range(3):
            k_map[j, h, p * LANES + 3 * hh + j] = -1.0
            q_map[j, h, FOX_FEAT_ROWS * h + 6 + j] = 1.0
            k_one[0, p * LANES + 6 + j] = 1.0
            q_one[0, FOX_FEAT_ROWS * h + 3 * hh + j] = 1.0
    return (jnp.asarray(k_map, BF16), jnp.asarray(q_map, BF16), jnp.asarray(k_one), jnp.asarray(q_one))


def _fox_decay_features(logits, b_ref, kmap_ref, qmap_ref, kone_ref, qone_ref, kf_ref, qf_ref, carry_ref):
    tc = logits.shape[0]
    z = logits + b_ref[...]
    log_f = jnp.minimum(z, 0.0) - _log1p_exp_neg_abs(z)
    row = lax.broadcasted_iota(I32, (tc, tc), 0)
    col = lax.broadcasted_iota(I32, (tc, tc), 1)
    tri = jnp.where(col <= row, 1.0, 0.0).astype(BF16)
    hi, mid, lo = _split3(log_f)
    yield
    cs = _dot(tri, hi) + _dot(tri, mid) + _dot(tri, lo) + carry_ref[...]
    carry_ref[...] = cs[tc - 1:tc, :]
    terms = _split3(cs)
    yield
    kf = kone_ref[...]
    qf = qone_ref[...]
    for j in range(3):
        kf = kf + _dot(terms[j], kmap_ref[j])
        qf = qf + _dot(terms[j], qmap_ref[j])
    kf_ref[...] = kf.astype(BF16)
    qf_ref[...] = qf.astype(BF16)


def _mixer_in_kernel(x_ref, g_ref, cos_ref, sin_ref, wb_ref, b_ref, kmap_ref, qmap_ref, kone_ref, qone_ref,
                     pb_ref, ps_ref, kf_ref, qf_ref, carry_ref, *, tiles_per_seq):
    @pl.when(pl.program_id(0) % tiles_per_seq == 0)
    def _():
        carry_ref[...] = jnp.zeros_like(carry_ref)

    h = _rms(x_ref[...], g_ref[...]).astype(BF16)
    cos = cos_ref[...]
    sin_signed = sin_ref[...]
    first_half = _first_half_lanes()
    n_cols = (N_PROJ_BLOCKS + 1) * LANES
    starts = list(range(0, n_cols, PROJ_CHUNK))
    decay_stages = iter(())
    for c0 in starts[-1:] + starts[:-1]:
        c1 = min(c0 + PROJ_CHUNK, n_cols)
        res = _dot(h, wb_ref[:, c0:c1])
        next(decay_stages, None)
        if c1 == n_cols:
            small = res[:, c1 - c0 - LANES:]
            ps_ref[...] = small
            decay_stages = _fox_decay_features(small, b_ref, kmap_ref, qmap_ref, kone_ref, qone_ref, kf_ref,
                                               qf_ref, carry_ref)
            next(decay_stages)
            c1 -= LANES
            res = res[:, :c1 - c0]
        if c0 < N_ROPE_BLOCKS * LANES:
            parts = []
            for b in range((c1 - c0) // LANES):
                xb = res[:, b * LANES:(b + 1) * LANES]
                partner = jnp.where(first_half, pltpu.roll(xb, LANES - HEAD_DIM // 2, 1),
                                    pltpu.roll(xb, HEAD_DIM // 2, 1))
                parts.append(xb * cos + partner * sin_signed)
            res = jnp.concatenate(parts, axis=1)
        pb_ref[:, c0:c1] = res.astype(BF16)


def _mixer_in(x2, gain, cos, sin_signed, wb, bias_row, batch, seq):
    m, d = x2.shape
    tm = min(512, seq)
    nb = N_PROJ_BLOCKS * LANES
    tok = lambda w: pl.BlockSpec((tm, w), lambda i: (i, 0))
    pb, ps, k_side, q_tok = pl.pallas_call(
        functools.partial(_mixer_in_kernel, tiles_per_seq=seq // tm),
        grid=(m // tm,),
        in_specs=[tok(d), _resident((1, d)), tok(LANES), tok(LANES), _resident((d, nb + LANES)),
                  _resident((1, LANES)), _resident((3, LANES, FOX_W)), _resident((3, LANES, LANES)),
                  _resident((1, FOX_W)), _resident((1, LANES))],
        out_specs=[tok(nb), tok(LANES), tok(FOX_W), tok(LANES)],
        out_shape=[jax.ShapeDtypeStruct((m, nb), BF16), jax.ShapeDtypeStruct((m, LANES), F32),
                   jax.ShapeDtypeStruct((m, FOX_W), BF16), jax.ShapeDtypeStruct((m, LANES), BF16)],
        scratch_shapes=[pltpu.VMEM((1, LANES), F32)],
        compiler_params=_params(1),
        name="mixer_in",
    )(x2, gain.reshape(1, d), cos, sin_signed, wb, bias_row, *_fox_feature_maps())
    q_side = q_tok[:, :N_FOX * FOX_FEAT_ROWS].reshape(batch, seq, N_FOX, FOX_FEAT_ROWS).transpose(0, 2, 3, 1)
    return pb, ps, k_side, q_side


def _head_lane_mask(hh):
    lane = lax.broadcasted_iota(I32, (1, LANES), 1)
    return (lane < HEAD_DIM) if hh == 0 else (lane >= HEAD_DIM)


def _head_row_mask(hh):
    row = lax.broadcasted_iota(I32, (LANES, 1), 0)
    return (row < HEAD_DIM) if hh == 0 else (row >= HEAD_DIM)


def _transpose_bf16(x):
    return x.astype(F32).T.astype(BF16)


def _fill_transposed(src_ref, dst_ref):
    n_blocks, _, tk = dst_ref.shape
    for j in range(n_blocks):
        dst_ref[j] = _transpose_bf16(src_ref[j * tk:(j + 1) * tk, :])


V_ROWS = HEAD_DIM + 16


def _fill_transposed_with_ones(src_ref, dst_ref, n_heads):
    n_blocks, _, tk = dst_ref.shape
    ones_pad = jnp.where(lax.broadcasted_iota(I32, (V_ROWS - HEAD_DIM, tk), 0) == 0, 1.0, 0.0).astype(BF16)
    for j in range(n_blocks):
        v_t = _transpose_bf16(src_ref[j * tk:(j + 1) * tk, :])
        for h in range(n_heads):
            dst_ref[j, h * V_ROWS:h * V_ROWS + HEAD_DIM, :] = v_t[h * HEAD_DIM:(h + 1) * HEAD_DIM, :]
            dst_ref[j, h * V_ROWS + HEAD_DIM:(h + 1) * V_ROWS, :] = ones_pad


def _softmax_steps(carry, scores, values_t):
    stats = []
    for (m, _), s in zip(carry, scores):
        m_new = jnp.maximum(m, jnp.max(s, axis=0, keepdims=True))
        stats.append((m_new, jnp.exp(m - m_new), jnp.exp(s - m_new).astype(BF16)))
    return tuple((m_new, alpha * acc + _dot(v_t, p))
                 for (m_new, alpha, p), (_, acc), v_t in zip(stats, carry, values_t))


def _softmax_init(tq):
    return (jnp.full((1, tq), NEG_BIG, F32), jnp.zeros((V_ROWS, tq), F32))


def _softmax_result(state):
    _, acc = state
    return acc[:HEAD_DIM, :] / acc[HEAD_DIM:HEAD_DIM + 1, :]


def _flash_blocks(n, score_fn, value_fn, carry, s_ref):
    n_heads = len(carry)

    def load(slot):
        return [s_ref[slot, h] for h in range(n_heads)]

    def store(slot, kb):
        for h, s in enumerate(score_fn(kb)):
            s_ref[slot, h] = s

    odd = n % 2
    carry = lax.cond(odd == 1, lambda c: _softmax_steps(c, score_fn(0), value_fn(0)), lambda c: c, carry)

    @pl.when(n >= 2)
    def _():
        store(0, odd)

    def body(j, c):
        kb = odd + 2 * j
        store(1, kb + 1)
        c = _softmax_steps(c, load(0), value_fn(kb))
        store(0, jnp.minimum(kb + 2, n - 1))
        return _softmax_steps(c, load(1), value_fn(kb + 1))

    return lax.fori_loop(0, n // 2, body, carry)


def _heads_to_token_rows(out_t):
    return jnp.concatenate(out_t, axis=0).T.astype(BF16)


def _fox_kernel(q_ref, k_ref, v_ref, kf_ref, qf_ref, o_ref, vt_ref, s_ref):
    tq = q_ref.shape[0]
    tk = vt_ref.shape[2]
    i = pl.program_id(1)

    @pl.when(i == 0)
    def _():
        _fill_transposed_with_ones(v_ref, vt_ref, N_FOX)

    key_pos = lax.broadcasted_iota(I32, (tk, tq), 0)
    qry_pos = lax.broadcasted_iota(I32, (tk, tq), 1)
    causal = [key_pos + d * tk <= qry_pos for d in range(2)]
    pad = jnp.zeros((LANES - FOX_FEAT_ROWS, tq), BF16)
    qx = []
    for p in range(FOX_W // LANES):
        q_t = q_ref[:, p * LANES:(p + 1) * LANES].astype(F32).T
        for hh in range(HEADS_PER_BLOCK):
            qx.append(jnp.concatenate([jnp.where(_head_row_mask(hh), q_t, 0.0).astype(BF16),
                                       qf_ref[0, p * HEADS_PER_BLOCK + hh], pad], axis=0))

    def store(slot, kb):
        k0 = pl.multiple_of(kb * tk, tk)
        for h in range(N_FOX):
            p = h // HEADS_PER_BLOCK
            kx = jnp.concatenate([k_ref[pl.ds(k0, tk), p * LANES:(p + 1) * LANES],
                                  kf_ref[pl.ds(k0, tk), p * LANES:(p + 1) * LANES]], axis=1)
            s_ref[slot, h] = _dot(kx, qx[h])

    def step(carry, slot, kb, mask=None):
        scores = [s_ref[slot, h] for h in range(N_FOX)]
        if mask is not None:
            scores = [jnp.where(mask, s, NEG_BIG) for s in scores]
        return _softmax_steps(carry, scores, [vt_ref[kb, h * V_ROWS:(h + 1) * V_ROWS, :] for h in range(N_FOX)])

    def trip(j, carry):
        kb = 2 * j
        store(1, kb + 1)
        carry = step(carry, 0, kb)
        store(0, kb + 2)
        return step(carry, 1, kb + 1)

    store(0, 0)
    carry = lax.fori_loop(0, i, trip, tuple(_softmax_init(tq) for _ in range(N_FOX)))
    kb = 2 * i
    store(1, kb + 1)
    carry = step(carry, 0, kb, causal[0])
    carry = step(carry, 1, kb + 1, causal[1])
    for p in range(FOX_W // LANES):
        o_ref[:, p * LANES:(p + 1) * LANES] = _heads_to_token_rows(
            [_softmax_result(st) for st in carry[p * HEADS_PER_BLOCK:(p + 1) * HEADS_PER_BLOCK]])


def _fox_attention(pb, k_side, q_side, batch, seq):
    tq = min(512, seq)
    tk = tq // 2
    nq = seq // tq
    m = batch * seq
    return pl.pallas_call(
        _fox_kernel,
        grid=(batch, nq),
        in_specs=[pl.BlockSpec((tq, FOX_W), lambda b, i: (b * nq + i, BLK_FOX_Q // 3)),
                  pl.BlockSpec((seq, FOX_W), lambda b, i: (b, BLK_FOX_K // 3)),
                  pl.BlockSpec((seq, FOX_W), lambda b, i: (b, BLK_FOX_V // 3)),
                  pl.BlockSpec((seq, FOX_W), lambda b, i: (b, 0)),
                  pl.BlockSpec((1, N_FOX, FOX_FEAT_ROWS, tq), lambda b, i: (b, 0, 0, i))],
        out_specs=pl.BlockSpec((tq, FOX_W), lambda b, i: (b * nq + i, 0)),
        out_shape=jax.ShapeDtypeStruct((m, FOX_W), BF16),
        scratch_shapes=[pltpu.VMEM((seq // tk, N_FOX * V_ROWS, tk), BF16), pltpu.VMEM((2, N_FOX, tk, tq), F32)],
        compiler_params=_params(2),
        name="fox_attention",
    )(pb, pb, pb, k_side, q_side)


def _sb_kernel(q_ref, k_ref, v_ref, o_ref, vt_ref):
    tq = q_ref.shape[0]
    tk = vt_ref.shape[2]
    i = pl.program_id(1)

    @pl.when(i == 0)
    def _():
        _fill_transposed(v_ref, vt_ref)

    heads = range(N_SB)
    q_t = [q_ref[:, p * LANES:(p + 1) * LANES].astype(F32).T for p in range(PAIR_W // LANES)]
    qh = [jnp.where(_head_row_mask(h % HEADS_PER_BLOCK), q_t[h // HEADS_PER_BLOCK], 0.0).astype(BF16) for h in heads]
    r = lax.broadcasted_iota(I32, (tk, tk), 0)
    c = lax.broadcasted_iota(I32, (tk, tk), 1)
    upper = jnp.where(c > r, 1.0, 0.0).astype(BF16)
    key_pos = lax.broadcasted_iota(I32, (tk, tq), 0)
    qry_pos = lax.broadcasted_iota(I32, (tk, tq), 1) + i * tq

    def process(kb, state, masked):
        k0 = pl.multiple_of(kb * tk, tk)
        strict = key_pos + k0 < qry_pos
        z = [_dot(k_ref[pl.ds(k0, tk), (h // HEADS_PER_BLOCK) * LANES:(h // HEADS_PER_BLOCK + 1) * LANES], qh[h])
             for h in heads]
        log_beta, log_1m = [], []
        for zh in z:
            l1m = jnp.minimum(-zh, 0.0) - jnp.log(1.0 + jnp.exp(-jnp.abs(zh)))
            log_beta.append(l1m + zh)
            log_1m.append(jnp.where(strict, l1m, 0.0) if masked else l1m)
        after = []
        for h in heads:
            hi, mid, lo = _split3(log_1m[h])
            after.append(_dot(upper, hi) + _dot(upper, mid) + _dot(upper, lo) + state[h][0])
        weights = []
        for h in heads:
            a = jnp.exp(log_beta[h] + after[h])
            weights.append((jnp.where(strict, a, 0.0) if masked else a).astype(BF16))
        return tuple((state[h][0] + jnp.sum(log_1m[h], axis=0, keepdims=True),
                      state[h][1] + _dot(vt_ref[kb, h * HEAD_DIM:(h + 1) * HEAD_DIM, :], weights[h]))
                     for h in heads)

    def run_max(state):
        m = state[0][0]
        for run, _ in state[1:]:
            m = jnp.maximum(m, run)
        return jnp.max(m)

    state = tuple((jnp.zeros((1, tq), F32), jnp.zeros((HEAD_DIM, tq), F32)) for _ in heads)
    n_diag = tq // tk
    for d in range(n_diag):
        state = process((i + 1) * n_diag - 1 - d, state, True)

    def cond(carry):
        kb, worst, _ = carry
        return jnp.logical_and(kb >= 0, worst > SB_CUTOFF)

    def body(carry):
        kb, _, st = carry
        st = process(kb, st, False)
        return kb - 1, run_max(st), st

    _, _, state = lax.while_loop(cond, body, (i * n_diag - 1, run_max(state), state))
    for p in range(PAIR_W // LANES):
        outs = [state[h][1] for h in heads if h // HEADS_PER_BLOCK == p]
        outs += [jnp.zeros((HEAD_DIM, tq), F32)] * (HEADS_PER_BLOCK - len(outs))
        o_ref[:, p * LANES:(p + 1) * LANES] = _heads_to_token_rows(outs)


def _sb_attention(pb, batch, seq):
    tq = min(256, seq)
    tk = min(128, seq)
    nq = seq // tq
    m = batch * seq
    return pl.pallas_call(
        _sb_kernel,
        grid=(batch, nq),
        in_specs=[pl.BlockSpec((tq, PAIR_W), lambda b, i: (b * nq + i, BLK_SB_Q // 3)),
                  pl.BlockSpec((seq, PAIR_W), lambda b, i: (b, BLK_SB_K // 3)),
                  pl.BlockSpec((seq, PAIR_W), lambda b, i: (b, BLK_SB_V // 3))],
        out_specs=pl.BlockSpec((tq, PAIR_W), lambda b, i: (b * nq + i, 0)),
        out_shape=jax.ShapeDtypeStruct((m, PAIR_W), BF16),
        scratch_shapes=[pltpu.VMEM((seq // tk, PAIR_W, tk), BF16)],
        compiler_params=_params(2),
        name="sb_attention",
    )(pb, pb, pb)


HALF_BITS = 16
HALF_OFFSET = 1 << (HALF_BITS - 1)


def _dsa_kernel(q_ref, iq_ref, w_ref, k_ref, v_ref, ka_ref, kb_ref, o_ref, key_ref, hi_ref, lo_ref, bias_ref,
                vt_ref, s_ref, *, topk, seq):
    t = q_ref.shape[0]
    i = pl.program_id(1)

    @pl.when(i == 0)
    def _():
        _fill_transposed_with_ones(v_ref, vt_ref, N_DSA)

    n_kb = i + 1
    kf = float(topk)
    imin = jnp.int32(INT_MIN)
    key_pos = lax.broadcasted_iota(I32, (t, t), 0)
    qry_pos = lax.broadcasted_iota(I32, (t, t), 1) + i * t

    iq_t = [_transpose_bf16(iq_ref[:, g * LANES:(g + 1) * LANES]) for g in range(N_IDX_HEADS // 2)]
    w_t = (w_ref[...] * (N_IDX_HEADS ** -0.5) * (IDX_DIM ** -0.5)).T

    def index_block(kb):
        k0 = pl.multiple_of(kb * t, t)
        ka = ka_ref[pl.ds(k0, t), :]
        kb_ = kb_ref[pl.ds(k0, t), :]
        score = jnp.zeros((t, t), F32)
        for g in range(N_IDX_HEADS // 2):
            lo = SMALL_IDXW_LANE + 2 * g
            score = score + jnp.maximum(_dot(ka, iq_t[g]), 0.0) * w_t[lo:lo + 1, :]
            score = score + jnp.maximum(_dot(kb_, iq_t[g]), 0.0) * w_t[lo + 1:lo + 2, :]
        bits = lax.bitcast_convert_type(score + 0.0, I32)
        key = jnp.where(bits < 0, bits ^ jnp.int32(0x7FFFFFFF), bits)
        key = jnp.where(key_pos + k0 <= qry_pos, key, imin)
        key_ref[pl.ds(k0, t), :] = key
        hi_ref[pl.ds(k0, t), :] = lax.shift_right_arithmetic(key, HALF_BITS).astype(jnp.int16)
        lo_ref[pl.ds(k0, t), :] = ((key & (2 * HALF_OFFSET - 1)) - HALF_OFFSET).astype(jnp.int16)

    odd = n_kb % 2

    @pl.when(odd == 1)
    def _():
        index_block(0)

    def index_pair(j, _):
        index_block(odd + 2 * j)
        index_block(odd + 2 * j + 1)
        return 0

    lax.fori_loop(0, n_kb // 2, index_pair, 0)

    masked16 = jnp.full((t, t), -HALF_OFFSET, jnp.int16)

    @pl.when(odd == 1)
    def _():
        k0 = pl.multiple_of(n_kb * t, t)
        hi_ref[pl.ds(k0, t), :] = masked16
        lo_ref[pl.ds(k0, t), :] = masked16

    n_scan = (n_kb + 1) // 2
    scan_rows = 2 * t

    def count16(ref, pred):
        rows = 16
        def body(trip, acc):
            k0 = pl.multiple_of(trip * scan_rows, scan_rows)
            ind = pred(ref[pl.ds(k0, scan_rows), :])
            parts = [ind[r:r + rows, :] for r in range(0, scan_rows, rows)]
            while len(parts) > 1:
                parts = [parts[j] + parts[j + 1] for j in range(0, len(parts), 2)]
            return acc + parts[0]
        acc = lax.fori_loop(0, n_scan, body, jnp.zeros((rows, t), jnp.int16))
        return jnp.sum(acc.astype(F32), axis=0, keepdims=True)

    one16, zero16 = jnp.int16(1), jnp.int16(0)

    def kth_half(ref, need, n_all):
        def bit_step(it, state):
            prefix, cnt_prefix = state
            cand_u = prefix | lax.shift_left(jnp.int32(1), HALF_BITS - 1 - it)
            cand = (cand_u - HALF_OFFSET).astype(jnp.int16)
            cnt = count16(ref, lambda blk: jnp.where(blk >= cand, one16, zero16))
            take = cnt >= need
            return jnp.where(take, cand_u, prefix), jnp.where(take, cnt, cnt_prefix)
        prefix, cnt = lax.fori_loop(0, HALF_BITS, bit_step, (jnp.zeros((1, t), I32), n_all))
        return prefix - HALF_OFFSET, cnt


    n_scanned = (n_scan * scan_rows).astype(F32)
    kth_hi, n_ge_hi = kth_half(hi_ref, kf, jnp.full((1, t), n_scanned, F32))
    kth_hi16 = kth_hi.astype(jnp.int16)

    def above_and_bucket(trip, acc):
        rows = 16
        k0 = pl.multiple_of(trip * scan_rows, scan_rows)
        hi = hi_ref[pl.ds(k0, scan_rows), :]
        lo_ref[pl.ds(k0, scan_rows), :] = jnp.where(hi == kth_hi16, lo_ref[pl.ds(k0, scan_rows), :],
                                                    jnp.int16(-HALF_OFFSET))
        ind = jnp.where(hi > kth_hi16, one16, zero16)
        parts = [ind[r:r + rows, :] for r in range(0, scan_rows, rows)]
        while len(parts) > 1:
            parts = [parts[j] + parts[j + 1] for j in range(0, len(parts), 2)]
        return acc + parts[0]

    n_gt_hi = jnp.sum(lax.fori_loop(0, n_scan, above_and_bucket, jnp.zeros((16, t), jnp.int16)).astype(F32),
                      axis=0, keepdims=True)
    need_lo = kf - n_gt_hi
    kth_lo, n_ge_lo = kth_half(lo_ref, need_lo, n_ge_hi - n_gt_hi)
    kth = lax.shift_left(kth_hi, HALF_BITS) + (kth_lo + HALF_OFFSET)

    excess = jnp.where(kth == imin, float(2 * seq), n_ge_lo - need_lo)
    any_excess = jnp.max(excess) > 0.0

    @pl.when(jnp.logical_not(any_excess))
    def _():
        def select(kb, _):
            k0 = pl.multiple_of(kb * t, t)
            bias_ref[pl.ds(k0, t), :] = jnp.where(key_ref[pl.ds(k0, t), :] >= kth, 0.0, NEG_BIG)
            return 0

        lax.fori_loop(0, n_kb, select, 0)

    @pl.when(any_excess)
    def _():
        r = lax.broadcasted_iota(I32, (t, t), 0)
        c = lax.broadcasted_iota(I32, (t, t), 1)
        later = jnp.where(c > r, 1.0, 0.0).astype(BF16)

        def select(step, tied_after):
            kb = n_kb - 1 - step
            k0 = pl.multiple_of(kb * t, t)
            kblk = key_ref[pl.ds(k0, t), :]
            tied = kblk == kth
            tied01 = jnp.where(tied, 1.0, 0.0).astype(BF16)
            after = _dot(later, tied01) + tied_after
            tie_bias = jnp.where(tied, jnp.where(after >= excess, 0.0, NEG_BIG), NEG_BIG)
            bias_ref[pl.ds(k0, t), :] = jnp.where(kblk > kth, 0.0, tie_bias)
            return after[0:1, :] + tied01[0:1, :].astype(F32)

        lax.fori_loop(0, n_kb, select, jnp.zeros((1, t), F32))

    q_t = [q_ref[:, p * LANES:(p + 1) * LANES].astype(F32).T for p in range(PAIR_W // LANES)]
    qh = [jnp.where(_head_row_mask(h % HEADS_PER_BLOCK), q_t[h // HEADS_PER_BLOCK], 0.0).astype(BF16)
          for h in range(N_DSA)]

    def scores(kb):
        k0 = pl.multiple_of(kb * t, t)
        bias = bias_ref[pl.ds(k0, t), :]
        return [_dot(k_ref[pl.ds(k0, t), (h // HEADS_PER_BLOCK) * LANES:(h // HEADS_PER_BLOCK + 1) * LANES], qh[h])
                + bias for h in range(N_DSA)]

    def values_t(kb):
        return [vt_ref[kb, h * V_ROWS:(h + 1) * V_ROWS, :] for h in range(N_DSA)]

    carry = _flash_blocks(n_kb, scores, values_t, tuple(_softmax_init(t) for _ in range(N_DSA)), s_ref)
    outs = [_softmax_result(st) for st in carry]
    outs += [jnp.zeros((HEAD_DIM, t), F32)] * (PAIR_W // HEAD_DIM - N_DSA)
    for p in range(PAIR_W // LANES):
        o_ref[:, p * LANES:(p + 1) * LANES] = _heads_to_token_rows(
            outs[p * HEADS_PER_BLOCK:(p + 1) * HEADS_PER_BLOCK])


def _dsa_attention(pb, ps, batch, seq):
    t = min(256, seq)
    nq = seq // t
    m = batch * seq
    topk = min(TOPK_MAX, seq // 4)
    return pl.pallas_call(
        functools.partial(_dsa_kernel, topk=topk, seq=seq),
        grid=(batch, nq),
        in_specs=[pl.BlockSpec((t, PAIR_W), lambda b, i: (b * nq + i, BLK_DSA_Q // 3)),
                  pl.BlockSpec((t, 4 * LANES), lambda b, i: (b * nq + i, BLK_IDX_Q // 4)),
                  pl.BlockSpec((t, LANES), lambda b, i: (b * nq + i, 0)),
                  pl.BlockSpec((seq, PAIR_W), lambda b, i: (b, BLK_DSA_K // 3)),
                  pl.BlockSpec((seq, PAIR_W), lambda b, i: (b, BLK_DSA_V // 3)),
                  pl.BlockSpec((seq, LANES), lambda b, i: (b, BLK_IDX_KA)),
                  pl.BlockSpec((seq, LANES), lambda b, i: (b, BLK_IDX_KB))],
        out_specs=pl.BlockSpec((t, PAIR_W), lambda b, i: (b * nq + i, 0)),
        out_shape=jax.ShapeDtypeStruct((m, PAIR_W), BF16),
        scratch_shapes=[pltpu.VMEM((seq, t), I32), pltpu.VMEM((seq + t, t), jnp.int16),
                        pltpu.VMEM((seq + t, t), jnp.int16),
                        pltpu.VMEM((seq, t), F32),
                        pltpu.VMEM((nq, N_DSA * V_ROWS, t), BF16), pltpu.VMEM((2, N_DSA, t, t), F32)],
        compiler_params=_params(2),
        name="dsa_attention",
    )(pb, pb, ps, pb, pb, pb, pb)


def _mixer_out_cross_kernel(x_ref, gain_ref, of_ref, od_ref, os_ref, wg_ref, wf_ref, wd_ref, ws_ref, wo_ref,
                            ca_gain_ref, ca_wq_ref, kv_ref, ca_wo_ref, o_ref):
    d = x_ref.shape[1]
    x = x_ref[...]
    h = _rms(x, gain_ref[...]).astype(BF16)
    merged = None
    for b, (o_b, w_b) in enumerate(((of_ref, wf_ref), (od_ref, wd_ref), (os_ref, ws_ref))):
        term = jax.nn.sigmoid(_dot(h, wg_ref[:, b * d:(b + 1) * d])) * _dot(o_b[...], w_b[...])
        merged = term if merged is None else merged + term
    x = x + _dot(merged.astype(BF16), wo_ref[...])

    q = _dot(_rms(x, ca_gain_ref[...]).astype(BF16), ca_wq_ref[...])
    scale = CA_HEAD_DIM ** -0.5
    outs = []
    for hd in range(N_CA_HEADS):
        qh = q[:, hd * CA_HEAD_DIM:(hd + 1) * CA_HEAD_DIM].astype(BF16)
        logits = _dot_nt(qh, kv_ref[:, hd * CA_HEAD_DIM:(hd + 1) * CA_HEAD_DIM]) * scale
        e = jnp.exp(logits - jnp.max(logits, axis=1, keepdims=True))
        pv = _dot(e.astype(BF16), kv_ref[:, CA_W + hd * CA_HEAD_DIM:CA_W + (hd + 1) * CA_HEAD_DIM])
        outs.append(pv / jnp.sum(e, axis=1, keepdims=True))
    o_ref[...] = x + _dot(jnp.concatenate(outs, axis=1).astype(BF16), ca_wo_ref[...])


def _mixer_out_cross(x2, gain, o_fox, o_dsa, o_sb, wg, wf, wd, ws, wo, ca_gain, ca_wq, kv, ca_wo, batch, seq):
    m, d = x2.shape
    tm = min(512, seq)
    n_mem = kv.shape[0] // batch
    per_batch = seq // tm
    row = lambda w: pl.BlockSpec((tm, w), lambda i: (i, 0))
    return pl.pallas_call(
        _mixer_out_cross_kernel,
        grid=(m // tm,),
        in_specs=[row(d), _resident((1, d)), row(FOX_W), row(PAIR_W), row(PAIR_W), _resident((d, N_BRANCH * d)),
                  _resident((FOX_W, d)), _resident((PAIR_W, d)), _resident((PAIR_W, d)), _resident((d, d)),
                  _resident((1, d)), _resident((d, CA_W)),
                  pl.BlockSpec((n_mem, 2 * CA_W), lambda i: (i // per_batch, 0)), _resident((CA_W, d))],
        out_specs=row(d),
        out_shape=jax.ShapeDtypeStruct((m, d), F32),
        compiler_params=_params(1),
        name="mixer_out_cross",
    )(x2, gain.reshape(1, d), o_fox, o_dsa, o_sb, wg, wf, wd, ws, wo, ca_gain.reshape(1, d), ca_wq, kv, ca_wo)


def _norm_matmul_kernel(x_ref, g_ref, w_ref, o_ref):
    o_ref[...] = _dot(_rms(x_ref[...], g_ref[...]).astype(BF16), w_ref[...]).astype(o_ref.dtype)


def _norm_matmul(x2, gain, w, out_dtype):
    m, d = x2.shape
    n = w.shape[1]
    tm = min(512, m)
    return pl.pallas_call(
        _norm_matmul_kernel,
        grid=(m // tm,),
        in_specs=[pl.BlockSpec((tm, d), lambda i: (i, 0)), _resident((1, d)), _resident((d, n))],
        out_specs=pl.BlockSpec((tm, n), lambda i: (i, 0)),
        out_shape=jax.ShapeDtypeStruct((m, n), out_dtype),
        compiler_params=_params(1),
        name="norm_matmul",
    )(x2, gain.reshape(1, d), w)


def _pad_cols(w, n):
    return jnp.pad(w, ((0, 0), (0, n - w.shape[1])))


def _pad_rows(w, n):
    return jnp.pad(w, ((0, n - w.shape[0]), (0, 0)))


def _mixer_weights(w_in):
    offs = np.cumsum((0,) + IN_SPLITS)
    part = lambda j: w_in[:, offs[j]:offs[j + 1]]
    thirds = lambda w: jnp.split(w, 3, axis=1)
    scale = HEAD_DIM ** -0.5
    fq, fk, fv = thirds(part(0))
    dq, dk, dv = thirds(part(2))
    sq, sk, sv = thirds(part(6))
    iq, ik, iw = part(3), part(4), part(5)
    zk = jnp.zeros_like(ik)
    small = jnp.concatenate([part(1), jnp.zeros((w_in.shape[0], SMALL_IDXW_LANE - N_FOX), w_in.dtype), iw], axis=1)
    wb = jnp.concatenate(
        [_pad_cols(dq * scale, PAIR_W), _pad_cols(dk, PAIR_W),
         jnp.concatenate([ik, zk], axis=1), jnp.concatenate([zk, ik], axis=1), iq,
         fq * scale, fk, fv, _pad_cols(dv, PAIR_W),
         _pad_cols(sq * scale, PAIR_W), _pad_cols(sk, PAIR_W), _pad_cols(sv, PAIR_W),
         _pad_cols(small, LANES)], axis=1).astype(BF16)
    return wb, part(7).astype(BF16)


def _mixer_and_cross(x2, cos, sin_signed, batch, seq, gain, w_in, b_fgate, w_fox_out, w_dsa_out, w_sb_out, w_out,
                     ca_gain, ca_wq, kv, ca_wo):
    wb, wg = _mixer_weights(w_in)
    bias_row = _pad_cols(b_fgate.reshape(1, N_FOX).astype(F32), LANES)
    pb, ps, k_side, q_side = _mixer_in(x2, gain, cos, sin_signed, wb, bias_row, batch, seq)

    o_fox = _fox_attention(pb, k_side, q_side, batch, seq)
    o_dsa = _dsa_attention(pb, ps, batch, seq)
    o_sb = _sb_attention(pb, batch, seq)
    return _mixer_out_cross(x2, gain, o_fox, o_dsa, o_sb, wg, w_fox_out.astype(BF16),
                            _pad_rows(w_dsa_out, PAIR_W).astype(BF16), _pad_rows(w_sb_out, PAIR_W).astype(BF16),
                            w_out.astype(BF16), ca_gain, ca_wq.astype(BF16), kv, ca_wo.astype(BF16), batch, seq)


def kernel(x, mem, positions, ffn1_norm, ffn1_w_gu, ffn1_w_down, mix_norm, w_in, b_fgate, w_fox_out, w_dsa_out,
           w_sb_out, w_out, ca_norm, mem_norm, ca_w_q, ca_w_kv, ca_w_o, ffn2_norm, ffn2_w_gu, ffn2_w_down,
           final_norm):
    batch, seq, d = x.shape
    depth = ffn1_norm.shape[0]
    x2 = x.reshape(batch * seq, d)
    mem2 = mem.reshape(-1, d)
    pos2 = positions.reshape(batch * seq, 1).astype(I32)
    half = HEAD_DIM // 2
    inv_freq = jnp.power(ROPE_THETA, -jnp.arange(half, dtype=F32) * (2.0 / HEAD_DIM))
    invf = jnp.tile(inv_freq, LANES // half).reshape(1, LANES)
    cos, sin_signed = _rope_tables(pos2, invf)
    for l in range(depth):
        x2 = _ffn(x2, ffn1_norm[l], ffn1_w_gu[l], ffn1_w_down[l])
        kv = _norm_matmul(mem2, mem_norm[l], ca_w_kv[l].astype(BF16), BF16)
        x2 = _mixer_and_cross(x2, cos, sin_signed, batch, seq, mix_norm[l], w_in[l], b_fgate[l], w_fox_out[l],
                              w_dsa_out[l], w_sb_out[l], w_out[l], ca_norm[l], ca_w_q[l], kv, ca_w_o[l])
        x2 = _ffn(x2, ffn2_norm[l], ffn2_w_gu[l], ffn2_w_down[l],
                  final_gain=final_norm if l == depth - 1 else None)
    return x2.reshape(batch, seq, d)
```

```python
import functools

import jax
import jax.numpy as jnp
import numpy as np
from jax import lax
from jax.experimental import pallas as pl
from jax.experimental.pallas import tpu as pltpu

F32 = jnp.float32
BF16 = jnp.bfloat16
I32 = jnp.int32

D_MODEL = 1024
HEAD_DIM = 64
N_FOX = 6
N_DSA = 5
N_SB = 5
N_IDX_HEADS = 8
IDX_DIM = 64
TOPK_MAX = 256
N_CA_HEADS = 4
CA_HEAD_DIM = 128
D_FF = 2816
ROPE_THETA = 10000.0
NORM_EPS = 1e-6
N_BRANCH = 3
HALF_STEP = 0.5
FOX_W = N_FOX * HEAD_DIM
DSA_W = N_DSA * HEAD_DIM
SB_W = N_SB * HEAD_DIM
CA_W = N_CA_HEADS * CA_HEAD_DIM
IN_SPLITS = (3 * FOX_W, N_FOX, 3 * DSA_W, N_IDX_HEADS * IDX_DIM, IDX_DIM, N_IDX_HEADS, 3 * SB_W,
             N_BRANCH * D_MODEL)

LANES = 128
HEADS_PER_BLOCK = LANES // HEAD_DIM
PAIR_W = 3 * LANES
VMEM_LIMIT = 56 * 2**20

BLK_DSA_Q, BLK_DSA_K = 0, 3
BLK_IDX_KA = 6
BLK_IDX_KB = 7
BLK_IDX_Q = 8
N_ROPE_BLOCKS = 12
BLK_FOX_Q, BLK_FOX_K, BLK_FOX_V = 12, 15, 18
BLK_DSA_V = 21
BLK_SB_Q, BLK_SB_K, BLK_SB_V = 24, 27, 30
N_PROJ_BLOCKS = 33
PROJ_CHUNK = 4 * LANES
SMALL_FGATE_LANE = 0
SMALL_IDXW_LANE = 8

NEG_BIG = -1e30
SB_CUTOFF = -110.0
INT_MIN = -2**31


def _params(n_grid):
    return pltpu.CompilerParams(dimension_semantics=("arbitrary",) * n_grid, vmem_limit_bytes=VMEM_LIMIT)


def _resident(shape):
    nd = len(shape)
    return pl.BlockSpec(shape, lambda *_: (0,) * nd, pipeline_mode=pl.Buffered(1))


def _rms(x, g):
    return x * lax.rsqrt(jnp.mean(x * x, axis=-1, keepdims=True) + NORM_EPS) * g


def _dot(a, b):
    return jnp.dot(a, b, preferred_element_type=F32)


def _dot_nt(a, b):
    return lax.dot_general(a, b, (((1,), (1,)), ((), ())), preferred_element_type=F32)


def _split3(x):
    hi = x.astype(BF16)
    r = x - hi.astype(F32)
    mid = r.astype(BF16)
    lo = (r - mid.astype(F32)).astype(BF16)
    return hi, mid, lo


def _log1p_exp_neg_abs(z):
    return jnp.log1p(jnp.exp(-jnp.abs(z)))


def _ffn_chunks():
    out, c = [], 0
    while c < D_FF:
        w = min(512, D_FF - c)
        out.append((c, w))
        c += w
    return tuple(out)


def _ffn_kernel(*refs, final):
    if final:
        x_ref, g_ref, wgu_ref, wd_ref, fn_ref, o_ref, a_ref = refs
    else:
        x_ref, g_ref, wgu_ref, wd_ref, o_ref, a_ref = refs
    x = x_ref[...]
    h = _rms(x, g_ref[...]).astype(BF16)
    for c0, w in _ffn_chunks():
        g = _dot(h, wgu_ref[:, c0:c0 + w])
        u = _dot(h, wgu_ref[:, D_FF + c0:D_FF + c0 + w])
        a_ref[:, c0:c0 + w] = (g * jax.nn.sigmoid(g) * u).astype(BF16)
    y = x + HALF_STEP * _dot(a_ref[...], wd_ref[...])
    if final:
        y = _rms(y, fn_ref[...])
    o_ref[...] = y


def _ffn(x2, gain, w_gu, w_down, final_gain=None):
    m, d = x2.shape
    tm = min(512, m)
    final = final_gain is not None
    in_specs = [pl.BlockSpec((tm, d), lambda i: (i, 0)), _resident((1, d)),
                _resident((d, 2 * D_FF)), _resident((D_FF, d))]
    args = [x2, gain.reshape(1, d), w_gu.astype(BF16), w_down.astype(BF16)]
    if final:
        in_specs.append(_resident((1, d)))
        args.append(final_gain.reshape(1, d))
    return pl.pallas_call(
        functools.partial(_ffn_kernel, final=final),
        grid=(m // tm,),
        in_specs=in_specs,
        out_specs=pl.BlockSpec((tm, d), lambda i: (i, 0)),
        out_shape=jax.ShapeDtypeStruct((m, d), F32),
        scratch_shapes=[pltpu.VMEM((tm, D_FF), BF16)],
        compiler_params=_params(1),
        name="ffn",
    )(*args)


def _first_half_lanes():
    lane = lax.broadcasted_iota(I32, (1, LANES), 1)
    return (lane % HEAD_DIM) < (HEAD_DIM // 2)


def _rope_tables_kernel(pos_ref, invf_ref, cos_ref, sin_ref):
    ang = pos_ref[...].astype(F32) * invf_ref[...]
    sin = jnp.sin(ang)
    cos_ref[...] = jnp.cos(ang)
    sin_ref[...] = jnp.where(_first_half_lanes(), -sin, sin)


def _rope_tables(pos2, invf):
    m = pos2.shape[0]
    tm = min(512, m)
    tok = pl.BlockSpec((tm, LANES), lambda i: (i, 0))
    return pl.pallas_call(
        _rope_tables_kernel,
        grid=(m // tm,),
        in_specs=[pl.BlockSpec((tm, 1), lambda i: (i, 0)), _resident((1, LANES))],
        out_specs=[tok, tok],
        out_shape=[jax.ShapeDtypeStruct((m, LANES), F32)] * 2,
        compiler_params=_params(1),
        name="rope_tables",
    )(pos2, invf)


FOX_FEAT_ROWS = 16


def _fox_feature_maps():
    k_map = np.zeros((3, LANES, FOX_W), np.float32)
    q_map = np.zeros((3, LANES, LANES), np.float32)
    k_one = np.zeros((1, FOX_W), np.float32)
    q_one = np.zeros((1, LANES), np.float32)
    for h in range(N_FOX):
        p, hh = divmod(h, HEADS_PER_BLOCK)
        for j in range(3):
            k_map[j, h, p * LANES + 3 * hh + j] = -1.0
            q_map[j, h, FOX_FEAT_ROWS * h + 6 + j] = 1.0
            k_one[0, p * LANES + 6 + j] = 1.0
            q_one[0, FOX_FEAT_ROWS * h + 3 * hh + j] = 1.0
    return (jnp.asarray(k_map, BF16), jnp.asarray(q_map, BF16), jnp.asarray(k_one), jnp.asarray(q_one))


def _fox_decay_features(logits, b_ref, kmap_ref, qmap_ref, kone_ref, qone_ref, kf_ref, qf_ref, carry_ref):
    tc = logits.shape[0]
    z = logits + b_ref[...]
    log_f = jnp.minimum(z, 0.0) - _log1p_exp_neg_abs(z)
    row = lax.broadcasted_iota(I32, (tc, tc), 0)
    col = lax.broadcasted_iota(I32, (tc, tc), 1)
    tri = jnp.where(col <= row, 1.0, 0.0).astype(BF16)
    hi, mid, lo = _split3(log_f)
    yield
    cs = _dot(tri, hi) + _dot(tri, mid) + _dot(tri, lo) + carry_ref[...]
    carry_ref[...] = cs[tc - 1:tc, :]
    terms = _split3(cs)
    yield
    kf = kone_ref[...]
    qf = qone_ref[...]
    for j in range(3):
        kf = kf + _dot(terms[j], kmap_ref[j])
        qf = qf + _dot(terms[j], qmap_ref[j])
    kf_ref[...] = kf.astype(BF16)
    qf_ref[...] = qf.astype(BF16)


def _mixer_in_kernel(x_ref, g_ref, cos_ref, sin_ref, wb_ref, b_ref, kmap_ref, qmap_ref, kone_ref, qone_ref,
                     pb_ref, ps_ref, kf_ref, qf_ref, carry_ref, *, tiles_per_seq):
    @pl.when(pl.program_id(0) % tiles_per_seq == 0)
    def _():
        carry_ref[...] = jnp.zeros_like(carry_ref)

    h = _rms(x_ref[...], g_ref[...]).astype(BF16)
    cos = cos_ref[...]
    sin_signed = sin_ref[...]
    first_half = _first_half_lanes()
    n_cols = (N_PROJ_BLOCKS + 1) * LANES
    starts = list(range(0, n_cols, PROJ_CHUNK))
    decay_stages = iter(())
    for c0 in starts[-1:] + starts[:-1]:
        c1 = min(c0 + PROJ_CHUNK, n_cols)
        res = _dot(h, wb_ref[:, c0:c1])
        next(decay_stages, None)
        if c1 == n_cols:
            small = res[:, c1 - c0 - LANES:]
            ps_ref[...] = small
            decay_stages = _fox_decay_features(small, b_ref, kmap_ref, qmap_ref, kone_ref, qone_ref, kf_ref,
                                               qf_ref, carry_ref)
            next(decay_stages)
            c1 -= LANES
            res = res[:, :c1 - c0]
        if c0 < N_ROPE_BLOCKS * LANES:
            parts = []
            for b in range((c1 - c0) // LANES):
                xb = res[:, b * LANES:(b + 1) * LANES]
                partner = jnp.where(first_half, pltpu.roll(xb, LANES - HEAD_DIM // 2, 1),
                                    pltpu.roll(xb, HEAD_DIM // 2, 1))
                parts.append(xb * cos + partner * sin_signed)
            res = jnp.concatenate(parts, axis=1)
        pb_ref[:, c0:c1] = res.astype(BF16)


def _mixer_in(x2, gain, cos, sin_signed, wb, bias_row, batch, seq):
    m, d = x2.shape
    tm = min(512, seq)
    nb = N_PROJ_BLOCKS * LANES
    tok = lambda w: pl.BlockSpec((tm, w), lambda i: (i, 0))
    pb, ps, k_side, q_tok = pl.pallas_call(
        functools.partial(_mixer_in_kernel, tiles_per_seq=seq // tm),
        grid=(m // tm,),
        in_specs=[tok(d), _resident((1, d)), tok(LANES), tok(LANES), _resident((d, nb + LANES)),
                  _resident((1, LANES)), _resident((3, LANES, FOX_W)), _resident((3, LANES, LANES)),
                  _resident((1, FOX_W)), _resident((1, LANES))],
        out_specs=[tok(nb), tok(LANES), tok(FOX_W), tok(LANES)],
        out_shape=[jax.ShapeDtypeStruct((m, nb), BF16), jax.ShapeDtypeStruct((m, LANES), F32),
                   jax.ShapeDtypeStruct((m, FOX_W), BF16), jax.ShapeDtypeStruct((m, LANES), BF16)],
        scratch_shapes=[pltpu.VMEM((1, LANES), F32)],
        compiler_params=_params(1),
        name="mixer_in",
    )(x2, gain.reshape(1, d), cos, sin_signed, wb, bias_row, *_fox_feature_maps())
    q_side = q_tok[:, :N_FOX * FOX_FEAT_ROWS].reshape(batch, seq, N_FOX, FOX_FEAT_ROWS).transpose(0, 2, 3, 1)
    return pb, ps, k_side, q_side


def _head_lane_mask(hh):
    lane = lax.broadcasted_iota(I32, (1, LANES), 1)
    return (lane < HEAD_DIM) if hh == 0 else (lane >= HEAD_DIM)


def _head_row_mask(hh):
    row = lax.broadcasted_iota(I32, (LANES, 1), 0)
    return (row < HEAD_DIM) if hh == 0 else (row >= HEAD_DIM)


def _transpose_bf16(x):
    return x.astype(F32).T.astype(BF16)


def _fill_transposed(src_ref, dst_ref):
    n_blocks, _, tk = dst_ref.shape
    for j in range(n_blocks):
        dst_ref[j] = _transpose_bf16(src_ref[j * tk:(j + 1) * tk, :])


V_ROWS = HEAD_DIM + 16


def _fill_transposed_with_ones(src_ref, dst_ref, n_heads):
    n_blocks, _, tk = dst_ref.shape
    ones_pad = jnp.where(lax.broadcasted_iota(I32, (V_ROWS - HEAD_DIM, tk), 0) == 0, 1.0, 0.0).astype(BF16)
    for j in range(n_blocks):
        v_t = _transpose_bf16(src_ref[j * tk:(j + 1) * tk, :])
        for h in range(n_heads):
            dst_ref[j, h * V_ROWS:h * V_ROWS + HEAD_DIM, :] = v_t[h * HEAD_DIM:(h + 1) * HEAD_DIM, :]
            dst_ref[j, h * V_ROWS + HEAD_DIM:(h + 1) * V_ROWS, :] = ones_pad


def _softmax_steps(carry, scores, values_t):
    stats = []
    for (m, _), s in zip(carry, scores):
        m_new = jnp.maximum(m, jnp.max(s, axis=0, keepdims=True))
        stats.append((m_new, jnp.exp(m - m_new), jnp.exp(s - m_new).astype(BF16)))
    return tuple((m_new, alpha * acc + _dot(v_t, p))
                 for (m_new, alpha, p), (_, acc), v_t in zip(stats, carry, values_t))


def _softmax_init(tq):
    return (jnp.full((1, tq), NEG_BIG, F32), jnp.zeros((V_ROWS, tq), F32))


def _softmax_result(state):
    _, acc = state
    return acc[:HEAD_DIM, :] / acc[HEAD_DIM:HEAD_DIM + 1, :]


def _flash_blocks(n, score_fn, value_fn, carry, s_ref):
    n_heads = len(carry)

    def load(slot):
        return [s_ref[slot, h] for h in range(n_heads)]

    def store(slot, kb):
        for h, s in enumerate(score_fn(kb)):
            s_ref[slot, h] = s

    odd = n % 2
    carry = lax.cond(odd == 1, lambda c: _softmax_steps(c, score_fn(0), value_fn(0)), lambda c: c, carry)

    @pl.when(n >= 2)
    def _():
        store(0, odd)

    def body(j, c):
        kb = odd + 2 * j
        store(1, kb + 1)
        c = _softmax_steps(c, load(0), value_fn(kb))
        store(0, jnp.minimum(kb + 2, n - 1))
        return _softmax_steps(c, load(1), value_fn(kb + 1))

    return lax.fori_loop(0, n // 2, body, carry)


def _heads_to_token_rows(out_t):
    return jnp.concatenate(out_t, axis=0).T.astype(BF16)


def _fox_kernel(q_ref, k_ref, v_ref, kf_ref, qf_ref, o_ref, vt_ref, s_ref):
    tq = q_ref.shape[0]
    tk = vt_ref.shape[2]
    i = pl.program_id(1)

    @pl.when(i == 0)
    def _():
        _fill_transposed_with_ones(v_ref, vt_ref, N_FOX)

    key_pos = lax.broadcasted_iota(I32, (tk, tq), 0)
    qry_pos = lax.broadcasted_iota(I32, (tk, tq), 1)
    causal = [key_pos + d * tk <= qry_pos for d in range(2)]
    pad = jnp.zeros((LANES - FOX_FEAT_ROWS, tq), BF16)
    qx = []
    for p in range(FOX_W // LANES):
        q_t = q_ref[:, p * LANES:(p + 1) * LANES].astype(F32).T
        for hh in range(HEADS_PER_BLOCK):
            qx.append(jnp.concatenate([jnp.where(_head_row_mask(hh), q_t, 0.0).astype(BF16),
                                       qf_ref[0, p * HEADS_PER_BLOCK + hh], pad], axis=0))

    def store(slot, kb):
        k0 = pl.multiple_of(kb * tk, tk)
        for h in range(N_FOX):
            p = h // HEADS_PER_BLOCK
            kx = jnp.concatenate([k_ref[pl.ds(k0, tk), p * LANES:(p + 1) * LANES],
                                  kf_ref[pl.ds(k0, tk), p * LANES:(p + 1) * LANES]], axis=1)
            s_ref[slot, h] = _dot(kx, qx[h])

    def step(carry, slot, kb, mask=None):
        scores = [s_ref[slot, h] for h in range(N_FOX)]
        if mask is not None:
            scores = [jnp.where(mask, s, NEG_BIG) for s in scores]
        return _softmax_steps(carry, scores, [vt_ref[kb, h * V_ROWS:(h + 1) * V_ROWS, :] for h in range(N_FOX)])

    def trip(j, carry):
        kb = 2 * j
        store(1, kb + 1)
        carry = step(carry, 0, kb)
        store(0, kb + 2)
        return step(carry, 1, kb + 1)

    store(0, 0)
    carry = lax.fori_loop(0, i, trip, tuple(_softmax_init(tq) for _ in range(N_FOX)))
    kb = 2 * i
    store(1, kb + 1)
    carry = step(carry, 0, kb, causal[0])
    carry = step(carry, 1, kb + 1, causal[1])
    for p in range(FOX_W // LANES):
        o_ref[:, p * LANES:(p + 1) * LANES] = _heads_to_token_rows(
            [_softmax_result(st) for st in carry[p * HEADS_PER_BLOCK:(p + 1) * HEADS_PER_BLOCK]])


def _fox_attention(pb, k_side, q_side, batch, seq):
    tq = min(512, seq)
    tk = tq // 2
    nq = seq // tq
    m = batch * seq
    return pl.pallas_call(
        _fox_kernel,
        grid=(batch, nq),
        in_specs=[pl.BlockSpec((tq, FOX_W), lambda b, i: (b * nq + i, BLK_FOX_Q // 3)),
                  pl.BlockSpec((seq, FOX_W), lambda b, i: (b, BLK_FOX_K // 3)),
                  pl.BlockSpec((seq, FOX_W), lambda b, i: (b, BLK_FOX_V // 3)),
                  pl.BlockSpec((seq, FOX_W), lambda b, i: (b, 0)),
                  pl.BlockSpec((1, N_FOX, FOX_FEAT_ROWS, tq), lambda b, i: (b, 0, 0, i))],
        out_specs=pl.BlockSpec((tq, FOX_W), lambda b, i: (b * nq + i, 0)),
        out_shape=jax.ShapeDtypeStruct((m, FOX_W), BF16),
        scratch_shapes=[pltpu.VMEM((seq // tk, N_FOX * V_ROWS, tk), BF16), pltpu.VMEM((2, N_FOX, tk, tq), F32)],
        compiler_params=_params(2),
        name="fox_attention",
    )(pb, pb, pb, k_side, q_side)


def _sb_kernel(q_ref, k_ref, v_ref, o_ref, vt_ref):
    tq = q_ref.shape[0]
    tk = vt_ref.shape[2]
    i = pl.program_id(1)

    @pl.when(i == 0)
    def _():
        _fill_transposed(v_ref, vt_ref)

    heads = range(N_SB)
    q_t = [q_ref[:, p * LANES:(p + 1) * LANES].astype(F32).T for p in range(PAIR_W // LANES)]
    qh = [jnp.where(_head_row_mask(h % HEADS_PER_BLOCK), q_t[h // HEADS_PER_BLOCK], 0.0).astype(BF16) for h in heads]
    r = lax.broadcasted_iota(I32, (tk, tk), 0)
    c = lax.broadcasted_iota(I32, (tk, tk), 1)
    upper = jnp.where(c > r, 1.0, 0.0).astype(BF16)
    key_pos = lax.broadcasted_iota(I32, (tk, tq), 0)
    qry_pos = lax.broadcasted_iota(I32, (tk, tq), 1) + i * tq

    def process(kb, state, masked):
        k0 = pl.multiple_of(kb * tk, tk)
        strict = key_pos + k0 < qry_pos
        z = [_dot(k_ref[pl.ds(k0, tk), (h // HEADS_PER_BLOCK) * LANES:(h // HEADS_PER_BLOCK + 1) * LANES], qh[h])
             for h in heads]
        log_beta, log_1m = [], []
        for zh in z:
            l1m = jnp.minimum(-zh, 0.0) - jnp.log(1.0 + jnp.exp(-jnp.abs(zh)))
            log_beta.append(l1m + zh)
            log_1m.append(jnp.where(strict, l1m, 0.0) if masked else l1m)
        after = []
        for h in heads:
            hi, mid, lo = _split3(log_1m[h])
            after.append(_dot(upper, hi) + _dot(upper, mid) + _dot(upper, lo) + state[h][0])
        weights = []
        for h in heads:
            a = jnp.exp(log_beta[h] + after[h])
            weights.append((jnp.where(strict, a, 0.0) if masked else a).astype(BF16))
        return tuple((state[h][0] + jnp.sum(log_1m[h], axis=0, keepdims=True),
                      state[h][1] + _dot(vt_ref[kb, h * HEAD_DIM:(h + 1) * HEAD_DIM, :], weights[h]))
                     for h in heads)

    def run_max(state):
        m = state[0][0]
        for run, _ in state[1:]:
            m = jnp.maximum(m, run)
        return jnp.max(m)

    state = tuple((jnp.zeros((1, tq), F32), jnp.zeros((HEAD_DIM, tq), F32)) for _ in heads)
    n_diag = tq // tk
    for d in range(n_diag):
        state = process((i + 1) * n_diag - 1 - d, state, True)

    def cond(carry):
        kb, worst, _ = carry
        return jnp.logical_and(kb >= 0, worst > SB_CUTOFF)

    def body(carry):
        kb, _, st = carry
        st = process(kb, st, False)
        return kb - 1, run_max(st), st

    _, _, state = lax.while_loop(cond, body, (i * n_diag - 1, run_max(state), state))
    for p in range(PAIR_W // LANES):
        outs = [state[h][1] for h in heads if h // HEADS_PER_BLOCK == p]
        outs += [jnp.zeros((HEAD_DIM, tq), F32)] * (HEADS_PER_BLOCK - len(outs))
        o_ref[:, p * LANES:(p + 1) * LANES] = _heads_to_token_rows(outs)


def _sb_attention(pb, batch, seq):
    tq = min(256, seq)
    tk = min(128, seq)
    nq = seq // tq
    m = batch * seq
    return pl.pallas_call(
        _sb_kernel,
        grid=(batch, nq),
        in_specs=[pl.BlockSpec((tq, PAIR_W), lambda b, i: (b * nq + i, BLK_SB_Q // 3)),
                  pl.BlockSpec((seq, PAIR_W), lambda b, i: (b, BLK_SB_K // 3)),
                  pl.BlockSpec((seq, PAIR_W), lambda b, i: (b, BLK_SB_V // 3))],
        out_specs=pl.BlockSpec((tq, PAIR_W), lambda b, i: (b * nq + i, 0)),
        out_shape=jax.ShapeDtypeStruct((m, PAIR_W), BF16),
        scratch_shapes=[pltpu.VMEM((seq // tk, PAIR_W, tk), BF16)],
        compiler_params=_params(2),
        name="sb_attention",
    )(pb, pb, pb)


HALF_BITS = 16
HALF_OFFSET = 1 << (HALF_BITS - 1)
WINDOW_BITS = 9


def _dsa_kernel(q_ref, iq_ref, w_ref, k_ref, v_ref, ka_ref, kb_ref, o_ref, key_ref, hi_ref, lo_ref, bias_ref,
                vt_ref, s_ref, *, topk, seq):
    t = q_ref.shape[0]
    i = pl.program_id(1)

    @pl.when(i == 0)
    def _():
        _fill_transposed_with_ones(v_ref, vt_ref, N_DSA)

    n_kb = i + 1
    kf = float(topk)
    imin = jnp.int32(INT_MIN)
    key_pos = lax.broadcasted_iota(I32, (t, t), 0)
    qry_pos = lax.broadcasted_iota(I32, (t, t), 1) + i * t

    iq_t = [_transpose_bf16(iq_ref[:, g * LANES:(g + 1) * LANES]) for g in range(N_IDX_HEADS // 2)]
    w_t = (w_ref[...] * (N_IDX_HEADS ** -0.5) * (IDX_DIM ** -0.5)).T

    def index_block(kb):
        k0 = pl.multiple_of(kb * t, t)
        ka = ka_ref[pl.ds(k0, t), :]
        kb_ = kb_ref[pl.ds(k0, t), :]
        score = jnp.zeros((t, t), F32)
        for g in range(N_IDX_HEADS // 2):
            lo = SMALL_IDXW_LANE + 2 * g
            score = score + jnp.maximum(_dot(ka, iq_t[g]), 0.0) * w_t[lo:lo + 1, :]
            score = score + jnp.maximum(_dot(kb_, iq_t[g]), 0.0) * w_t[lo + 1:lo + 2, :]
        bits = lax.bitcast_convert_type(score + 0.0, I32)
        key = jnp.where(bits < 0, bits ^ jnp.int32(0x7FFFFFFF), bits)
        key = jnp.where(key_pos + k0 <= qry_pos, key, imin)
        key_ref[pl.ds(k0, t), :] = key
        hi_ref[pl.ds(k0, t), :] = lax.shift_right_arithmetic(key, HALF_BITS).astype(jnp.int16)
        lo_ref[pl.ds(k0, t), :] = ((key & (2 * HALF_OFFSET - 1)) - HALF_OFFSET).astype(jnp.int16)

    odd = n_kb % 2

    @pl.when(odd == 1)
    def _():
        index_block(0)

    def index_pair(j, _):
        index_block(odd + 2 * j)
        index_block(odd + 2 * j + 1)
        return 0

    lax.fori_loop(0, n_kb // 2, index_pair, 0)

    masked16 = jnp.full((t, t), -HALF_OFFSET, jnp.int16)

    @pl.when(odd == 1)
    def _():
        k0 = pl.multiple_of(n_kb * t, t)
        hi_ref[pl.ds(k0, t), :] = masked16
        lo_ref[pl.ds(k0, t), :] = masked16

    n_scan = (n_kb + 1) // 2
    scan_rows = 2 * t

    def count16(ref, pred):
        rows = 16
        def body(trip, acc):
            k0 = pl.multiple_of(trip * scan_rows, scan_rows)
            ind = pred(ref[pl.ds(k0, scan_rows), :])
            parts = [ind[r:r + rows, :] for r in range(0, scan_rows, rows)]
            while len(parts) > 1:
                parts = [parts[j] + parts[j + 1] for j in range(0, len(parts), 2)]
            return acc + parts[0]
        acc = lax.fori_loop(0, n_scan, body, jnp.zeros((rows, t), jnp.int16))
        return jnp.sum(acc.astype(F32), axis=0, keepdims=True)

    one16, zero16 = jnp.int16(1), jnp.int16(0)

    def kth_half(ref, need, n_all):
        def bit_step(it, state):
            prefix, cnt_prefix = state
            cand_u = prefix | lax.shift_left(jnp.int32(1), HALF_BITS - 1 - it)
            cand = (cand_u - HALF_OFFSET).astype(jnp.int16)
            cnt = count16(ref, lambda blk: jnp.where(blk >= cand, one16, zero16))
            take = cnt >= need
            return jnp.where(take, cand_u, prefix), jnp.where(take, cnt, cnt_prefix)
        prefix, cnt = lax.fori_loop(0, HALF_BITS, bit_step, (jnp.zeros((1, t), I32), n_all))
        return prefix - HALF_OFFSET, cnt


    def kth_high(need):
        def top_rows(trip, acc):
            k0 = pl.multiple_of(trip * scan_rows, scan_rows)
            blk = hi_ref[pl.ds(k0, scan_rows), :]
            parts = [blk[r:r + 16, :] for r in range(0, scan_rows, 16)]
            parts.append(acc)
            while len(parts) > 1:
                pairs = [(parts[j], parts[j + 1]) for j in range(0, len(parts) - 1, 2)]
                parts = [jnp.where(a > b, a, b) for a, b in pairs] + parts[len(pairs) * 2:]
            return parts[0]

        top = jnp.max(lax.fori_loop(0, n_scan, top_rows, masked16[:16, :]).astype(I32), axis=0, keepdims=True)

        def count_from(dist):
            cand = jnp.maximum(top - dist, -HALF_OFFSET).astype(jnp.int16)
            return count16(hi_ref, lambda blk: jnp.where(blk >= cand, one16, zero16))

        window = (1 << WINDOW_BITS) - 1
        n_window = count_from(window)
        fits = jnp.min(n_window) >= need
        n_bits = jnp.where(fits, WINDOW_BITS, HALF_BITS)
        widest = jnp.where(fits, window, (1 << HALF_BITS) - 1)
        n_widest = jnp.where(fits, n_window, (n_scan * scan_rows).astype(F32))

        def bit_step(it, state):
            dist, best, n_best = state
            bit = lax.shift_left(jnp.int32(1), n_bits - 1 - it)
            trial = dist | (bit - 1)
            cnt = count_from(trial)
            ok = cnt >= need
            return jnp.where(ok, dist, dist | bit), jnp.where(ok, trial, best), jnp.where(ok, cnt, n_best)

        init = (jnp.zeros((1, t), I32), jnp.full((1, t), widest, I32), jnp.full((1, t), n_widest, F32))
        _, best, n_best = lax.fori_loop(0, n_bits, bit_step, init)
        return jnp.maximum(top - best, -HALF_OFFSET), n_best

    kth_hi, n_ge_hi = kth_high(kf)
    kth_hi16 = kth_hi.astype(jnp.int16)

    def above_and_bucket(trip, acc):
        rows = 16
        k0 = pl.multiple_of(trip * scan_rows, scan_rows)
        hi = hi_ref[pl.ds(k0, scan_rows), :]
        lo_ref[pl.ds(k0, scan_rows), :] = jnp.where(hi == kth_hi16, lo_ref[pl.ds(k0, scan_rows), :],
                                                    jnp.int16(-HALF_OFFSET))
        ind = jnp.where(hi > kth_hi16, one16, zero16)
        parts = [ind[r:r + rows, :] for r in range(0, scan_rows, rows)]
        while len(parts) > 1:
            parts = [parts[j] + parts[j + 1] for j in range(0, len(parts), 2)]
        return acc + parts[0]

    n_gt_hi = jnp.sum(lax.fori_loop(0, n_scan, above_and_bucket, jnp.zeros((16, t), jnp.int16)).astype(F32),
                      axis=0, keepdims=True)
    need_lo = kf - n_gt_hi
    kth_lo, n_ge_lo = kth_half(lo_ref, need_lo, n_ge_hi - n_gt_hi)
    kth = lax.shift_left(kth_hi, HALF_BITS) + (kth_lo + HALF_OFFSET)

    excess = jnp.where(kth == imin, float(2 * seq), n_ge_lo - need_lo)
    any_excess = jnp.max(excess) > 0.0

    @pl.when(jnp.logical_not(any_excess))
    def _():
        def select(kb, _):
            k0 = pl.multiple_of(kb * t, t)
            bias_ref[pl.ds(k0, t), :] = jnp.where(key_ref[pl.ds(k0, t), :] >= kth, 0.0, NEG_BIG)
            return 0

        lax.fori_loop(0, n_kb, select, 0)

    @pl.when(any_excess)
    def _():
        r = lax.broadcasted_iota(I32, (t, t), 0)
        c = lax.broadcasted_iota(I32, (t, t), 1)
        later = jnp.where(c > r, 1.0, 0.0).astype(BF16)

        def select(step, tied_after):
            kb = n_kb - 1 - step
            k0 = pl.multiple_of(kb * t, t)
            kblk = key_ref[pl.ds(k0, t), :]
            tied = kblk == kth
            tied01 = jnp.where(tied, 1.0, 0.0).astype(BF16)
            after = _dot(later, tied01) + tied_after
            tie_bias = jnp.where(tied, jnp.where(after >= excess, 0.0, NEG_BIG), NEG_BIG)
            bias_ref[pl.ds(k0, t), :] = jnp.where(kblk > kth, 0.0, tie_bias)
            return after[0:1, :] + tied01[0:1, :].astype(F32)

        lax.fori_loop(0, n_kb, select, jnp.zeros((1, t), F32))

    q_t = [q_ref[:, p * LANES:(p + 1) * LANES].astype(F32).T for p in range(PAIR_W // LANES)]
    qh = [jnp.where(_head_row_mask(h % HEADS_PER_BLOCK), q_t[h // HEADS_PER_BLOCK], 0.0).astype(BF16)
          for h in range(N_DSA)]

    def scores(kb):
        k0 = pl.multiple_of(kb * t, t)
        bias = bias_ref[pl.ds(k0, t), :]
        return [_dot(k_ref[pl.ds(k0, t), (h // HEADS_PER_BLOCK) * LANES:(h // HEADS_PER_BLOCK + 1) * LANES], qh[h])
                + bias for h in range(N_DSA)]

    def values_t(kb):
        return [vt_ref[kb, h * V_ROWS:(h + 1) * V_ROWS, :] for h in range(N_DSA)]

    carry = _flash_blocks(n_kb, scores, values_t, tuple(_softmax_init(t) for _ in range(N_DSA)), s_ref)
    outs = [_softmax_result(st) for st in carry]
    outs += [jnp.zeros((HEAD_DIM, t), F32)] * (PAIR_W // HEAD_DIM - N_DSA)
    for p in range(PAIR_W // LANES):
        o_ref[:, p * LANES:(p + 1) * LANES] = _heads_to_token_rows(
            outs[p * HEADS_PER_BLOCK:(p + 1) * HEADS_PER_BLOCK])


def _dsa_attention(pb, ps, batch, seq):
    t = min(256, seq)
    nq = seq // t
    m = batch * seq
    topk = min(TOPK_MAX, seq // 4)
    return pl.pallas_call(
        functools.partial(_dsa_kernel, topk=topk, seq=seq),
        grid=(batch, nq),
        in_specs=[pl.BlockSpec((t, PAIR_W), lambda b, i: (b * nq + i, BLK_DSA_Q // 3)),
                  pl.BlockSpec((t, 4 * LANES), lambda b, i: (b * nq + i, BLK_IDX_Q // 4)),
                  pl.BlockSpec((t, LANES), lambda b, i: (b * nq + i, 0)),
                  pl.BlockSpec((seq, PAIR_W), lambda b, i: (b, BLK_DSA_K // 3)),
                  pl.BlockSpec((seq, PAIR_W), lambda b, i: (b, BLK_DSA_V // 3)),
                  pl.BlockSpec((seq, LANES), lambda b, i: (b, BLK_IDX_KA)),
                  pl.BlockSpec((seq, LANES), lambda b, i: (b, BLK_IDX_KB))],
        out_specs=pl.BlockSpec((t, PAIR_W), lambda b, i: (b * nq + i, 0)),
        out_shape=jax.ShapeDtypeStruct((m, PAIR_W), BF16),
        scratch_shapes=[pltpu.VMEM((seq, t), I32), pltpu.VMEM((seq + t, t), jnp.int16),
                        pltpu.VMEM((seq + t, t), jnp.int16),
                        pltpu.VMEM((seq, t), F32),
                        pltpu.VMEM((nq, N_DSA * V_ROWS, t), BF16), pltpu.VMEM((2, N_DSA, t, t), F32)],
        compiler_params=_params(2),
        name="dsa_attention",
    )(pb, pb, ps, pb, pb, pb, pb)


def _mixer_out_cross_kernel(x_ref, gain_ref, of_ref, od_ref, os_ref, wg_ref, wf_ref, wd_ref, ws_ref, wo_ref,
                            ca_gain_ref, ca_wq_ref, kv_ref, ca_wo_ref, o_ref):
    d = x_ref.shape[1]
    x = x_ref[...]
    h = _rms(x, gain_ref[...]).astype(BF16)
    merged = None
    for b, (o_b, w_b) in enumerate(((of_ref, wf_ref), (od_ref, wd_ref), (os_ref, ws_ref))):
        term = jax.nn.sigmoid(_dot(h, wg_ref[:, b * d:(b + 1) * d])) * _dot(o_b[...], w_b[...])
        merged = term if merged is None else merged + term
    x = x + _dot(merged.astype(BF16), wo_ref[...])

    q = _dot(_rms(x, ca_gain_ref[...]).astype(BF16), ca_wq_ref[...])
    scale = CA_HEAD_DIM ** -0.5
    outs = []
    for hd in range(N_CA_HEADS):
        qh = q[:, hd * CA_HEAD_DIM:(hd + 1) * CA_HEAD_DIM].astype(BF16)
        logits = _dot_nt(qh, kv_ref[:, hd * CA_HEAD_DIM:(hd + 1) * CA_HEAD_DIM]) * scale
        e = jnp.exp(logits - jnp.max(logits, axis=1, keepdims=True))
        pv = _dot(e.astype(BF16), kv_ref[:, CA_W + hd * CA_HEAD_DIM:CA_W + (hd + 1) * CA_HEAD_DIM])
        outs.append(pv / jnp.sum(e, axis=1, keepdims=True))
    o_ref[...] = x + _dot(jnp.concatenate(outs, axis=1).astype(BF16), ca_wo_ref[...])


def _mixer_out_cross(x2, gain, o_fox, o_dsa, o_sb, wg, wf, wd, ws, wo, ca_gain, ca_wq, kv, ca_wo, batch, seq):
    m, d = x2.shape
    tm = min(512, seq)
    n_mem = kv.shape[0] // batch
    per_batch = seq // tm
    row = lambda w: pl.BlockSpec((tm, w), lambda i: (i, 0))
    return pl.pallas_call(
        _mixer_out_cross_kernel,
        grid=(m // tm,),
        in_specs=[row(d), _resident((1, d)), row(FOX_W), row(PAIR_W), row(PAIR_W), _resident((d, N_BRANCH * d)),
                  _resident((FOX_W, d)), _resident((PAIR_W, d)), _resident((PAIR_W, d)), _resident((d, d)),
                  _resident((1, d)), _resident((d, CA_W)),
                  pl.BlockSpec((n_mem, 2 * CA_W), lambda i: (i // per_batch, 0)), _resident((CA_W, d))],
        out_specs=row(d),
        out_shape=jax.ShapeDtypeStruct((m, d), F32),
        compiler_params=_params(1),
        name="mixer_out_cross",
    )(x2, gain.reshape(1, d), o_fox, o_dsa, o_sb, wg, wf, wd, ws, wo, ca_gain.reshape(1, d), ca_wq, kv, ca_wo)


def _norm_matmul_kernel(x_ref, g_ref, w_ref, o_ref):
    o_ref[...] = _dot(_rms(x_ref[...], g_ref[...]).astype(BF16), w_ref[...]).astype(o_ref.dtype)


def _norm_matmul(x2, gain, w, out_dtype):
    m, d = x2.shape
    n = w.shape[1]
    tm = min(512, m)
    return pl.pallas_call(
        _norm_matmul_kernel,
        grid=(m // tm,),
        in_specs=[pl.BlockSpec((tm, d), lambda i: (i, 0)), _resident((1, d)), _resident((d, n))],
        out_specs=pl.BlockSpec((tm, n), lambda i: (i, 0)),
        out_shape=jax.ShapeDtypeStruct((m, n), out_dtype),
        compiler_params=_params(1),
        name="norm_matmul",
    )(x2, gain.reshape(1, d), w)


def _pad_cols(w, n):
    return jnp.pad(w, ((0, 0), (0, n - w.shape[1])))


def _pad_rows(w, n):
    return jnp.pad(w, ((0, n - w.shape[0]), (0, 0)))


def _mixer_weights(w_in):
    offs = np.cumsum((0,) + IN_SPLITS)
    part = lambda j: w_in[:, offs[j]:offs[j + 1]]
    thirds = lambda w: jnp.split(w, 3, axis=1)
    scale = HEAD_DIM ** -0.5
    fq, fk, fv = thirds(part(0))
    dq, dk, dv = thirds(part(2))
    sq, sk, sv = thirds(part(6))
    iq, ik, iw = part(3), part(4), part(5)
    zk = jnp.zeros_like(ik)
    small = jnp.concatenate([part(1), jnp.zeros((w_in.shape[0], SMALL_IDXW_LANE - N_FOX), w_in.dtype), iw], axis=1)
    wb = jnp.concatenate(
        [_pad_cols(dq * scale, PAIR_W), _pad_cols(dk, PAIR_W),
         jnp.concatenate([ik, zk], axis=1), jnp.concatenate([zk, ik], axis=1), iq,
         fq * scale, fk, fv, _pad_cols(dv, PAIR_W),
         _pad_cols(sq * scale, PAIR_W), _pad_cols(sk, PAIR_W), _pad_cols(sv, PAIR_W),
         _pad_cols(small, LANES)], axis=1).astype(BF16)
    return wb, part(7).astype(BF16)


def _mixer_and_cross(x2, cos, sin_signed, batch, seq, gain, w_in, b_fgate, w_fox_out, w_dsa_out, w_sb_out, w_out,
                     ca_gain, ca_wq, kv, ca_wo):
    wb, wg = _mixer_weights(w_in)
    bias_row = _pad_cols(b_fgate.reshape(1, N_FOX).astype(F32), LANES)
    pb, ps, k_side, q_side = _mixer_in(x2, gain, cos, sin_signed, wb, bias_row, batch, seq)

    o_fox = _fox_attention(pb, k_side, q_side, batch, seq)
    o_dsa = _dsa_attention(pb, ps, batch, seq)
    o_sb = _sb_attention(pb, batch, seq)
    return _mixer_out_cross(x2, gain, o_fox, o_dsa, o_sb, wg, w_fox_out.astype(BF16),
                            _pad_rows(w_dsa_out, PAIR_W).astype(BF16), _pad_rows(w_sb_out, PAIR_W).astype(BF16),
                            w_out.astype(BF16), ca_gain, ca_wq.astype(BF16), kv, ca_wo.astype(BF16), batch, seq)


def kernel(x, mem, positions, ffn1_norm, ffn1_w_gu, ffn1_w_down, mix_norm, w_in, b_fgate, w_fox_out, w_dsa_out,
           w_sb_out, w_out, ca_norm, mem_norm, ca_w_q, ca_w_kv, ca_w_o, ffn2_norm, ffn2_w_gu, ffn2_w_down,
           final_norm):
    batch, seq, d = x.shape
    depth = ffn1_norm.shape[0]
    x2 = x.reshape(batch * seq, d)
    mem2 = mem.reshape(-1, d)
    pos2 = positions.reshape(batch * seq, 1).astype(I32)
    half = HEAD_DIM // 2
    inv_freq = jnp.power(ROPE_THETA, -jnp.arange(half, dtype=F32) * (2.0 / HEAD_DIM))
    invf = jnp.tile(inv_freq, LANES // half).reshape(1, LANES)
    cos, sin_signed = _rope_tables(pos2, invf)
    for l in range(depth):
        x2 = _ffn(x2, ffn1_norm[l], ffn1_w_gu[l], ffn1_w_down[l])
        kv = _norm_matmul(mem2, mem_norm[l], ca_w_kv[l].astype(BF16), BF16)
        x2 = _mixer_and_cross(x2, cos, sin_signed, batch, seq, mix_norm[l], w_in[l], b_fgate[l], w_fox_out[l],
                              w_dsa_out[l], w_sb_out[l], w_out[l], ca_norm[l], ca_w_q[l], kv, ca_w_o[l])
        x2 = _ffn(x2, ffn2_norm[l], ffn2_w_gu[l], ffn2_w_down[l],
                  final_gain=final_norm if l == depth - 1 else None)
    return x2.reshape(batch, seq, d)
```

```python
import functools

import jax
import jax.numpy as jnp
import numpy as np
from jax import lax
from jax.experimental import pallas as pl
from jax.experimental.pallas import tpu as pltpu

F32 = jnp.float32
BF16 = jnp.bfloat16
I32 = jnp.int32

D_MODEL = 1024
HEAD_DIM = 64
N_FOX = 6
N_DSA = 5
N_SB = 5
N_IDX_HEADS = 8
IDX_DIM = 64
TOPK_MAX = 256
N_CA_HEADS = 4
CA_HEAD_DIM = 128
D_FF = 2816
ROPE_THETA = 10000.0
NORM_EPS = 1e-6
N_BRANCH = 3
HALF_STEP = 0.5
FOX_W = N_FOX * HEAD_DIM
DSA_W = N_DSA * HEAD_DIM
SB_W = N_SB * HEAD_DIM
CA_W = N_CA_HEADS * CA_HEAD_DIM
IN_SPLITS = (3 * FOX_W, N_FOX, 3 * DSA_W, N_IDX_HEADS * IDX_DIM, IDX_DIM, N_IDX_HEADS, 3 * SB_W,
             N_BRANCH * D_MODEL)

LANES = 128
HEADS_PER_BLOCK = LANES // HEAD_DIM
PAIR_W = 3 * LANES
VMEM_LIMIT = 56 * 2**20

BLK_DSA_Q, BLK_DSA_K = 0, 3
BLK_IDX_KA = 6
BLK_IDX_KB = 7
BLK_IDX_Q = 8
N_ROPE_BLOCKS = 12
BLK_FOX_Q, BLK_FOX_K, BLK_FOX_V = 12, 15, 18
BLK_DSA_V = 21
BLK_SB_Q, BLK_SB_K, BLK_SB_V = 24, 27, 30
N_PROJ_BLOCKS = 33
PROJ_CHUNK = 4 * LANES
SMALL_FGATE_LANE = 0
SMALL_IDXW_LANE = 8

NEG_BIG = -1e30
SB_CUTOFF = -110.0
INT_MIN = -2**31


def _params(n_grid):
    return pltpu.CompilerParams(dimension_semantics=("arbitrary",) * n_grid, vmem_limit_bytes=VMEM_LIMIT)


def _resident(shape):
    nd = len(shape)
    return pl.BlockSpec(shape, lambda *_: (0,) * nd, pipeline_mode=pl.Buffered(1))


def _rms(x, g):
    return x * lax.rsqrt(jnp.mean(x * x, axis=-1, keepdims=True) + NORM_EPS) * g


def _dot(a, b):
    return jnp.dot(a, b, preferred_element_type=F32)


def _dot_nt(a, b):
    return lax.dot_general(a, b, (((1,), (1,)), ((), ())), preferred_element_type=F32)


def _split3(x):
    hi = x.astype(BF16)
    r = x - hi.astype(F32)
    mid = r.astype(BF16)
    lo = (r - mid.astype(F32)).astype(BF16)
    return hi, mid, lo


def _log1p_exp_neg_abs(z):
    return jnp.log1p(jnp.exp(-jnp.abs(z)))


def _ffn_chunks():
    out, c = [], 0
    while c < D_FF:
        w = min(512, D_FF - c)
        out.append((c, w))
        c += w
    return tuple(out)


def _ffn_kernel(*refs, final):
    if final:
        x_ref, g_ref, wgu_ref, wd_ref, fn_ref, o_ref, a_ref = refs
    else:
        x_ref, g_ref, wgu_ref, wd_ref, o_ref, a_ref = refs
    x = x_ref[...]
    h = _rms(x, g_ref[...]).astype(BF16)
    for c0, w in _ffn_chunks():
        g = _dot(h, wgu_ref[:, c0:c0 + w])
        u = _dot(h, wgu_ref[:, D_FF + c0:D_FF + c0 + w])
        a_ref[:, c0:c0 + w] = (g * jax.nn.sigmoid(g) * u).astype(BF16)
    y = x + HALF_STEP * _dot(a_ref[...], wd_ref[...])
    if final:
        y = _rms(y, fn_ref[...])
    o_ref[...] = y


def _ffn(x2, gain, w_gu, w_down, final_gain=None):
    m, d = x2.shape
    tm = min(512, m)
    final = final_gain is not None
    in_specs = [pl.BlockSpec((tm, d), lambda i: (i, 0)), _resident((1, d)),
                _resident((d, 2 * D_FF)), _resident((D_FF, d))]
    args = [x2, gain.reshape(1, d), w_gu.astype(BF16), w_down.astype(BF16)]
    if final:
        in_specs.append(_resident((1, d)))
        args.append(final_gain.reshape(1, d))
    return pl.pallas_call(
        functools.partial(_ffn_kernel, final=final),
        grid=(m // tm,),
        in_specs=in_specs,
        out_specs=pl.BlockSpec((tm, d), lambda i: (i, 0)),
        out_shape=jax.ShapeDtypeStruct((m, d), F32),
        scratch_shapes=[pltpu.VMEM((tm, D_FF), BF16)],
        compiler_params=_params(1),
        name="ffn",
    )(*args)


def _first_half_lanes():
    lane = lax.broadcasted_iota(I32, (1, LANES), 1)
    return (lane % HEAD_DIM) < (HEAD_DIM // 2)


def _rope_tables_kernel(pos_ref, invf_ref, cos_ref, sin_ref):
    ang = pos_ref[...].astype(F32) * invf_ref[...]
    sin = jnp.sin(ang)
    cos_ref[...] = jnp.cos(ang)
    sin_ref[...] = jnp.where(_first_half_lanes(), -sin, sin)


def _rope_tables(pos2, invf):
    m = pos2.shape[0]
    tm = min(512, m)
    tok = pl.BlockSpec((tm, LANES), lambda i: (i, 0))
    return pl.pallas_call(
        _rope_tables_kernel,
        grid=(m // tm,),
        in_specs=[pl.BlockSpec((tm, 1), lambda i: (i, 0)), _resident((1, LANES))],
        out_specs=[tok, tok],
        out_shape=[jax.ShapeDtypeStruct((m, LANES), F32)] * 2,
        compiler_params=_params(1),
        name="rope_tables",
    )(pos2, invf)


FOX_FEAT_ROWS = 16


def _fox_feature_maps():
    k_map = np.zeros((3, LANES, FOX_W), np.float32)
    q_map = np.zeros((3, LANES, LANES), np.float32)
    k_one = np.zeros((1, FOX_W), np.float32)
    q_one = np.zeros((1, LANES), np.float32)
    for h in range(N_FOX):
        p, hh = divmod(h, HEADS_PER_BLOCK)
        for j in range(3):
            k_map[j, h, p * LANES + 3 * hh + j] = -1.0
            q_map[j, h, FOX_FEAT_ROWS * h + 6 + j] = 1.0
            k_one[0, p * LANES + 6 + j] = 1.0
            q_one[0, FOX_FEAT_ROWS * h + 3 * hh + j] = 1.0
    return (jnp.asarray(k_map, BF16), jnp.asarray(q_map, BF16), jnp.asarray(k_one), jnp.asarray(q_one))


def _fox_decay_features(logits, b_ref, kmap_ref, qmap_ref, kone_ref, qone_ref, kf_ref, qf_ref, carry_ref):
    tc = logits.shape[0]
    z = logits + b_ref[...]
    log_f = jnp.minimum(z, 0.0) - _log1p_exp_neg_abs(z)
    row = lax.broadcasted_iota(I32, (tc, tc), 0)
    col = lax.broadcasted_iota(I32, (tc, tc), 1)
    tri = jnp.where(col <= row, 1.0, 0.0).astype(BF16)
    hi, mid, lo = _split3(log_f)
    yield
    cs = _dot(tri, hi) + _dot(tri, mid) + _dot(tri, lo) + carry_ref[...]
    carry_ref[...] = cs[tc - 1:tc, :]
    terms = _split3(cs)
    yield
    kf = kone_ref[...]
    qf = qone_ref[...]
    for j in range(3):
        kf = kf + _dot(terms[j], kmap_ref[j])
        qf = qf + _dot(terms[j], qmap_ref[j])
    kf_ref[...] = kf.astype(BF16)
    qf_ref[...] = qf.astype(BF16)


def _mixer_in_kernel(x_ref, g_ref, cos_ref, sin_ref, wb_ref, b_ref, kmap_ref, qmap_ref, kone_ref, qone_ref,
                     pb_ref, ps_ref, kf_ref, qf_ref, carry_ref, *, tiles_per_seq):
    @pl.when(pl.program_id(0) % tiles_per_seq == 0)
    def _():
        carry_ref[...] = jnp.zeros_like(carry_ref)

    h = _rms(x_ref[...], g_ref[...]).astype(BF16)
    cos = cos_ref[...]
    sin_signed = sin_ref[...]
    first_half = _first_half_lanes()
    n_cols = (N_PROJ_BLOCKS + 1) * LANES
    starts = list(range(0, n_cols, PROJ_CHUNK))
    decay_stages = iter(())
    for c0 in starts[-1:] + starts[:-1]:
        c1 = min(c0 + PROJ_CHUNK, n_cols)
        res = _dot(h, wb_ref[:, c0:c1])
        next(decay_stages, None)
        if c1 == n_cols:
            small = res[:, c1 - c0 - LANES:]
            ps_ref[...] = small
            decay_stages = _fox_decay_features(small, b_ref, kmap_ref, qmap_ref, kone_ref, qone_ref, kf_ref,
                                               qf_ref, carry_ref)
            next(decay_stages)
            c1 -= LANES
            res = res[:, :c1 - c0]
        if c0 < N_ROPE_BLOCKS * LANES:
            parts = []
            for b in range((c1 - c0) // LANES):
                xb = res[:, b * LANES:(b + 1) * LANES]
                partner = jnp.where(first_half, pltpu.roll(xb, LANES - HEAD_DIM // 2, 1),
                                    pltpu.roll(xb, HEAD_DIM // 2, 1))
                parts.append(xb * cos + partner * sin_signed)
            res = jnp.concatenate(parts, axis=1)
        pb_ref[:, c0:c1] = res.astype(BF16)


def _mixer_in(x2, gain, cos, sin_signed, wb, bias_row, batch, seq):
    m, d = x2.shape
    tm = min(512, seq)
    nb = N_PROJ_BLOCKS * LANES
    tok = lambda w: pl.BlockSpec((tm, w), lambda i: (i, 0))
    pb, ps, k_side, q_tok = pl.pallas_call(
        functools.partial(_mixer_in_kernel, tiles_per_seq=seq // tm),
        grid=(m // tm,),
        in_specs=[tok(d), _resident((1, d)), tok(LANES), tok(LANES), _resident((d, nb + LANES)),
                  _resident((1, LANES)), _resident((3, LANES, FOX_W)), _resident((3, LANES, LANES)),
                  _resident((1, FOX_W)), _resident((1, LANES))],
        out_specs=[tok(nb), tok(LANES), tok(FOX_W), tok(LANES)],
        out_shape=[jax.ShapeDtypeStruct((m, nb), BF16), jax.ShapeDtypeStruct((m, LANES), F32),
                   jax.ShapeDtypeStruct((m, FOX_W), BF16), jax.ShapeDtypeStruct((m, LANES), BF16)],
        scratch_shapes=[pltpu.VMEM((1, LANES), F32)],
        compiler_params=_params(1),
        name="mixer_in",
    )(x2, gain.reshape(1, d), cos, sin_signed, wb, bias_row, *_fox_feature_maps())
    q_side = q_tok[:, :N_FOX * FOX_FEAT_ROWS].reshape(batch, seq, N_FOX, FOX_FEAT_ROWS).transpose(0, 2, 3, 1)
    return pb, ps, k_side, q_side


def _head_lane_mask(hh):
    lane = lax.broadcasted_iota(I32, (1, LANES), 1)
    return (lane < HEAD_DIM) if hh == 0 else (lane >= HEAD_DIM)


def _head_row_mask(hh):
    row = lax.broadcasted_iota(I32, (LANES, 1), 0)
    return (row < HEAD_DIM) if hh == 0 else (row >= HEAD_DIM)


def _transpose_bf16(x):
    return x.astype(F32).T.astype(BF16)


def _fill_transposed(src_ref, dst_ref):
    n_blocks, _, tk = dst_ref.shape
    for j in range(n_blocks):
        dst_ref[j] = _transpose_bf16(src_ref[j * tk:(j + 1) * tk, :])


V_ROWS = HEAD_DIM + 16


def _fill_transposed_with_ones(src_ref, dst_ref, n_heads):
    n_blocks, _, tk = dst_ref.shape
    ones_pad = jnp.where(lax.broadcasted_iota(I32, (V_ROWS - HEAD_DIM, tk), 0) == 0, 1.0, 0.0).astype(BF16)
    for j in range(n_blocks):
        v_t = _transpose_bf16(src_ref[j * tk:(j + 1) * tk, :])
        for h in range(n_heads):
            dst_ref[j, h * V_ROWS:h * V_ROWS + HEAD_DIM, :] = v_t[h * HEAD_DIM:(h + 1) * HEAD_DIM, :]
            dst_ref[j, h * V_ROWS + HEAD_DIM:(h + 1) * V_ROWS, :] = ones_pad


def _softmax_steps(carry, scores, values_t):
    stats = []
    for (m, _), s in zip(carry, scores):
        m_new = jnp.maximum(m, jnp.max(s, axis=0, keepdims=True))
        stats.append((m_new, jnp.exp(m - m_new), jnp.exp(s - m_new).astype(BF16)))
    return tuple((m_new, alpha * acc + _dot(v_t, p))
                 for (m_new, alpha, p), (_, acc), v_t in zip(stats, carry, values_t))


def _softmax_init(tq):
    return (jnp.full((1, tq), NEG_BIG, F32), jnp.zeros((V_ROWS, tq), F32))


def _softmax_result(state):
    _, acc = state
    return acc[:HEAD_DIM, :] / acc[HEAD_DIM:HEAD_DIM + 1, :]


def _flash_blocks(n, score_fn, value_fn, carry, s_ref):
    n_heads = len(carry)

    def load(slot):
        return [s_ref[slot, h] for h in range(n_heads)]

    def store(slot, kb):
        for h, s in enumerate(score_fn(kb)):
            s_ref[slot, h] = s

    odd = n % 2
    carry = lax.cond(odd == 1, lambda c: _softmax_steps(c, score_fn(0), value_fn(0)), lambda c: c, carry)

    @pl.when(n >= 2)
    def _():
        store(0, odd)

    def body(j, c):
        kb = odd + 2 * j
        store(1, kb + 1)
        c = _softmax_steps(c, load(0), value_fn(kb))
        store(0, jnp.minimum(kb + 2, n - 1))
        return _softmax_steps(c, load(1), value_fn(kb + 1))

    return lax.fori_loop(0, n // 2, body, carry)


def _heads_to_token_rows(out_t):
    return jnp.concatenate(out_t, axis=0).T.astype(BF16)


def _fox_kernel(q_ref, k_ref, v_ref, kf_ref, qf_ref, o_ref, vt_ref, s_ref):
    tq = q_ref.shape[0]
    tk = vt_ref.shape[2]
    i = pl.program_id(1)

    @pl.when(i == 0)
    def _():
        _fill_transposed_with_ones(v_ref, vt_ref, N_FOX)

    key_pos = lax.broadcasted_iota(I32, (tk, tq), 0)
    qry_pos = lax.broadcasted_iota(I32, (tk, tq), 1)
    causal = [key_pos + d * tk <= qry_pos for d in range(2)]
    pad = jnp.zeros((LANES - FOX_FEAT_ROWS, tq), BF16)
    qx = []
    for p in range(FOX_W // LANES):
        q_t = q_ref[:, p * LANES:(p + 1) * LANES].astype(F32).T
        for hh in range(HEADS_PER_BLOCK):
            qx.append(jnp.concatenate([jnp.where(_head_row_mask(hh), q_t, 0.0).astype(BF16),
                                       qf_ref[0, p * HEADS_PER_BLOCK + hh], pad], axis=0))

    def store(slot, kb):
        k0 = pl.multiple_of(kb * tk, tk)
        for h in range(N_FOX):
            p = h // HEADS_PER_BLOCK
            kx = jnp.concatenate([k_ref[pl.ds(k0, tk), p * LANES:(p + 1) * LANES],
                                  kf_ref[pl.ds(k0, tk), p * LANES:(p + 1) * LANES]], axis=1)
            s_ref[slot, h] = _dot(kx, qx[h])

    def step(carry, slot, kb, mask=None):
        scores = [s_ref[slot, h] for h in range(N_FOX)]
        if mask is not None:
            scores = [jnp.where(mask, s, NEG_BIG) for s in scores]
        return _softmax_steps(carry, scores, [vt_ref[kb, h * V_ROWS:(h + 1) * V_ROWS, :] for h in range(N_FOX)])

    def trip(j, carry):
        kb = 2 * j
        store(1, kb + 1)
        carry = step(carry, 0, kb)
        store(0, kb + 2)
        return step(carry, 1, kb + 1)

    store(0, 0)
    carry = lax.fori_loop(0, i, trip, tuple(_softmax_init(tq) for _ in range(N_FOX)))
    kb = 2 * i
    store(1, kb + 1)
    carry = step(carry, 0, kb, causal[0])
    carry = step(carry, 1, kb + 1, causal[1])
    for p in range(FOX_W // LANES):
        o_ref[:, p * LANES:(p + 1) * LANES] = _heads_to_token_rows(
            [_softmax_result(st) for st in carry[p * HEADS_PER_BLOCK:(p + 1) * HEADS_PER_BLOCK]])


def _fox_attention(pb, k_side, q_side, batch, seq):
    tq = min(512, seq)
    tk = tq // 2
    nq = seq // tq
    m = batch * seq
    return pl.pallas_call(
        _fox_kernel,
        grid=(batch, nq),
        in_specs=[pl.BlockSpec((tq, FOX_W), lambda b, i: (b * nq + i, BLK_FOX_Q // 3)),
                  pl.BlockSpec((seq, FOX_W), lambda b, i: (b, BLK_FOX_K // 3)),
                  pl.BlockSpec((seq, FOX_W), lambda b, i: (b, BLK_FOX_V // 3)),
                  pl.BlockSpec((seq, FOX_W), lambda b, i: (b, 0)),
                  pl.BlockSpec((1, N_FOX, FOX_FEAT_ROWS, tq), lambda b, i: (b, 0, 0, i))],
        out_specs=pl.BlockSpec((tq, FOX_W), lambda b, i: (b * nq + i, 0)),
        out_shape=jax.ShapeDtypeStruct((m, FOX_W), BF16),
        scratch_shapes=[pltpu.VMEM((seq // tk, N_FOX * V_ROWS, tk), BF16), pltpu.VMEM((2, N_FOX, tk, tq), F32)],
        compiler_params=_params(2),
        name="fox_attention",
    )(pb, pb, pb, k_side, q_side)


def _sb_kernel(q_ref, k_ref, v_ref, o_ref, vt_ref):
    tq = q_ref.shape[0]
    tk = vt_ref.shape[2]
    i = pl.program_id(1)

    @pl.when(i == 0)
    def _():
        _fill_transposed(v_ref, vt_ref)

    heads = range(N_SB)
    q_t = [q_ref[:, p * LANES:(p + 1) * LANES].astype(F32).T for p in range(PAIR_W // LANES)]
    qh = [jnp.where(_head_row_mask(h % HEADS_PER_BLOCK), q_t[h // HEADS_PER_BLOCK], 0.0).astype(BF16) for h in heads]
    r = lax.broadcasted_iota(I32, (tk, tk), 0)
    c = lax.broadcasted_iota(I32, (tk, tk), 1)
    upper = jnp.where(c > r, 1.0, 0.0).astype(BF16)
    key_pos = lax.broadcasted_iota(I32, (tk, tq), 0)
    qry_pos = lax.broadcasted_iota(I32, (tk, tq), 1) + i * tq

    def process(kb, state, masked):
        k0 = pl.multiple_of(kb * tk, tk)
        strict = key_pos + k0 < qry_pos
        z = [_dot(k_ref[pl.ds(k0, tk), (h // HEADS_PER_BLOCK) * LANES:(h // HEADS_PER_BLOCK + 1) * LANES], qh[h])
             for h in heads]
        log_beta, log_1m = [], []
        for zh in z:
            l1m = jnp.minimum(-zh, 0.0) - jnp.log(1.0 + jnp.exp(-jnp.abs(zh)))
            log_beta.append(l1m + zh)
            log_1m.append(jnp.where(strict, l1m, 0.0) if masked else l1m)
        after = []
        for h in heads:
            hi, mid, lo = _split3(log_1m[h])
            after.append(_dot(upper, hi) + _dot(upper, mid) + _dot(upper, lo) + state[h][0])
        weights = []
        for h in heads:
            a = jnp.exp(log_beta[h] + after[h])
            weights.append((jnp.where(strict, a, 0.0) if masked else a).astype(BF16))
        return tuple((state[h][0] + jnp.sum(log_1m[h], axis=0, keepdims=True),
                      state[h][1] + _dot(vt_ref[kb, h * HEAD_DIM:(h + 1) * HEAD_DIM, :], weights[h]))
                     for h in heads)

    def run_max(state):
        m = state[0][0]
        for run, _ in state[1:]:
            m = jnp.maximum(m, run)
        return jnp.max(m)

    state = tuple((jnp.zeros((1, tq), F32), jnp.zeros((HEAD_DIM, tq), F32)) for _ in heads)
    n_diag = tq // tk
    for d in range(n_diag):
        state = process((i + 1) * n_diag - 1 - d, state, True)

    def cond(carry):
        kb, worst, _ = carry
        return jnp.logical_and(kb >= 0, worst > SB_CUTOFF)

    def body(carry):
        kb, _, st = carry
        st = process(kb, st, False)
        return kb - 1, run_max(st), st

    _, _, state = lax.while_loop(cond, body, (i * n_diag - 1, run_max(state), state))
    for p in range(PAIR_W // LANES):
        outs = [state[h][1] for h in heads if h // HEADS_PER_BLOCK == p]
        outs += [jnp.zeros((HEAD_DIM, tq), F32)] * (HEADS_PER_BLOCK - len(outs))
        o_ref[:, p * LANES:(p + 1) * LANES] = _heads_to_token_rows(outs)


def _sb_attention(pb, batch, seq):
    tq = min(256, seq)
    tk = min(256, seq)
    nq = seq // tq
    m = batch * seq
    return pl.pallas_call(
        _sb_kernel,
        grid=(batch, nq),
        in_specs=[pl.BlockSpec((tq, PAIR_W), lambda b, i: (b * nq + i, BLK_SB_Q // 3)),
                  pl.BlockSpec((seq, PAIR_W), lambda b, i: (b, BLK_SB_K // 3)),
                  pl.BlockSpec((seq, PAIR_W), lambda b, i: (b, BLK_SB_V // 3))],
        out_specs=pl.BlockSpec((tq, PAIR_W), lambda b, i: (b * nq + i, 0)),
        out_shape=jax.ShapeDtypeStruct((m, PAIR_W), BF16),
        scratch_shapes=[pltpu.VMEM((seq // tk, PAIR_W, tk), BF16)],
        compiler_params=_params(2),
        name="sb_attention",
    )(pb, pb, pb)


HALF_BITS = 16
HALF_OFFSET = 1 << (HALF_BITS - 1)


def _dsa_kernel(q_ref, iq_ref, w_ref, k_ref, v_ref, ka_ref, kb_ref, o_ref, key_ref, hi_ref, lo_ref, bias_ref,
                vt_ref, s_ref, *, topk, seq):
    t = q_ref.shape[0]
    i = pl.program_id(1)

    @pl.when(i == 0)
    def _():
        _fill_transposed_with_ones(v_ref, vt_ref, N_DSA)

    n_kb = i + 1
    kf = float(topk)
    imin = jnp.int32(INT_MIN)
    key_pos = lax.broadcasted_iota(I32, (t, t), 0)
    qry_pos = lax.broadcasted_iota(I32, (t, t), 1) + i * t

    iq_t = [_transpose_bf16(iq_ref[:, g * LANES:(g + 1) * LANES]) for g in range(N_IDX_HEADS // 2)]
    w_t = (w_ref[...] * (N_IDX_HEADS ** -0.5) * (IDX_DIM ** -0.5)).T

    def index_block(kb):
        k0 = pl.multiple_of(kb * t, t)
        ka = ka_ref[pl.ds(k0, t), :]
        kb_ = kb_ref[pl.ds(k0, t), :]
        score = jnp.zeros((t, t), F32)
        for g in range(N_IDX_HEADS // 2):
            lo = SMALL_IDXW_LANE + 2 * g
            score = score + jnp.maximum(_dot(ka, iq_t[g]), 0.0) * w_t[lo:lo + 1, :]
            score = score + jnp.maximum(_dot(kb_, iq_t[g]), 0.0) * w_t[lo + 1:lo + 2, :]
        bits = lax.bitcast_convert_type(score + 0.0, I32)
        key = jnp.where(bits < 0, bits ^ jnp.int32(0x7FFFFFFF), bits)
        key = jnp.where(key_pos + k0 <= qry_pos, key, imin)
        key_ref[pl.ds(k0, t), :] = key
        hi_ref[pl.ds(k0, t), :] = lax.shift_right_arithmetic(key, HALF_BITS).astype(jnp.int16)
        lo_ref[pl.ds(k0, t), :] = ((key & (2 * HALF_OFFSET - 1)) - HALF_OFFSET).astype(jnp.int16)

    odd = n_kb % 2

    @pl.when(odd == 1)
    def _():
        index_block(0)

    def index_pair(j, _):
        index_block(odd + 2 * j)
        index_block(odd + 2 * j + 1)
        return 0

    lax.fori_loop(0, n_kb // 2, index_pair, 0)

    masked16 = jnp.full((t, t), -HALF_OFFSET, jnp.int16)

    @pl.when(odd == 1)
    def _():
        k0 = pl.multiple_of(n_kb * t, t)
        hi_ref[pl.ds(k0, t), :] = masked16
        lo_ref[pl.ds(k0, t), :] = masked16

    n_scan = (n_kb + 1) // 2
    scan_rows = 2 * t

    def count16(ref, pred):
        rows = 16
        def body(trip, acc):
            k0 = pl.multiple_of(trip * scan_rows, scan_rows)
            ind = pred(ref[pl.ds(k0, scan_rows), :])
            parts = [ind[r:r + rows, :] for r in range(0, scan_rows, rows)]
            while len(parts) > 1:
                parts = [parts[j] + parts[j + 1] for j in range(0, len(parts), 2)]
            return acc + parts[0]
        acc = lax.fori_loop(0, n_scan, body, jnp.zeros((rows, t), jnp.int16))
        return jnp.sum(acc.astype(F32), axis=0, keepdims=True)

    one16, zero16 = jnp.int16(1), jnp.int16(0)

    def kth_half(ref, need, n_all):
        def bit_step(it, state):
            prefix, cnt_prefix = state
            cand_u = prefix | lax.shift_left(jnp.int32(1), HALF_BITS - 1 - it)
            cand = (cand_u - HALF_OFFSET).astype(jnp.int16)
            cnt = count16(ref, lambda blk: jnp.where(blk >= cand, one16, zero16))
            take = cnt >= need
            return jnp.where(take, cand_u, prefix), jnp.where(take, cnt, cnt_prefix)
        prefix, cnt = lax.fori_loop(0, HALF_BITS, bit_step, (jnp.zeros((1, t), I32), n_all))
        return prefix - HALF_OFFSET, cnt


    n_scanned = (n_scan * scan_rows).astype(F32)
    kth_hi, n_ge_hi = kth_half(hi_ref, kf, jnp.full((1, t), n_scanned, F32))
    kth_hi16 = kth_hi.astype(jnp.int16)

    def above_and_bucket(trip, acc):
        rows = 16
        k0 = pl.multiple_of(trip * scan_rows, scan_rows)
        hi = hi_ref[pl.ds(k0, scan_rows), :]
        lo_ref[pl.ds(k0, scan_rows), :] = jnp.where(hi == kth_hi16, lo_ref[pl.ds(k0, scan_rows), :],
                                                    jnp.int16(-HALF_OFFSET))
        ind = jnp.where(hi > kth_hi16, one16, zero16)
        parts = [ind[r:r + rows, :] for r in range(0, scan_rows, rows)]
        while len(parts) > 1:
            parts = [parts[j] + parts[j + 1] for j in range(0, len(parts), 2)]
        return acc + parts[0]

    n_gt_hi = jnp.sum(lax.fori_loop(0, n_scan, above_and_bucket, jnp.zeros((16, t), jnp.int16)).astype(F32),
                      axis=0, keepdims=True)
    need_lo = kf - n_gt_hi
    kth_lo, n_ge_lo = kth_half(lo_ref, need_lo, n_ge_hi - n_gt_hi)
    kth = lax.shift_left(kth_hi, HALF_BITS) + (kth_lo + HALF_OFFSET)

    excess = jnp.where(kth == imin, float(2 * seq), n_ge_lo - need_lo)
    any_excess = jnp.max(excess) > 0.0

    @pl.when(jnp.logical_not(any_excess))
    def _():
        def select(kb, _):
            k0 = pl.multiple_of(kb * t, t)
            bias_ref[pl.ds(k0, t), :] = jnp.where(key_ref[pl.ds(k0, t), :] >= kth, 0.0, NEG_BIG)
            return 0

        lax.fori_loop(0, n_kb, select, 0)

    @pl.when(any_excess)
    def _():
        r = lax.broadcasted_iota(I32, (t, t), 0)
        c = lax.broadcasted_iota(I32, (t, t), 1)
        later = jnp.where(c > r, 1.0, 0.0).astype(BF16)

        def select(step, tied_after):
            kb = n_kb - 1 - step
            k0 = pl.multiple_of(kb * t, t)
            kblk = key_ref[pl.ds(k0, t), :]
            tied = kblk == kth
            tied01 = jnp.where(tied, 1.0, 0.0).astype(BF16)
            after = _dot(later, tied01) + tied_after
            tie_bias = jnp.where(tied, jnp.where(after >= excess, 0.0, NEG_BIG), NEG_BIG)
            bias_ref[pl.ds(k0, t), :] = jnp.where(kblk > kth, 0.0, tie_bias)
            return after[0:1, :] + tied01[0:1, :].astype(F32)

        lax.fori_loop(0, n_kb, select, jnp.zeros((1, t), F32))

    q_t = [q_ref[:, p * LANES:(p + 1) * LANES].astype(F32).T for p in range(PAIR_W // LANES)]
    qh = [jnp.where(_head_row_mask(h % HEADS_PER_BLOCK), q_t[h // HEADS_PER_BLOCK], 0.0).astype(BF16)
          for h in range(N_DSA)]

    def scores(kb):
        k0 = pl.multiple_of(kb * t, t)
        bias = bias_ref[pl.ds(k0, t), :]
        return [_dot(k_ref[pl.ds(k0, t), (h // HEADS_PER_BLOCK) * LANES:(h // HEADS_PER_BLOCK + 1) * LANES], qh[h])
                + bias for h in range(N_DSA)]

    def values_t(kb):
        return [vt_ref[kb, h * V_ROWS:(h + 1) * V_ROWS, :] for h in range(N_DSA)]

    carry = _flash_blocks(n_kb, scores, values_t, tuple(_softmax_init(t) for _ in range(N_DSA)), s_ref)
    outs = [_softmax_result(st) for st in carry]
    outs += [jnp.zeros((HEAD_DIM, t), F32)] * (PAIR_W // HEAD_DIM - N_DSA)
    for p in range(PAIR_W // LANES):
        o_ref[:, p * LANES:(p + 1) * LANES] = _heads_to_token_rows(
            outs[p * HEADS_PER_BLOCK:(p + 1) * HEADS_PER_BLOCK])


def _dsa_attention(pb, ps, batch, seq):
    t = min(256, seq)
    nq = seq // t
    m = batch * seq
    topk = min(TOPK_MAX, seq // 4)
    return pl.pallas_call(
        functools.partial(_dsa_kernel, topk=topk, seq=seq),
        grid=(batch, nq),
        in_specs=[pl.BlockSpec((t, PAIR_W), lambda b, i: (b * nq + i, BLK_DSA_Q // 3)),
                  pl.BlockSpec((t, 4 * LANES), lambda b, i: (b * nq + i, BLK_IDX_Q // 4)),
                  pl.BlockSpec((t, LANES), lambda b, i: (b * nq + i, 0)),
                  pl.BlockSpec((seq, PAIR_W), lambda b, i: (b, BLK_DSA_K // 3)),
                  pl.BlockSpec((seq, PAIR_W), lambda b, i: (b, BLK_DSA_V // 3)),
                  pl.BlockSpec((seq, LANES), lambda b, i: (b, BLK_IDX_KA)),
                  pl.BlockSpec((seq, LANES), lambda b, i: (b, BLK_IDX_KB))],
        out_specs=pl.BlockSpec((t, PAIR_W), lambda b, i: (b * nq + i, 0)),
        out_shape=jax.ShapeDtypeStruct((m, PAIR_W), BF16),
        scratch_shapes=[pltpu.VMEM((seq, t), I32), pltpu.VMEM((seq + t, t), jnp.int16),
                        pltpu.VMEM((seq + t, t), jnp.int16),
                        pltpu.VMEM((seq, t), F32),
                        pltpu.VMEM((nq, N_DSA * V_ROWS, t), BF16), pltpu.VMEM((2, N_DSA, t, t), F32)],
        compiler_params=_params(2),
        name="dsa_attention",
    )(pb, pb, ps, pb, pb, pb, pb)


def _mixer_out_cross_kernel(x_ref, gain_ref, of_ref, od_ref, os_ref, wg_ref, wf_ref, wd_ref, ws_ref, wo_ref,
                            ca_gain_ref, ca_wq_ref, kv_ref, ca_wo_ref, o_ref):
    d = x_ref.shape[1]
    x = x_ref[...]
    h = _rms(x, gain_ref[...]).astype(BF16)
    merged = None
    for b, (o_b, w_b) in enumerate(((of_ref, wf_ref), (od_ref, wd_ref), (os_ref, ws_ref))):
        term = jax.nn.sigmoid(_dot(h, wg_ref[:, b * d:(b + 1) * d])) * _dot(o_b[...], w_b[...])
        merged = term if merged is None else merged + term
    x = x + _dot(merged.astype(BF16), wo_ref[...])

    q = _dot(_rms(x, ca_gain_ref[...]).astype(BF16), ca_wq_ref[...])
    scale = CA_HEAD_DIM ** -0.5
    outs = []
    for hd in range(N_CA_HEADS):
        qh = q[:, hd * CA_HEAD_DIM:(hd + 1) * CA_HEAD_DIM].astype(BF16)
        logits = _dot_nt(qh, kv_ref[:, hd * CA_HEAD_DIM:(hd + 1) * CA_HEAD_DIM]) * scale
        e = jnp.exp(logits - jnp.max(logits, axis=1, keepdims=True))
        pv = _dot(e.astype(BF16), kv_ref[:, CA_W + hd * CA_HEAD_DIM:CA_W + (hd + 1) * CA_HEAD_DIM])
        outs.append(pv / jnp.sum(e, axis=1, keepdims=True))
    o_ref[...] = x + _dot(jnp.concatenate(outs, axis=1).astype(BF16), ca_wo_ref[...])


def _mixer_out_cross(x2, gain, o_fox, o_dsa, o_sb, wg, wf, wd, ws, wo, ca_gain, ca_wq, kv, ca_wo, batch, seq):
    m, d = x2.shape
    tm = min(512, seq)
    n_mem = kv.shape[0] // batch
    per_batch = seq // tm
    row = lambda w: pl.BlockSpec((tm, w), lambda i: (i, 0))
    return pl.pallas_call(
        _mixer_out_cross_kernel,
        grid=(m // tm,),
        in_specs=[row(d), _resident((1, d)), row(FOX_W), row(PAIR_W), row(PAIR_W), _resident((d, N_BRANCH * d)),
                  _resident((FOX_W, d)), _resident((PAIR_W, d)), _resident((PAIR_W, d)), _resident((d, d)),
                  _resident((1, d)), _resident((d, CA_W)),
                  pl.BlockSpec((n_mem, 2 * CA_W), lambda i: (i // per_batch, 0)), _resident((CA_W, d))],
        out_specs=row(d),
        out_shape=jax.ShapeDtypeStruct((m, d), F32),
        compiler_params=_params(1),
        name="mixer_out_cross",
    )(x2, gain.reshape(1, d), o_fox, o_dsa, o_sb, wg, wf, wd, ws, wo, ca_gain.reshape(1, d), ca_wq, kv, ca_wo)


def _norm_matmul_kernel(x_ref, g_ref, w_ref, o_ref):
    o_ref[...] = _dot(_rms(x_ref[...], g_ref[...]).astype(BF16), w_ref[...]).astype(o_ref.dtype)


def _norm_matmul(x2, gain, w, out_dtype):
    m, d = x2.shape
    n = w.shape[1]
    tm = min(512, m)
    return pl.pallas_call(
        _norm_matmul_kernel,
        grid=(m // tm,),
        in_specs=[pl.BlockSpec((tm, d), lambda i: (i, 0)), _resident((1, d)), _resident((d, n))],
        out_specs=pl.BlockSpec((tm, n), lambda i: (i, 0)),
        out_shape=jax.ShapeDtypeStruct((m, n), out_dtype),
        compiler_params=_params(1),
        name="norm_matmul",
    )(x2, gain.reshape(1, d), w)


def _pad_cols(w, n):
    return jnp.pad(w, ((0, 0), (0, n - w.shape[1])))


def _pad_rows(w, n):
    return jnp.pad(w, ((0, n - w.shape[0]), (0, 0)))


def _mixer_weights(w_in):
    offs = np.cumsum((0,) + IN_SPLITS)
    part = lambda j: w_in[:, offs[j]:offs[j + 1]]
    thirds = lambda w: jnp.split(w, 3, axis=1)
    scale = HEAD_DIM ** -0.5
    fq, fk, fv = thirds(part(0))
    dq, dk, dv = thirds(part(2))
    sq, sk, sv = thirds(part(6))
    iq, ik, iw = part(3), part(4), part(5)
    zk = jnp.zeros_like(ik)
    small = jnp.concatenate([part(1), jnp.zeros((w_in.shape[0], SMALL_IDXW_LANE - N_FOX), w_in.dtype), iw], axis=1)
    wb = jnp.concatenate(
        [_pad_cols(dq * scale, PAIR_W), _pad_cols(dk, PAIR_W),
         jnp.concatenate([ik, zk], axis=1), jnp.concatenate([zk, ik], axis=1), iq,
         fq * scale, fk, fv, _pad_cols(dv, PAIR_W),
         _pad_cols(sq * scale, PAIR_W), _pad_cols(sk, PAIR_W), _pad_cols(sv, PAIR_W),
         _pad_cols(small, LANES)], axis=1).astype(BF16)
    return wb, part(7).astype(BF16)


def _mixer_and_cross(x2, cos, sin_signed, batch, seq, gain, w_in, b_fgate, w_fox_out, w_dsa_out, w_sb_out, w_out,
                     ca_gain, ca_wq, kv, ca_wo):
    wb, wg = _mixer_weights(w_in)
    bias_row = _pad_cols(b_fgate.reshape(1, N_FOX).astype(F32), LANES)
    pb, ps, k_side, q_side = _mixer_in(x2, gain, cos, sin_signed, wb, bias_row, batch, seq)

    o_fox = _fox_attention(pb, k_side, q_side, batch, seq)
    o_dsa = _dsa_attention(pb, ps, batch, seq)
    o_sb = _sb_attention(pb, batch, seq)
    return _mixer_out_cross(x2, gain, o_fox, o_dsa, o_sb, wg, w_fox_out.astype(BF16),
                            _pad_rows(w_dsa_out, PAIR_W).astype(BF16), _pad_rows(w_sb_out, PAIR_W).astype(BF16),
                            w_out.astype(BF16), ca_gain, ca_wq.astype(BF16), kv, ca_wo.astype(BF16), batch, seq)


def kernel(x, mem, positions, ffn1_norm, ffn1_w_gu, ffn1_w_down, mix_norm, w_in, b_fgate, w_fox_out, w_dsa_out,
           w_sb_out, w_out, ca_norm, mem_norm, ca_w_q, ca_w_kv, ca_w_o, ffn2_norm, ffn2_w_gu, ffn2_w_down,
           final_norm):
    batch, seq, d = x.shape
    depth = ffn1_norm.shape[0]
    x2 = x.reshape(batch * seq, d)
    mem2 = mem.reshape(-1, d)
    pos2 = positions.reshape(batch * seq, 1).astype(I32)
    half = HEAD_DIM // 2
    inv_freq = jnp.power(ROPE_THETA, -jnp.arange(half, dtype=F32) * (2.0 / HEAD_DIM))
    invf = jnp.tile(inv_freq, LANES // half).reshape(1, LANES)
    cos, sin_signed = _rope_tables(pos2, invf)
    for l in range(depth):
        x2 = _ffn(x2, ffn1_norm[l], ffn1_w_gu[l], ffn1_w_down[l])
        kv = _norm_matmul(mem2, mem_norm[l], ca_w_kv[l].astype(BF16), BF16)
        x2 = _mixer_and_cross(x2, cos, sin_signed, batch, seq, mix_norm[l], w_in[l], b_fgate[l], w_fox_out[l],
                              w_dsa_out[l], w_sb_out[l], w_out[l], ca_norm[l], ca_w_q[l], kv, ca_w_o[l])
        x2 = _ffn(x2, ffn2_norm[l], ffn2_w_gu[l], ffn2_w_down[l],
                  final_gain=final_norm if l == depth - 1 else None)
    return x2.reshape(batch, seq, d)
```

```python
import functools

import jax
import jax.numpy as jnp
import numpy as np
from jax import lax
from jax.experimental import pallas as pl
from jax.experimental.pallas import tpu as pltpu

F32 = jnp.float32
BF16 = jnp.bfloat16
I32 = jnp.int32

D_MODEL = 1024
HEAD_DIM = 64
N_FOX = 6
N_DSA = 5
N_SB = 5
N_IDX_HEADS = 8
IDX_DIM = 64
TOPK_MAX = 256
N_CA_HEADS = 4
CA_HEAD_DIM = 128
D_FF = 2816
ROPE_THETA = 10000.0
NORM_EPS = 1e-6
N_BRANCH = 3
HALF_STEP = 0.5
FOX_W = N_FOX * HEAD_DIM
DSA_W = N_DSA * HEAD_DIM
SB_W = N_SB * HEAD_DIM
CA_W = N_CA_HEADS * CA_HEAD_DIM
IN_SPLITS = (3 * FOX_W, N_FOX, 3 * DSA_W, N_IDX_HEADS * IDX_DIM, IDX_DIM, N_IDX_HEADS, 3 * SB_W,
             N_BRANCH * D_MODEL)

LANES = 128
HEADS_PER_BLOCK = LANES // HEAD_DIM
PAIR_W = 3 * LANES
VMEM_LIMIT = 56 * 2**20

BLK_DSA_Q, BLK_DSA_K = 0, 3
BLK_IDX_KA = 6
BLK_IDX_KB = 7
BLK_IDX_Q = 8
N_ROPE_BLOCKS = 12
BLK_FOX_Q, BLK_FOX_K, BLK_FOX_V = 12, 15, 18
BLK_DSA_V = 21
BLK_SB_Q, BLK_SB_K, BLK_SB_V = 24, 27, 30
N_PROJ_BLOCKS = 33
PROJ_CHUNK = 4 * LANES
SMALL_FGATE_LANE = 0
SMALL_IDXW_LANE = 8

NEG_BIG = -1e30
SB_CUTOFF = -110.0
INT_MIN = -2**31


def _params(n_grid):
    return pltpu.CompilerParams(dimension_semantics=("arbitrary",) * n_grid, vmem_limit_bytes=VMEM_LIMIT)


def _resident(shape):
    nd = len(shape)
    return pl.BlockSpec(shape, lambda *_: (0,) * nd, pipeline_mode=pl.Buffered(1))


def _rms(x, g):
    return x * lax.rsqrt(jnp.mean(x * x, axis=-1, keepdims=True) + NORM_EPS) * g


def _dot(a, b):
    return jnp.dot(a, b, preferred_element_type=F32)


def _dot_nt(a, b):
    return lax.dot_general(a, b, (((1,), (1,)), ((), ())), preferred_element_type=F32)


def _split3(x):
    hi = x.astype(BF16)
    r = x - hi.astype(F32)
    mid = r.astype(BF16)
    lo = (r - mid.astype(F32)).astype(BF16)
    return hi, mid, lo


def _log1p_exp_neg_abs(z):
    return jnp.log1p(jnp.exp(-jnp.abs(z)))


def _ffn_chunks():
    out, c = [], 0
    while c < D_FF:
        w = min(512, D_FF - c)
        out.append((c, w))
        c += w
    return tuple(out)


def _ffn_kernel(*refs, final):
    if final:
        x_ref, g_ref, wgu_ref, wd_ref, fn_ref, o_ref, a_ref = refs
    else:
        x_ref, g_ref, wgu_ref, wd_ref, o_ref, a_ref = refs
    x = x_ref[...]
    h = _rms(x, g_ref[...]).astype(BF16)
    for c0, w in _ffn_chunks():
        g = _dot(h, wgu_ref[:, c0:c0 + w])
        u = _dot(h, wgu_ref[:, D_FF + c0:D_FF + c0 + w])
        a_ref[:, c0:c0 + w] = (g * jax.nn.sigmoid(g) * u).astype(BF16)
    y = x + HALF_STEP * _dot(a_ref[...], wd_ref[...])
    if final:
        y = _rms(y, fn_ref[...])
    o_ref[...] = y


def _ffn(x2, gain, w_gu, w_down, final_gain=None):
    m, d = x2.shape
    tm = min(1024, m)
    final = final_gain is not None
    in_specs = [pl.BlockSpec((tm, d), lambda i: (i, 0)), _resident((1, d)),
                _resident((d, 2 * D_FF)), _resident((D_FF, d))]
    args = [x2, gain.reshape(1, d), w_gu.astype(BF16), w_down.astype(BF16)]
    if final:
        in_specs.append(_resident((1, d)))
        args.append(final_gain.reshape(1, d))
    return pl.pallas_call(
        functools.partial(_ffn_kernel, final=final),
        grid=(m // tm,),
        in_specs=in_specs,
        out_specs=pl.BlockSpec((tm, d), lambda i: (i, 0)),
        out_shape=jax.ShapeDtypeStruct((m, d), F32),
        scratch_shapes=[pltpu.VMEM((tm, D_FF), BF16)],
        compiler_params=_params(1),
        name="ffn",
    )(*args)


def _first_half_lanes():
    lane = lax.broadcasted_iota(I32, (1, LANES), 1)
    return (lane % HEAD_DIM) < (HEAD_DIM // 2)


def _rope_tables_kernel(pos_ref, invf_ref, cos_ref, sin_ref):
    ang = pos_ref[...].astype(F32) * invf_ref[...]
    sin = jnp.sin(ang)
    cos_ref[...] = jnp.cos(ang)
    sin_ref[...] = jnp.where(_first_half_lanes(), -sin, sin)


def _rope_tables(pos2, invf):
    m = pos2.shape[0]
    tm = min(512, m)
    tok = pl.BlockSpec((tm, LANES), lambda i: (i, 0))
    return pl.pallas_call(
        _rope_tables_kernel,
        grid=(m // tm,),
        in_specs=[pl.BlockSpec((tm, 1), lambda i: (i, 0)), _resident((1, LANES))],
        out_specs=[tok, tok],
        out_shape=[jax.ShapeDtypeStruct((m, LANES), F32)] * 2,
        compiler_params=_params(1),
        name="rope_tables",
    )(pos2, invf)


FOX_FEAT_ROWS = 16


def _fox_feature_maps():
    k_map = np.zeros((3, LANES, FOX_W), np.float32)
    q_map = np.zeros((3, LANES, LANES), np.float32)
    k_one = np.zeros((1, FOX_W), np.float32)
    q_one = np.zeros((1, LANES), np.float32)
    for h in range(N_FOX):
        p, hh = divmod(h, HEADS_PER_BLOCK)
        for j in range(3):
            k_map[j, h, p * LANES + 3 * hh + j] = -1.0
            q_map[j, h, FOX_FEAT_ROWS * h + 6 + j] = 1.0
            k_one[0, p * LANES + 6 + j] = 1.0
            q_one[0, FOX_FEAT_ROWS * h + 3 * hh + j] = 1.0
    return (jnp.asarray(k_map, BF16), jnp.asarray(q_map, BF16), jnp.asarray(k_one), jnp.asarray(q_one))


def _fox_decay_features(logits, b_ref, kmap_ref, qmap_ref, kone_ref, qone_ref, kf_ref, qf_ref, carry_ref):
    tc = logits.shape[0]
    z = logits + b_ref[...]
    log_f = jnp.minimum(z, 0.0) - _log1p_exp_neg_abs(z)
    row = lax.broadcasted_iota(I32, (tc, tc), 0)
    col = lax.broadcasted_iota(I32, (tc, tc), 1)
    tri = jnp.where(col <= row, 1.0, 0.0).astype(BF16)
    hi, mid, lo = _split3(log_f)
    yield
    cs = _dot(tri, hi) + _dot(tri, mid) + _dot(tri, lo) + carry_ref[...]
    carry_ref[...] = cs[tc - 1:tc, :]
    terms = _split3(cs)
    yield
    kf = kone_ref[...]
    qf = qone_ref[...]
    for j in range(3):
        kf = kf + _dot(terms[j], kmap_ref[j])
        qf = qf + _dot(terms[j], qmap_ref[j])
    kf_ref[...] = kf.astype(BF16)
    qf_ref[...] = qf.astype(BF16)


def _mixer_in_kernel(x_ref, g_ref, cos_ref, sin_ref, wb_ref, b_ref, kmap_ref, qmap_ref, kone_ref, qone_ref,
                     pb_ref, ps_ref, kf_ref, qf_ref, carry_ref, *, tiles_per_seq):
    @pl.when(pl.program_id(0) % tiles_per_seq == 0)
    def _():
        carry_ref[...] = jnp.zeros_like(carry_ref)

    h = _rms(x_ref[...], g_ref[...]).astype(BF16)
    cos = cos_ref[...]
    sin_signed = sin_ref[...]
    first_half = _first_half_lanes()
    n_cols = (N_PROJ_BLOCKS + 1) * LANES
    starts = list(range(0, n_cols, PROJ_CHUNK))
    decay_stages = iter(())
    for c0 in starts[-1:] + starts[:-1]:
        c1 = min(c0 + PROJ_CHUNK, n_cols)
        res = _dot(h, wb_ref[:, c0:c1])
        next(decay_stages, None)
        if c1 == n_cols:
            small = res[:, c1 - c0 - LANES:]
            ps_ref[...] = small
            decay_stages = _fox_decay_features(small, b_ref, kmap_ref, qmap_ref, kone_ref, qone_ref, kf_ref,
                                               qf_ref, carry_ref)
            next(decay_stages)
            c1 -= LANES
            res = res[:, :c1 - c0]
        if c0 < N_ROPE_BLOCKS * LANES:
            parts = []
            for b in range((c1 - c0) // LANES):
                xb = res[:, b * LANES:(b + 1) * LANES]
                partner = jnp.where(first_half, pltpu.roll(xb, LANES - HEAD_DIM // 2, 1),
                                    pltpu.roll(xb, HEAD_DIM // 2, 1))
                parts.append(xb * cos + partner * sin_signed)
            res = jnp.concatenate(parts, axis=1)
        pb_ref[:, c0:c1] = res.astype(BF16)


def _mixer_in(x2, gain, cos, sin_signed, wb, bias_row, batch, seq):
    m, d = x2.shape
    tm = min(512, seq)
    nb = N_PROJ_BLOCKS * LANES
    tok = lambda w: pl.BlockSpec((tm, w), lambda i: (i, 0))
    pb, ps, k_side, q_tok = pl.pallas_call(
        functools.partial(_mixer_in_kernel, tiles_per_seq=seq // tm),
        grid=(m // tm,),
        in_specs=[tok(d), _resident((1, d)), tok(LANES), tok(LANES), _resident((d, nb + LANES)),
                  _resident((1, LANES)), _resident((3, LANES, FOX_W)), _resident((3, LANES, LANES)),
                  _resident((1, FOX_W)), _resident((1, LANES))],
        out_specs=[tok(nb), tok(LANES), tok(FOX_W), tok(LANES)],
        out_shape=[jax.ShapeDtypeStruct((m, nb), BF16), jax.ShapeDtypeStruct((m, LANES), F32),
                   jax.ShapeDtypeStruct((m, FOX_W), BF16), jax.ShapeDtypeStruct((m, LANES), BF16)],
        scratch_shapes=[pltpu.VMEM((1, LANES), F32)],
        compiler_params=_params(1),
        name="mixer_in",
    )(x2, gain.reshape(1, d), cos, sin_signed, wb, bias_row, *_fox_feature_maps())
    q_side = q_tok[:, :N_FOX * FOX_FEAT_ROWS].reshape(batch, seq, N_FOX, FOX_FEAT_ROWS).transpose(0, 2, 3, 1)
    return pb, ps, k_side, q_side


def _head_lane_mask(hh):
    lane = lax.broadcasted_iota(I32, (1, LANES), 1)
    return (lane < HEAD_DIM) if hh == 0 else (lane >= HEAD_DIM)


def _head_row_mask(hh):
    row = lax.broadcasted_iota(I32, (LANES, 1), 0)
    return (row < HEAD_DIM) if hh == 0 else (row >= HEAD_DIM)


def _transpose_bf16(x):
    return x.astype(F32).T.astype(BF16)


def _fill_transposed(src_ref, dst_ref):
    n_blocks, _, tk = dst_ref.shape
    for j in range(n_blocks):
        dst_ref[j] = _transpose_bf16(src_ref[j * tk:(j + 1) * tk, :])


V_ROWS = HEAD_DIM + 16


def _fill_transposed_with_ones(src_ref, dst_ref, n_heads):
    n_blocks, _, tk = dst_ref.shape
    ones_pad = jnp.where(lax.broadcasted_iota(I32, (V_ROWS - HEAD_DIM, tk), 0) == 0, 1.0, 0.0).astype(BF16)
    for j in range(n_blocks):
        v_t = _transpose_bf16(src_ref[j * tk:(j + 1) * tk, :])
        for h in range(n_heads):
            dst_ref[j, h * V_ROWS:h * V_ROWS + HEAD_DIM, :] = v_t[h * HEAD_DIM:(h + 1) * HEAD_DIM, :]
            dst_ref[j, h * V_ROWS + HEAD_DIM:(h + 1) * V_ROWS, :] = ones_pad


def _softmax_steps(carry, scores, values_t):
    stats = []
    for (m, _), s in zip(carry, scores):
        m_new = jnp.maximum(m, jnp.max(s, axis=0, keepdims=True))
        stats.append((m_new, jnp.exp(m - m_new), jnp.exp(s - m_new).astype(BF16)))
    return tuple((m_new, alpha * acc + _dot(v_t, p))
                 for (m_new, alpha, p), (_, acc), v_t in zip(stats, carry, values_t))


def _softmax_init(tq):
    return (jnp.full((1, tq), NEG_BIG, F32), jnp.zeros((V_ROWS, tq), F32))


def _softmax_result(state):
    _, acc = state
    return acc[:HEAD_DIM, :] / acc[HEAD_DIM:HEAD_DIM + 1, :]


def _flash_blocks(n, score_fn, value_fn, carry, s_ref):
    n_heads = len(carry)

    def load(slot):
        return [s_ref[slot, h] for h in range(n_heads)]

    def store(slot, kb):
        for h, s in enumerate(score_fn(kb)):
            s_ref[slot, h] = s

    odd = n % 2
    carry = lax.cond(odd == 1, lambda c: _softmax_steps(c, score_fn(0), value_fn(0)), lambda c: c, carry)

    @pl.when(n >= 2)
    def _():
        store(0, odd)

    def body(j, c):
        kb = odd + 2 * j
        store(1, kb + 1)
        c = _softmax_steps(c, load(0), value_fn(kb))
        store(0, jnp.minimum(kb + 2, n - 1))
        return _softmax_steps(c, load(1), value_fn(kb + 1))

    return lax.fori_loop(0, n // 2, body, carry)


def _heads_to_token_rows(out_t):
    return jnp.concatenate(out_t, axis=0).T.astype(BF16)


def _fox_kernel(q_ref, k_ref, v_ref, kf_ref, qf_ref, o_ref, vt_ref, s_ref):
    tq = q_ref.shape[0]
    tk = vt_ref.shape[2]
    i = pl.program_id(1)

    @pl.when(i == 0)
    def _():
        _fill_transposed_with_ones(v_ref, vt_ref, N_FOX)

    key_pos = lax.broadcasted_iota(I32, (tk, tq), 0)
    qry_pos = lax.broadcasted_iota(I32, (tk, tq), 1)
    causal = [key_pos + d * tk <= qry_pos for d in range(2)]
    pad = jnp.zeros((LANES - FOX_FEAT_ROWS, tq), BF16)
    qx = []
    for p in range(FOX_W // LANES):
        q_t = q_ref[:, p * LANES:(p + 1) * LANES].astype(F32).T
        for hh in range(HEADS_PER_BLOCK):
            qx.append(jnp.concatenate([jnp.where(_head_row_mask(hh), q_t, 0.0).astype(BF16),
                                       qf_ref[0, p * HEADS_PER_BLOCK + hh], pad], axis=0))

    def store(slot, kb):
        k0 = pl.multiple_of(kb * tk, tk)
        for h in range(N_FOX):
            p = h // HEADS_PER_BLOCK
            kx = jnp.concatenate([k_ref[pl.ds(k0, tk), p * LANES:(p + 1) * LANES],
                                  kf_ref[pl.ds(k0, tk), p * LANES:(p + 1) * LANES]], axis=1)
            s_ref[slot, h] = _dot(kx, qx[h])

    def step(carry, slot, kb, mask=None):
        scores = [s_ref[slot, h] for h in range(N_FOX)]
        if mask is not None:
            scores = [jnp.where(mask, s, NEG_BIG) for s in scores]
        return _softmax_steps(carry, scores, [vt_ref[kb, h * V_ROWS:(h + 1) * V_ROWS, :] for h in range(N_FOX)])

    def trip(j, carry):
        kb = 2 * j
        store(1, kb + 1)
        carry = step(carry, 0, kb)
        store(0, kb + 2)
        return step(carry, 1, kb + 1)

    store(0, 0)
    carry = lax.fori_loop(0, i, trip, tuple(_softmax_init(tq) for _ in range(N_FOX)))
    kb = 2 * i
    store(1, kb + 1)
    carry = step(carry, 0, kb, causal[0])
    carry = step(carry, 1, kb + 1, causal[1])
    for p in range(FOX_W // LANES):
        o_ref[:, p * LANES:(p + 1) * LANES] = _heads_to_token_rows(
            [_softmax_result(st) for st in carry[p * HEADS_PER_BLOCK:(p + 1) * HEADS_PER_BLOCK]])


def _fox_attention(pb, k_side, q_side, batch, seq):
    tq = min(512, seq)
    tk = tq // 2
    nq = seq // tq
    m = batch * seq
    return pl.pallas_call(
        _fox_kernel,
        grid=(batch, nq),
        in_specs=[pl.BlockSpec((tq, FOX_W), lambda b, i: (b * nq + i, BLK_FOX_Q // 3)),
                  pl.BlockSpec((seq, FOX_W), lambda b, i: (b, BLK_FOX_K // 3)),
                  pl.BlockSpec((seq, FOX_W), lambda b, i: (b, BLK_FOX_V // 3)),
                  pl.BlockSpec((seq, FOX_W), lambda b, i: (b, 0)),
                  pl.BlockSpec((1, N_FOX, FOX_FEAT_ROWS, tq), lambda b, i: (b, 0, 0, i))],
        out_specs=pl.BlockSpec((tq, FOX_W), lambda b, i: (b * nq + i, 0)),
        out_shape=jax.ShapeDtypeStruct((m, FOX_W), BF16),
        scratch_shapes=[pltpu.VMEM((seq // tk, N_FOX * V_ROWS, tk), BF16), pltpu.VMEM((2, N_FOX, tk, tq), F32)],
        compiler_params=_params(2),
        name="fox_attention",
    )(pb, pb, pb, k_side, q_side)


def _sb_kernel(q_ref, k_ref, v_ref, o_ref, vt_ref):
    tq = q_ref.shape[0]
    tk = vt_ref.shape[2]
    i = pl.program_id(1)

    @pl.when(i == 0)
    def _():
        _fill_transposed(v_ref, vt_ref)

    heads = range(N_SB)
    q_t = [q_ref[:, p * LANES:(p + 1) * LANES].astype(F32).T for p in range(PAIR_W // LANES)]
    qh = [jnp.where(_head_row_mask(h % HEADS_PER_BLOCK), q_t[h // HEADS_PER_BLOCK], 0.0).astype(BF16) for h in heads]
    r = lax.broadcasted_iota(I32, (tk, tk), 0)
    c = lax.broadcasted_iota(I32, (tk, tk), 1)
    upper = jnp.where(c > r, 1.0, 0.0).astype(BF16)
    key_pos = lax.broadcasted_iota(I32, (tk, tq), 0)
    qry_pos = lax.broadcasted_iota(I32, (tk, tq), 1) + i * tq

    def process(kb, state, masked):
        k0 = pl.multiple_of(kb * tk, tk)
        strict = key_pos + k0 < qry_pos
        z = [_dot(k_ref[pl.ds(k0, tk), (h // HEADS_PER_BLOCK) * LANES:(h // HEADS_PER_BLOCK + 1) * LANES], qh[h])
             for h in heads]
        log_beta, log_1m = [], []
        for zh in z:
            l1m = jnp.minimum(-zh, 0.0) - jnp.log(1.0 + jnp.exp(-jnp.abs(zh)))
            log_beta.append(l1m + zh)
            log_1m.append(jnp.where(strict, l1m, 0.0) if masked else l1m)
        after = []
        for h in heads:
            hi, mid, lo = _split3(log_1m[h])
            after.append(_dot(upper, hi) + _dot(upper, mid) + _dot(upper, lo) + state[h][0])
        weights = []
        for h in heads:
            a = jnp.exp(log_beta[h] + after[h])
            weights.append((jnp.where(strict, a, 0.0) if masked else a).astype(BF16))
        return tuple((state[h][0] + jnp.sum(log_1m[h], axis=0, keepdims=True),
                      state[h][1] + _dot(vt_ref[kb, h * HEAD_DIM:(h + 1) * HEAD_DIM, :], weights[h]))
                     for h in heads)

    def run_max(state):
        m = state[0][0]
        for run, _ in state[1:]:
            m = jnp.maximum(m, run)
        return jnp.max(m)

    state = tuple((jnp.zeros((1, tq), F32), jnp.zeros((HEAD_DIM, tq), F32)) for _ in heads)
    n_diag = tq // tk
    for d in range(n_diag):
        state = process((i + 1) * n_diag - 1 - d, state, True)

    def cond(carry):
        kb, worst, _ = carry
        return jnp.logical_and(kb >= 0, worst > SB_CUTOFF)

    def body(carry):
        kb, _, st = carry
        st = process(kb, st, False)
        return kb - 1, run_max(st), st

    _, _, state = lax.while_loop(cond, body, (i * n_diag - 1, run_max(state), state))
    for p in range(PAIR_W // LANES):
        outs = [state[h][1] for h in heads if h // HEADS_PER_BLOCK == p]
        outs += [jnp.zeros((HEAD_DIM, tq), F32)] * (HEADS_PER_BLOCK - len(outs))
        o_ref[:, p * LANES:(p + 1) * LANES] = _heads_to_token_rows(outs)


def _sb_attention(pb, batch, seq):
    tq = min(256, seq)
    tk = min(256, seq)
    nq = seq // tq
    m = batch * seq
    return pl.pallas_call(
        _sb_kernel,
        grid=(batch, nq),
        in_specs=[pl.BlockSpec((tq, PAIR_W), lambda b, i: (b * nq + i, BLK_SB_Q // 3)),
                  pl.BlockSpec((seq, PAIR_W), lambda b, i: (b, BLK_SB_K // 3)),
                  pl.BlockSpec((seq, PAIR_W), lambda b, i: (b, BLK_SB_V // 3))],
        out_specs=pl.BlockSpec((tq, PAIR_W), lambda b, i: (b * nq + i, 0)),
        out_shape=jax.ShapeDtypeStruct((m, PAIR_W), BF16),
        scratch_shapes=[pltpu.VMEM((seq // tk, PAIR_W, tk), BF16)],
        compiler_params=_params(2),
        name="sb_attention",
    )(pb, pb, pb)


HALF_BITS = 16
HALF_OFFSET = 1 << (HALF_BITS - 1)


def _dsa_kernel(q_ref, iq_ref, w_ref, k_ref, v_ref, ka_ref, kb_ref, o_ref, key_ref, hi_ref, lo_ref, bias_ref,
                vt_ref, s_ref, *, topk, seq):
    t = q_ref.shape[0]
    i = pl.program_id(1)

    @pl.when(i == 0)
    def _():
        _fill_transposed_with_ones(v_ref, vt_ref, N_DSA)

    n_kb = i + 1
    kf = float(topk)
    imin = jnp.int32(INT_MIN)
    key_pos = lax.broadcasted_iota(I32, (t, t), 0)
    qry_pos = lax.broadcasted_iota(I32, (t, t), 1) + i * t

    iq_t = [_transpose_bf16(iq_ref[:, g * LANES:(g + 1) * LANES]) for g in range(N_IDX_HEADS // 2)]
    w_t = (w_ref[...] * (N_IDX_HEADS ** -0.5) * (IDX_DIM ** -0.5)).T

    def index_block(kb):
        k0 = pl.multiple_of(kb * t, t)
        ka = ka_ref[pl.ds(k0, t), :]
        kb_ = kb_ref[pl.ds(k0, t), :]
        score = jnp.zeros((t, t), F32)
        for g in range(N_IDX_HEADS // 2):
            lo = SMALL_IDXW_LANE + 2 * g
            score = score + jnp.maximum(_dot(ka, iq_t[g]), 0.0) * w_t[lo:lo + 1, :]
            score = score + jnp.maximum(_dot(kb_, iq_t[g]), 0.0) * w_t[lo + 1:lo + 2, :]
        bits = lax.bitcast_convert_type(score + 0.0, I32)
        key = jnp.where(bits < 0, bits ^ jnp.int32(0x7FFFFFFF), bits)
        key = jnp.where(key_pos + k0 <= qry_pos, key, imin)
        key_ref[pl.ds(k0, t), :] = key
        hi_ref[pl.ds(k0, t), :] = lax.shift_right_arithmetic(key, HALF_BITS).astype(jnp.int16)
        lo_ref[pl.ds(k0, t), :] = ((key & (2 * HALF_OFFSET - 1)) - HALF_OFFSET).astype(jnp.int16)

    odd = n_kb % 2

    @pl.when(odd == 1)
    def _():
        index_block(0)

    def index_pair(j, _):
        index_block(odd + 2 * j)
        index_block(odd + 2 * j + 1)
        return 0

    lax.fori_loop(0, n_kb // 2, index_pair, 0)

    masked16 = jnp.full((t, t), -HALF_OFFSET, jnp.int16)

    @pl.when(odd == 1)
    def _():
        k0 = pl.multiple_of(n_kb * t, t)
        hi_ref[pl.ds(k0, t), :] = masked16
        lo_ref[pl.ds(k0, t), :] = masked16

    n_scan = (n_kb + 1) // 2
    scan_rows = 2 * t

    def count16(ref, pred):
        rows = 16
        def body(trip, acc):
            k0 = pl.multiple_of(trip * scan_rows, scan_rows)
            ind = pred(ref[pl.ds(k0, scan_rows), :])
            parts = [ind[r:r + rows, :] for r in range(0, scan_rows, rows)]
            while len(parts) > 1:
                parts = [parts[j] + parts[j + 1] for j in range(0, len(parts), 2)]
            return acc + parts[0]
        acc = lax.fori_loop(0, n_scan, body, jnp.zeros((rows, t), jnp.int16))
        return jnp.sum(acc.astype(F32), axis=0, keepdims=True)

    one16, zero16 = jnp.int16(1), jnp.int16(0)

    def kth_half(ref, need, n_all):
        def bit_step(it, state):
            prefix, cnt_prefix = state
            cand_u = prefix | lax.shift_left(jnp.int32(1), HALF_BITS - 1 - it)
            cand = (cand_u - HALF_OFFSET).astype(jnp.int16)
            cnt = count16(ref, lambda blk: jnp.where(blk >= cand, one16, zero16))
            take = cnt >= need
            return jnp.where(take, cand_u, prefix), jnp.where(take, cnt, cnt_prefix)
        prefix, cnt = lax.fori_loop(0, HALF_BITS, bit_step, (jnp.zeros((1, t), I32), n_all))
        return prefix - HALF_OFFSET, cnt


    n_scanned = (n_scan * scan_rows).astype(F32)
    kth_hi, n_ge_hi = kth_half(hi_ref, kf, jnp.full((1, t), n_scanned, F32))
    kth_hi16 = kth_hi.astype(jnp.int16)

    def above_and_bucket(trip, acc):
        rows = 16
        k0 = pl.multiple_of(trip * scan_rows, scan_rows)
        hi = hi_ref[pl.ds(k0, scan_rows), :]
        lo_ref[pl.ds(k0, scan_rows), :] = jnp.where(hi == kth_hi16, lo_ref[pl.ds(k0, scan_rows), :],
                                                    jnp.int16(-HALF_OFFSET))
        ind = jnp.where(hi > kth_hi16, one16, zero16)
        parts = [ind[r:r + rows, :] for r in range(0, scan_rows, rows)]
        while len(parts) > 1:
            parts = [parts[j] + parts[j + 1] for j in range(0, len(parts), 2)]
        return acc + parts[0]

    n_gt_hi = jnp.sum(lax.fori_loop(0, n_scan, above_and_bucket, jnp.zeros((16, t), jnp.int16)).astype(F32),
                      axis=0, keepdims=True)
    need_lo = kf - n_gt_hi
    kth_lo, n_ge_lo = kth_half(lo_ref, need_lo, n_ge_hi - n_gt_hi)
    kth = lax.shift_left(kth_hi, HALF_BITS) + (kth_lo + HALF_OFFSET)

    excess = jnp.where(kth == imin, float(2 * seq), n_ge_lo - need_lo)
    any_excess = jnp.max(excess) > 0.0

    @pl.when(jnp.logical_not(any_excess))
    def _():
        def select(kb, _):
            k0 = pl.multiple_of(kb * t, t)
            bias_ref[pl.ds(k0, t), :] = jnp.where(key_ref[pl.ds(k0, t), :] >= kth, 0.0, NEG_BIG)
            return 0

        lax.fori_loop(0, n_kb, select, 0)

    @pl.when(any_excess)
    def _():
        r = lax.broadcasted_iota(I32, (t, t), 0)
        c = lax.broadcasted_iota(I32, (t, t), 1)
        later = jnp.where(c > r, 1.0, 0.0).astype(BF16)

        def select(step, tied_after):
            kb = n_kb - 1 - step
            k0 = pl.multiple_of(kb * t, t)
            kblk = key_ref[pl.ds(k0, t), :]
            tied = kblk == kth
            tied01 = jnp.where(tied, 1.0, 0.0).astype(BF16)
            after = _dot(later, tied01) + tied_after
            tie_bias = jnp.where(tied, jnp.where(after >= excess, 0.0, NEG_BIG), NEG_BIG)
            bias_ref[pl.ds(k0, t), :] = jnp.where(kblk > kth, 0.0, tie_bias)
            return after[0:1, :] + tied01[0:1, :].astype(F32)

        lax.fori_loop(0, n_kb, select, jnp.zeros((1, t), F32))

    q_t = [q_ref[:, p * LANES:(p + 1) * LANES].astype(F32).T for p in range(PAIR_W // LANES)]
    qh = [jnp.where(_head_row_mask(h % HEADS_PER_BLOCK), q_t[h // HEADS_PER_BLOCK], 0.0).astype(BF16)
          for h in range(N_DSA)]

    def scores(kb):
        k0 = pl.multiple_of(kb * t, t)
        bias = bias_ref[pl.ds(k0, t), :]
        return [_dot(k_ref[pl.ds(k0, t), (h // HEADS_PER_BLOCK) * LANES:(h // HEADS_PER_BLOCK + 1) * LANES], qh[h])
                + bias for h in range(N_DSA)]

    def values_t(kb):
        return [vt_ref[kb, h * V_ROWS:(h + 1) * V_ROWS, :] for h in range(N_DSA)]

    carry = _flash_blocks(n_kb, scores, values_t, tuple(_softmax_init(t) for _ in range(N_DSA)), s_ref)
    outs = [_softmax_result(st) for st in carry]
    outs += [jnp.zeros((HEAD_DIM, t), F32)] * (PAIR_W // HEAD_DIM - N_DSA)
    for p in range(PAIR_W // LANES):
        o_ref[:, p * LANES:(p + 1) * LANES] = _heads_to_token_rows(
            outs[p * HEADS_PER_BLOCK:(p + 1) * HEADS_PER_BLOCK])


def _dsa_attention(pb, ps, batch, seq):
    t = min(256, seq)
    nq = seq // t
    m = batch * seq
    topk = min(TOPK_MAX, seq // 4)
    return pl.pallas_call(
        functools.partial(_dsa_kernel, topk=topk, seq=seq),
        grid=(batch, nq),
        in_specs=[pl.BlockSpec((t, PAIR_W), lambda b, i: (b * nq + i, BLK_DSA_Q // 3)),
                  pl.BlockSpec((t, 4 * LANES), lambda b, i: (b * nq + i, BLK_IDX_Q // 4)),
                  pl.BlockSpec((t, LANES), lambda b, i: (b * nq + i, 0)),
                  pl.BlockSpec((seq, PAIR_W), lambda b, i: (b, BLK_DSA_K // 3)),
                  pl.BlockSpec((seq, PAIR_W), lambda b, i: (b, BLK_DSA_V // 3)),
                  pl.BlockSpec((seq, LANES), lambda b, i: (b, BLK_IDX_KA)),
                  pl.BlockSpec((seq, LANES), lambda b, i: (b, BLK_IDX_KB))],
        out_specs=pl.BlockSpec((t, PAIR_W), lambda b, i: (b * nq + i, 0)),
        out_shape=jax.ShapeDtypeStruct((m, PAIR_W), BF16),
        scratch_shapes=[pltpu.VMEM((seq, t), I32), pltpu.VMEM((seq + t, t), jnp.int16),
                        pltpu.VMEM((seq + t, t), jnp.int16),
                        pltpu.VMEM((seq, t), F32),
                        pltpu.VMEM((nq, N_DSA * V_ROWS, t), BF16), pltpu.VMEM((2, N_DSA, t, t), F32)],
        compiler_params=_params(2),
        name="dsa_attention",
    )(pb, pb, ps, pb, pb, pb, pb)


def _mixer_out_cross_kernel(x_ref, gain_ref, of_ref, od_ref, os_ref, wg_ref, wf_ref, wd_ref, ws_ref, wo_ref,
                            ca_gain_ref, ca_wq_ref, kv_ref, ca_wo_ref, o_ref):
    d = x_ref.shape[1]
    x = x_ref[...]
    h = _rms(x, gain_ref[...]).astype(BF16)
    merged = None
    for b, (o_b, w_b) in enumerate(((of_ref, wf_ref), (od_ref, wd_ref), (os_ref, ws_ref))):
        term = jax.nn.sigmoid(_dot(h, wg_ref[:, b * d:(b + 1) * d])) * _dot(o_b[...], w_b[...])
        merged = term if merged is None else merged + term
    x = x + _dot(merged.astype(BF16), wo_ref[...])

    q = _dot(_rms(x, ca_gain_ref[...]).astype(BF16), ca_wq_ref[...])
    scale = CA_HEAD_DIM ** -0.5
    outs = []
    for hd in range(N_CA_HEADS):
        qh = q[:, hd * CA_HEAD_DIM:(hd + 1) * CA_HEAD_DIM].astype(BF16)
        logits = _dot_nt(qh, kv_ref[:, hd * CA_HEAD_DIM:(hd + 1) * CA_HEAD_DIM]) * scale
        e = jnp.exp(logits - jnp.max(logits, axis=1, keepdims=True))
        pv = _dot(e.astype(BF16), kv_ref[:, CA_W + hd * CA_HEAD_DIM:CA_W + (hd + 1) * CA_HEAD_DIM])
        outs.append(pv / jnp.sum(e, axis=1, keepdims=True))
    o_ref[...] = x + _dot(jnp.concatenate(outs, axis=1).astype(BF16), ca_wo_ref[...])


def _mixer_out_cross(x2, gain, o_fox, o_dsa, o_sb, wg, wf, wd, ws, wo, ca_gain, ca_wq, kv, ca_wo, batch, seq):
    m, d = x2.shape
    tm = min(1024, seq)
    n_mem = kv.shape[0] // batch
    per_batch = seq // tm
    row = lambda w: pl.BlockSpec((tm, w), lambda i: (i, 0))
    return pl.pallas_call(
        _mixer_out_cross_kernel,
        grid=(m // tm,),
        in_specs=[row(d), _resident((1, d)), row(FOX_W), row(PAIR_W), row(PAIR_W), _resident((d, N_BRANCH * d)),
                  _resident((FOX_W, d)), _resident((PAIR_W, d)), _resident((PAIR_W, d)), _resident((d, d)),
                  _resident((1, d)), _resident((d, CA_W)),
                  pl.BlockSpec((n_mem, 2 * CA_W), lambda i: (i // per_batch, 0)), _resident((CA_W, d))],
        out_specs=row(d),
        out_shape=jax.ShapeDtypeStruct((m, d), F32),
        compiler_params=_params(1),
        name="mixer_out_cross",
    )(x2, gain.reshape(1, d), o_fox, o_dsa, o_sb, wg, wf, wd, ws, wo, ca_gain.reshape(1, d), ca_wq, kv, ca_wo)


def _norm_matmul_kernel(x_ref, g_ref, w_ref, o_ref):
    o_ref[...] = _dot(_rms(x_ref[...], g_ref[...]).astype(BF16), w_ref[...]).astype(o_ref.dtype)


def _norm_matmul(x2, gain, w, out_dtype):
    m, d = x2.shape
    n = w.shape[1]
    tm = min(512, m)
    return pl.pallas_call(
        _norm_matmul_kernel,
        grid=(m // tm,),
        in_specs=[pl.BlockSpec((tm, d), lambda i: (i, 0)), _resident((1, d)), _resident((d, n))],
        out_specs=pl.BlockSpec((tm, n), lambda i: (i, 0)),
        out_shape=jax.ShapeDtypeStruct((m, n), out_dtype),
        compiler_params=_params(1),
        name="norm_matmul",
    )(x2, gain.reshape(1, d), w)


def _pad_cols(w, n):
    return jnp.pad(w, ((0, 0), (0, n - w.shape[1])))


def _pad_rows(w, n):
    return jnp.pad(w, ((0, n - w.shape[0]), (0, 0)))


def _mixer_weights(w_in):
    offs = np.cumsum((0,) + IN_SPLITS)
    part = lambda j: w_in[:, offs[j]:offs[j + 1]]
    thirds = lambda w: jnp.split(w, 3, axis=1)
    scale = HEAD_DIM ** -0.5
    fq, fk, fv = thirds(part(0))
    dq, dk, dv = thirds(part(2))
    sq, sk, sv = thirds(part(6))
    iq, ik, iw = part(3), part(4), part(5)
    zk = jnp.zeros_like(ik)
    small = jnp.concatenate([part(1), jnp.zeros((w_in.shape[0], SMALL_IDXW_LANE - N_FOX), w_in.dtype), iw], axis=1)
    wb = jnp.concatenate(
        [_pad_cols(dq * scale, PAIR_W), _pad_cols(dk, PAIR_W),
         jnp.concatenate([ik, zk], axis=1), jnp.concatenate([zk, ik], axis=1), iq,
         fq * scale, fk, fv, _pad_cols(dv, PAIR_W),
         _pad_cols(sq * scale, PAIR_W), _pad_cols(sk, PAIR_W), _pad_cols(sv, PAIR_W),
         _pad_cols(small, LANES)], axis=1).astype(BF16)
    return wb, part(7).astype(BF16)


def _mixer_and_cross(x2, cos, sin_signed, batch, seq, gain, w_in, b_fgate, w_fox_out, w_dsa_out, w_sb_out, w_out,
                     ca_gain, ca_wq, kv, ca_wo):
    wb, wg = _mixer_weights(w_in)
    bias_row = _pad_cols(b_fgate.reshape(1, N_FOX).astype(F32), LANES)
    pb, ps, k_side, q_side = _mixer_in(x2, gain, cos, sin_signed, wb, bias_row, batch, seq)

    o_fox = _fox_attention(pb, k_side, q_side, batch, seq)
    o_dsa = _dsa_attention(pb, ps, batch, seq)
    o_sb = _sb_attention(pb, batch, seq)
    return _mixer_out_cross(x2, gain, o_fox, o_dsa, o_sb, wg, w_fox_out.astype(BF16),
                            _pad_rows(w_dsa_out, PAIR_W).astype(BF16), _pad_rows(w_sb_out, PAIR_W).astype(BF16),
                            w_out.astype(BF16), ca_gain, ca_wq.astype(BF16), kv, ca_wo.astype(BF16), batch, seq)


def kernel(x, mem, positions, ffn1_norm, ffn1_w_gu, ffn1_w_down, mix_norm, w_in, b_fgate, w_fox_out, w_dsa_out,
           w_sb_out, w_out, ca_norm, mem_norm, ca_w_q, ca_w_kv, ca_w_o, ffn2_norm, ffn2_w_gu, ffn2_w_down,
           final_norm):
    batch, seq, d = x.shape
    depth = ffn1_norm.shape[0]
    x2 = x.reshape(batch * seq, d)
    mem2 = mem.reshape(-1, d)
    pos2 = positions.reshape(batch * seq, 1).astype(I32)
    half = HEAD_DIM // 2
    inv_freq = jnp.power(ROPE_THETA, -jnp.arange(half, dtype=F32) * (2.0 / HEAD_DIM))
    invf = jnp.tile(inv_freq, LANES // half).reshape(1, LANES)
    cos, sin_signed = _rope_tables(pos2, invf)
    for l in range(depth):
        x2 = _ffn(x2, ffn1_norm[l], ffn1_w_gu[l], ffn1_w_down[l])
        kv = _norm_matmul(mem2, mem_norm[l], ca_w_kv[l].astype(BF16), BF16)
        x2 = _mixer_and_cross(x2, cos, sin_signed, batch, seq, mix_norm[l], w_in[l], b_fgate[l], w_fox_out[l],
                              w_dsa_out[l], w_sb_out[l], w_out[l], ca_norm[l], ca_w_q[l], kv, ca_w_o[l])
        x2 = _ffn(x2, ffn2_norm[l], ffn2_w_gu[l], ffn2_w_down[l],
                  final_gain=final_norm if l == depth - 1 else None)
    return x2.reshape(batch, seq, d)
```

```python
import functools

import jax
import jax.numpy as jnp
import numpy as np
from jax import lax
from jax.experimental import pallas as pl
from jax.experimental.pallas import tpu as pltpu

F32 = jnp.float32
BF16 = jnp.bfloat16
I32 = jnp.int32

D_MODEL = 1024
HEAD_DIM = 64
N_FOX = 6
N_DSA = 5
N_SB = 5
N_IDX_HEADS = 8
IDX_DIM = 64
TOPK_MAX = 256
N_CA_HEADS = 4
CA_HEAD_DIM = 128
D_FF = 2816
ROPE_THETA = 10000.0
NORM_EPS = 1e-6
N_BRANCH = 3
HALF_STEP = 0.5
FOX_W = N_FOX * HEAD_DIM
DSA_W = N_DSA * HEAD_DIM
SB_W = N_SB * HEAD_DIM
CA_W = N_CA_HEADS * CA_HEAD_DIM
IN_SPLITS = (3 * FOX_W, N_FOX, 3 * DSA_W, N_IDX_HEADS * IDX_DIM, IDX_DIM, N_IDX_HEADS, 3 * SB_W,
             N_BRANCH * D_MODEL)

LANES = 128
HEADS_PER_BLOCK = LANES // HEAD_DIM
PAIR_W = 3 * LANES
VMEM_LIMIT = 56 * 2**20

BLK_DSA_Q, BLK_DSA_K = 0, 3
BLK_IDX_KA = 6
BLK_IDX_KB = 7
BLK_IDX_Q = 8
N_ROPE_BLOCKS = 12
BLK_FOX_Q, BLK_FOX_K, BLK_FOX_V = 12, 15, 18
BLK_DSA_V = 21
BLK_SB_Q, BLK_SB_K, BLK_SB_V = 24, 27, 30
N_PROJ_BLOCKS = 33
PROJ_CHUNK = 4 * LANES
SMALL_FGATE_LANE = 0
SMALL_IDXW_LANE = 8

NEG_BIG = -1e30
SB_CUTOFF = -110.0
INT_MIN = -2**31


def _params(n_grid):
    return pltpu.CompilerParams(dimension_semantics=("arbitrary",) * n_grid, vmem_limit_bytes=VMEM_LIMIT)


def _resident(shape):
    nd = len(shape)
    return pl.BlockSpec(shape, lambda *_: (0,) * nd, pipeline_mode=pl.Buffered(1))


def _layer(shape, layer):
    nd = len(shape)
    return pl.BlockSpec((None,) + tuple(shape), lambda *_: (layer,) + (0,) * nd, pipeline_mode=pl.Buffered(1))


def _rms(x, g):
    return x * lax.rsqrt(jnp.mean(x * x, axis=-1, keepdims=True) + NORM_EPS) * g


def _dot(a, b):
    return jnp.dot(a, b, preferred_element_type=F32)


def _dot_nt(a, b):
    return lax.dot_general(a, b, (((1,), (1,)), ((), ())), preferred_element_type=F32)


def _split3(x):
    hi = x.astype(BF16)
    r = x - hi.astype(F32)
    mid = r.astype(BF16)
    lo = (r - mid.astype(F32)).astype(BF16)
    return hi, mid, lo


def _log1p_exp_neg_abs(z):
    return jnp.log1p(jnp.exp(-jnp.abs(z)))


def _ffn_chunks():
    out, c = [], 0
    while c < D_FF:
        w = min(512, D_FF - c)
        out.append((c, w))
        c += w
    return tuple(out)


def _ffn_kernel(*refs, final):
    if final:
        x_ref, g_ref, wgu_ref, wd_ref, fn_ref, o_ref, a_ref = refs
    else:
        x_ref, g_ref, wgu_ref, wd_ref, o_ref, a_ref = refs
    x = x_ref[...]
    h = _rms(x, g_ref[...]).astype(BF16)
    for c0, w in _ffn_chunks():
        g = _dot(h, wgu_ref[:, c0:c0 + w])
        u = _dot(h, wgu_ref[:, D_FF + c0:D_FF + c0 + w])
        a_ref[:, c0:c0 + w] = (g * jax.nn.sigmoid(g) * u).astype(BF16)
    y = x + HALF_STEP * _dot(a_ref[...], wd_ref[...])
    if final:
        y = _rms(y, fn_ref[...])
    o_ref[...] = y


def _ffn(x2, layer, gains, w_gu, w_down, final_gain=None):
    m, d = x2.shape
    tm = min(1024, m)
    final = final_gain is not None
    in_specs = [pl.BlockSpec((tm, d), lambda i: (i, 0)), _layer((1, d), layer),
                _layer((d, 2 * D_FF), layer), _layer((D_FF, d), layer)]
    args = [x2, gains, w_gu, w_down]
    if final:
        in_specs.append(_resident((1, d)))
        args.append(final_gain.reshape(1, d))
    return pl.pallas_call(
        functools.partial(_ffn_kernel, final=final),
        grid=(m // tm,),
        in_specs=in_specs,
        out_specs=pl.BlockSpec((tm, d), lambda i: (i, 0)),
        out_shape=jax.ShapeDtypeStruct((m, d), F32),
        scratch_shapes=[pltpu.VMEM((tm, D_FF), BF16)],
        compiler_params=_params(1),
        name="ffn",
    )(*args)


def _first_half_lanes():
    lane = lax.broadcasted_iota(I32, (1, LANES), 1)
    return (lane % HEAD_DIM) < (HEAD_DIM // 2)


def _rope_tables_kernel(pos_ref, invf_ref, cos_ref, sin_ref):
    ang = pos_ref[...].astype(F32) * invf_ref[...]
    sin = jnp.sin(ang)
    cos_ref[...] = jnp.cos(ang)
    sin_ref[...] = jnp.where(_first_half_lanes(), -sin, sin)


def _rope_tables(pos2, invf):
    m = pos2.shape[0]
    tm = min(512, m)
    tok = pl.BlockSpec((tm, LANES), lambda i: (i, 0))
    return pl.pallas_call(
        _rope_tables_kernel,
        grid=(m // tm,),
        in_specs=[pl.BlockSpec((tm, 1), lambda i: (i, 0)), _resident((1, LANES))],
        out_specs=[tok, tok],
        out_shape=[jax.ShapeDtypeStruct((m, LANES), F32)] * 2,
        compiler_params=_params(1),
        name="rope_tables",
    )(pos2, invf)


FOX_FEAT_ROWS = 16


def _fox_feature_maps():
    k_map = np.zeros((3, LANES, FOX_W), np.float32)
    q_map = np.zeros((3, LANES, LANES), np.float32)
    k_one = np.zeros((1, FOX_W), np.float32)
    q_one = np.zeros((1, LANES), np.float32)
    for h in range(N_FOX):
        p, hh = divmod(h, HEADS_PER_BLOCK)
        for j in range(3):
            k_map[j, h, p * LANES + 3 * hh + j] = -1.0
            q_map[j, h, FOX_FEAT_ROWS * h + 6 + j] = 1.0
            k_one[0, p * LANES + 6 + j] = 1.0
            q_one[0, FOX_FEAT_ROWS * h + 3 * hh + j] = 1.0
    return (jnp.asarray(k_map, BF16), jnp.asarray(q_map, BF16), jnp.asarray(k_one), jnp.asarray(q_one))


def _fox_decay_features(logits, b_ref, kmap_ref, qmap_ref, kone_ref, qone_ref, kf_ref, qf_ref, carry_ref):
    tc = logits.shape[0]
    z = logits + b_ref[...]
    log_f = jnp.minimum(z, 0.0) - _log1p_exp_neg_abs(z)
    row = lax.broadcasted_iota(I32, (tc, tc), 0)
    col = lax.broadcasted_iota(I32, (tc, tc), 1)
    tri = jnp.where(col <= row, 1.0, 0.0).astype(BF16)
    hi, mid, lo = _split3(log_f)
    yield
    cs = _dot(tri, hi) + _dot(tri, mid) + _dot(tri, lo) + carry_ref[...]
    carry_ref[...] = cs[tc - 1:tc, :]
    terms = _split3(cs)
    yield
    kf = kone_ref[...]
    qf = qone_ref[...]
    for j in range(3):
        kf = kf + _dot(terms[j], kmap_ref[j])
        qf = qf + _dot(terms[j], qmap_ref[j])
    kf_ref[...] = kf.astype(BF16)
    qf_ref[...] = qf.astype(BF16)


def _mixer_in_kernel(x_ref, g_ref, cos_ref, sin_ref, wb_ref, b_ref, kmap_ref, qmap_ref, kone_ref, qone_ref,
                     pb_ref, ps_ref, kf_ref, qf_ref, carry_ref, *, tiles_per_seq):
    @pl.when(pl.program_id(0) % tiles_per_seq == 0)
    def _():
        carry_ref[...] = jnp.zeros_like(carry_ref)

    h = _rms(x_ref[...], g_ref[...]).astype(BF16)
    cos = cos_ref[...]
    sin_signed = sin_ref[...]
    first_half = _first_half_lanes()
    n_cols = (N_PROJ_BLOCKS + 1) * LANES
    starts = list(range(0, n_cols, PROJ_CHUNK))
    decay_stages = iter(())
    for c0 in starts[-1:] + starts[:-1]:
        c1 = min(c0 + PROJ_CHUNK, n_cols)
        res = _dot(h, wb_ref[:, c0:c1])
        next(decay_stages, None)
        if c1 == n_cols:
            small = res[:, c1 - c0 - LANES:]
            ps_ref[...] = small
            decay_stages = _fox_decay_features(small, b_ref, kmap_ref, qmap_ref, kone_ref, qone_ref, kf_ref,
                                               qf_ref, carry_ref)
            next(decay_stages)
            c1 -= LANES
            res = res[:, :c1 - c0]
        if c0 < N_ROPE_BLOCKS * LANES:
            parts = []
            for b in range((c1 - c0) // LANES):
                xb = res[:, b * LANES:(b + 1) * LANES]
                partner = jnp.where(first_half, pltpu.roll(xb, LANES - HEAD_DIM // 2, 1),
                                    pltpu.roll(xb, HEAD_DIM // 2, 1))
                parts.append(xb * cos + partner * sin_signed)
            res = jnp.concatenate(parts, axis=1)
        pb_ref[:, c0:c1] = res.astype(BF16)


def _mixer_in(x2, layer, gains, cos, sin_signed, wb, bias_rows, feature_maps, batch, seq):
    m, d = x2.shape
    tm = min(512, seq)
    nb = N_PROJ_BLOCKS * LANES
    tok = lambda w: pl.BlockSpec((tm, w), lambda i: (i, 0))
    pb, ps, k_side, q_tok = pl.pallas_call(
        functools.partial(_mixer_in_kernel, tiles_per_seq=seq // tm),
        grid=(m // tm,),
        in_specs=[tok(d), _layer((1, d), layer), tok(LANES), tok(LANES), _layer((d, nb + LANES), layer),
                  _layer((1, LANES), layer), _resident((3, LANES, FOX_W)), _resident((3, LANES, LANES)),
                  _resident((1, FOX_W)), _resident((1, LANES))],
        out_specs=[tok(nb), tok(LANES), tok(FOX_W), tok(LANES)],
        out_shape=[jax.ShapeDtypeStruct((m, nb), BF16), jax.ShapeDtypeStruct((m, LANES), F32),
                   jax.ShapeDtypeStruct((m, FOX_W), BF16), jax.ShapeDtypeStruct((m, LANES), BF16)],
        scratch_shapes=[pltpu.VMEM((1, LANES), F32)],
        compiler_params=_params(1),
        name="mixer_in",
    )(x2, gains, cos, sin_signed, wb, bias_rows, *feature_maps)
    q_side = q_tok[:, :N_FOX * FOX_FEAT_ROWS].reshape(batch, seq, N_FOX, FOX_FEAT_ROWS).transpose(0, 2, 3, 1)
    return pb, ps, k_side, q_side


def _head_lane_mask(hh):
    lane = lax.broadcasted_iota(I32, (1, LANES), 1)
    return (lane < HEAD_DIM) if hh == 0 else (lane >= HEAD_DIM)


def _head_row_mask(hh):
    row = lax.broadcasted_iota(I32, (LANES, 1), 0)
    return (row < HEAD_DIM) if hh == 0 else (row >= HEAD_DIM)


def _transpose_bf16(x):
    return x.astype(F32).T.astype(BF16)


def _fill_transposed(src_ref, dst_ref):
    n_blocks, _, tk = dst_ref.shape
    for j in range(n_blocks):
        dst_ref[j] = _transpose_bf16(src_ref[j * tk:(j + 1) * tk, :])


V_ROWS = HEAD_DIM + 16


def _fill_transposed_with_ones(src_ref, dst_ref, n_heads):
    n_blocks, _, tk = dst_ref.shape
    ones_pad = jnp.where(lax.broadcasted_iota(I32, (V_ROWS - HEAD_DIM, tk), 0) == 0, 1.0, 0.0).astype(BF16)
    for j in range(n_blocks):
        v_t = _transpose_bf16(src_ref[j * tk:(j + 1) * tk, :])
        for h in range(n_heads):
            dst_ref[j, h * V_ROWS:h * V_ROWS + HEAD_DIM, :] = v_t[h * HEAD_DIM:(h + 1) * HEAD_DIM, :]
            dst_ref[j, h * V_ROWS + HEAD_DIM:(h + 1) * V_ROWS, :] = ones_pad


def _softmax_steps(carry, scores, values_t):
    stats = []
    for (m, _), s in zip(carry, scores):
        m_new = jnp.maximum(m, jnp.max(s, axis=0, keepdims=True))
        stats.append((m_new, jnp.exp(m - m_new), jnp.exp(s - m_new).astype(BF16)))
    return tuple((m_new, alpha * acc + _dot(v_t, p))
                 for (m_new, alpha, p), (_, acc), v_t in zip(stats, carry, values_t))


def _softmax_init(tq):
    return (jnp.full((1, tq), NEG_BIG, F32), jnp.zeros((V_ROWS, tq), F32))


def _softmax_result(state):
    _, acc = state
    return acc[:HEAD_DIM, :] / acc[HEAD_DIM:HEAD_DIM + 1, :]


def _flash_blocks(n, score_fn, value_fn, carry, s_ref):
    n_heads = len(carry)

    def load(slot):
        return [s_ref[slot, h] for h in range(n_heads)]

    def store(slot, kb):
        for h, s in enumerate(score_fn(kb)):
            s_ref[slot, h] = s

    odd = n % 2
    carry = lax.cond(odd == 1, lambda c: _softmax_steps(c, score_fn(0), value_fn(0)), lambda c: c, carry)

    @pl.when(n >= 2)
    def _():
        store(0, odd)

    def body(j, c):
        kb = odd + 2 * j
        store(1, kb + 1)
        c = _softmax_steps(c, load(0), value_fn(kb))
        store(0, jnp.minimum(kb + 2, n - 1))
        return _softmax_steps(c, load(1), value_fn(kb + 1))

    return lax.fori_loop(0, n // 2, body, carry)


def _heads_to_token_rows(out_t):
    return jnp.concatenate(out_t, axis=0).T.astype(BF16)


def _fox_kernel(q_ref, k_ref, v_ref, kf_ref, qf_ref, o_ref, vt_ref, s_ref):
    tq = q_ref.shape[0]
    tk = vt_ref.shape[2]
    i = pl.program_id(1)

    @pl.when(i == 0)
    def _():
        _fill_transposed_with_ones(v_ref, vt_ref, N_FOX)

    key_pos = lax.broadcasted_iota(I32, (tk, tq), 0)
    qry_pos = lax.broadcasted_iota(I32, (tk, tq), 1)
    causal = [key_pos + d * tk <= qry_pos for d in range(2)]
    pad = jnp.zeros((LANES - FOX_FEAT_ROWS, tq), BF16)
    qx = []
    for p in range(FOX_W // LANES):
        q_t = q_ref[:, p * LANES:(p + 1) * LANES].astype(F32).T
        for hh in range(HEADS_PER_BLOCK):
            qx.append(jnp.concatenate([jnp.where(_head_row_mask(hh), q_t, 0.0).astype(BF16),
                                       qf_ref[0, p * HEADS_PER_BLOCK + hh], pad], axis=0))

    def store(slot, kb):
        k0 = pl.multiple_of(kb * tk, tk)
        for h in range(N_FOX):
            p = h // HEADS_PER_BLOCK
            kx = jnp.concatenate([k_ref[pl.ds(k0, tk), p * LANES:(p + 1) * LANES],
                                  kf_ref[pl.ds(k0, tk), p * LANES:(p + 1) * LANES]], axis=1)
            s_ref[slot, h] = _dot(kx, qx[h])

    def step(carry, slot, kb, mask=None):
        scores = [s_ref[slot, h] for h in range(N_FOX)]
        if mask is not None:
            scores = [jnp.where(mask, s, NEG_BIG) for s in scores]
        return _softmax_steps(carry, scores, [vt_ref[kb, h * V_ROWS:(h + 1) * V_ROWS, :] for h in range(N_FOX)])

    def trip(j, carry):
        kb = 2 * j
        store(1, kb + 1)
        carry = step(carry, 0, kb)
        store(0, kb + 2)
        return step(carry, 1, kb + 1)

    store(0, 0)
    carry = lax.fori_loop(0, i, trip, tuple(_softmax_init(tq) for _ in range(N_FOX)))
    kb = 2 * i
    store(1, kb + 1)
    carry = step(carry, 0, kb, causal[0])
    carry = step(carry, 1, kb + 1, causal[1])
    for p in range(FOX_W // LANES):
        o_ref[:, p * LANES:(p + 1) * LANES] = _heads_to_token_rows(
            [_softmax_result(st) for st in carry[p * HEADS_PER_BLOCK:(p + 1) * HEADS_PER_BLOCK]])


def _fox_attention(pb, k_side, q_side, batch, seq):
    tq = min(512, seq)
    tk = tq // 2
    nq = seq // tq
    m = batch * seq
    return pl.pallas_call(
        _fox_kernel,
        grid=(batch, nq),
        in_specs=[pl.BlockSpec((tq, FOX_W), lambda b, i: (b * nq + i, BLK_FOX_Q // 3)),
                  pl.BlockSpec((seq, FOX_W), lambda b, i: (b, BLK_FOX_K // 3)),
                  pl.BlockSpec((seq, FOX_W), lambda b, i: (b, BLK_FOX_V // 3)),
                  pl.BlockSpec((seq, FOX_W), lambda b, i: (b, 0)),
                  pl.BlockSpec((1, N_FOX, FOX_FEAT_ROWS, tq), lambda b, i: (b, 0, 0, i))],
        out_specs=pl.BlockSpec((tq, FOX_W), lambda b, i: (b * nq + i, 0)),
        out_shape=jax.ShapeDtypeStruct((m, FOX_W), BF16),
        scratch_shapes=[pltpu.VMEM((seq // tk, N_FOX * V_ROWS, tk), BF16), pltpu.VMEM((2, N_FOX, tk, tq), F32)],
        compiler_params=_params(2),
        name="fox_attention",
    )(pb, pb, pb, k_side, q_side)


def _sb_kernel(q_ref, k_ref, v_ref, o_ref, vt_ref):
    tq = q_ref.shape[0]
    tk = vt_ref.shape[2]
    i = pl.program_id(1)

    @pl.when(i == 0)
    def _():
        _fill_transposed(v_ref, vt_ref)

    heads = range(N_SB)
    q_t = [q_ref[:, p * LANES:(p + 1) * LANES].astype(F32).T for p in range(PAIR_W // LANES)]
    qh = [jnp.where(_head_row_mask(h % HEADS_PER_BLOCK), q_t[h // HEADS_PER_BLOCK], 0.0).astype(BF16) for h in heads]
    r = lax.broadcasted_iota(I32, (tk, tk), 0)
    c = lax.broadcasted_iota(I32, (tk, tk), 1)
    upper = jnp.where(c > r, 1.0, 0.0).astype(BF16)
    key_pos = lax.broadcasted_iota(I32, (tk, tq), 0)
    qry_pos = lax.broadcasted_iota(I32, (tk, tq), 1) + i * tq

    def process(kb, state, masked):
        k0 = pl.multiple_of(kb * tk, tk)
        strict = key_pos + k0 < qry_pos
        z = [_dot(k_ref[pl.ds(k0, tk), (h // HEADS_PER_BLOCK) * LANES:(h // HEADS_PER_BLOCK + 1) * LANES], qh[h])
             for h in heads]
        log_beta, log_1m = [], []
        for zh in z:
            l1m = jnp.minimum(-zh, 0.0) - jnp.log(1.0 + jnp.exp(-jnp.abs(zh)))
            log_beta.append(l1m + zh)
            log_1m.append(jnp.where(strict, l1m, 0.0) if masked else l1m)
        after = []
        for h in heads:
            hi, mid, lo = _split3(log_1m[h])
            after.append(_dot(upper, hi) + _dot(upper, mid) + _dot(upper, lo) + state[h][0])
        weights = []
        for h in heads:
            a = jnp.exp(log_beta[h] + after[h])
            weights.append((jnp.where(strict, a, 0.0) if masked else a).astype(BF16))
        return tuple((state[h][0] + jnp.sum(log_1m[h], axis=0, keepdims=True),
                      state[h][1] + _dot(vt_ref[kb, h * HEAD_DIM:(h + 1) * HEAD_DIM, :], weights[h]))
                     for h in heads)

    def run_max(state):
        m = state[0][0]
        for run, _ in state[1:]:
            m = jnp.maximum(m, run)
        return jnp.max(m)

    state = tuple((jnp.zeros((1, tq), F32), jnp.zeros((HEAD_DIM, tq), F32)) for _ in heads)
    n_diag = tq // tk
    for d in range(n_diag):
        state = process((i + 1) * n_diag - 1 - d, state, True)

    def cond(carry):
        kb, worst, _ = carry
        return jnp.logical_and(kb >= 0, worst > SB_CUTOFF)

    def body(carry):
        kb, _, st = carry
        st = process(kb, st, False)
        return kb - 1, run_max(st), st

    _, _, state = lax.while_loop(cond, body, (i * n_diag - 1, run_max(state), state))
    for p in range(PAIR_W // LANES):
        outs = [state[h][1] for h in heads if h // HEADS_PER_BLOCK == p]
        outs += [jnp.zeros((HEAD_DIM, tq), F32)] * (HEADS_PER_BLOCK - len(outs))
        o_ref[:, p * LANES:(p + 1) * LANES] = _heads_to_token_rows(outs)


def _sb_attention(pb, batch, seq):
    tq = min(256, seq)
    tk = min(256, seq)
    nq = seq // tq
    m = batch * seq
    return pl.pallas_call(
        _sb_kernel,
        grid=(batch, nq),
        in_specs=[pl.BlockSpec((tq, PAIR_W), lambda b, i: (b * nq + i, BLK_SB_Q // 3)),
                  pl.BlockSpec((seq, PAIR_W), lambda b, i: (b, BLK_SB_K // 3)),
                  pl.BlockSpec((seq, PAIR_W), lambda b, i: (b, BLK_SB_V // 3))],
        out_specs=pl.BlockSpec((tq, PAIR_W), lambda b, i: (b * nq + i, 0)),
        out_shape=jax.ShapeDtypeStruct((m, PAIR_W), BF16),
        scratch_shapes=[pltpu.VMEM((seq // tk, PAIR_W, tk), BF16)],
        compiler_params=_params(2),
        name="sb_attention",
    )(pb, pb, pb)


HALF_BITS = 16
HALF_OFFSET = 1 << (HALF_BITS - 1)


def _dsa_kernel(q_ref, iq_ref, w_ref, k_ref, v_ref, ka_ref, kb_ref, o_ref, key_ref, hi_ref, lo_ref, bias_ref,
                vt_ref, s_ref, *, topk, seq):
    t = q_ref.shape[0]
    i = pl.program_id(1)

    @pl.when(i == 0)
    def _():
        _fill_transposed_with_ones(v_ref, vt_ref, N_DSA)

    n_kb = i + 1
    kf = float(topk)
    imin = jnp.int32(INT_MIN)
    key_pos = lax.broadcasted_iota(I32, (t, t), 0)
    qry_pos = lax.broadcasted_iota(I32, (t, t), 1) + i * t

    iq_t = [_transpose_bf16(iq_ref[:, g * LANES:(g + 1) * LANES]) for g in range(N_IDX_HEADS // 2)]
    w_t = (w_ref[...] * (N_IDX_HEADS ** -0.5) * (IDX_DIM ** -0.5)).T

    def index_block(kb):
        k0 = pl.multiple_of(kb * t, t)
        ka = ka_ref[pl.ds(k0, t), :]
        kb_ = kb_ref[pl.ds(k0, t), :]
        score = jnp.zeros((t, t), F32)
        for g in range(N_IDX_HEADS // 2):
            lo = SMALL_IDXW_LANE + 2 * g
            score = score + jnp.maximum(_dot(ka, iq_t[g]), 0.0) * w_t[lo:lo + 1, :]
            score = score + jnp.maximum(_dot(kb_, iq_t[g]), 0.0) * w_t[lo + 1:lo + 2, :]
        bits = lax.bitcast_convert_type(score + 0.0, I32)
        key = jnp.where(bits < 0, bits ^ jnp.int32(0x7FFFFFFF), bits)
        key = jnp.where(key_pos + k0 <= qry_pos, key, imin)
        key_ref[pl.ds(k0, t), :] = key
        hi_ref[pl.ds(k0, t), :] = lax.shift_right_arithmetic(key, HALF_BITS).astype(jnp.int16)
        lo_ref[pl.ds(k0, t), :] = ((key & (2 * HALF_OFFSET - 1)) - HALF_OFFSET).astype(jnp.int16)

    odd = n_kb % 2

    @pl.when(odd == 1)
    def _():
        index_block(0)

    def index_pair(j, _):
        index_block(odd + 2 * j)
        index_block(odd + 2 * j + 1)
        return 0

    lax.fori_loop(0, n_kb // 2, index_pair, 0)

    masked16 = jnp.full((t, t), -HALF_OFFSET, jnp.int16)

    @pl.when(odd == 1)
    def _():
        k0 = pl.multiple_of(n_kb * t, t)
        hi_ref[pl.ds(k0, t), :] = masked16
        lo_ref[pl.ds(k0, t), :] = masked16

    n_scan = (n_kb + 1) // 2
    scan_rows = 2 * t

    def count16(ref, pred):
        rows = 16
        def body(trip, acc):
            k0 = pl.multiple_of(trip * scan_rows, scan_rows)
            ind = pred(ref[pl.ds(k0, scan_rows), :])
            parts = [ind[r:r + rows, :] for r in range(0, scan_rows, rows)]
            while len(parts) > 1:
                parts = [parts[j] + parts[j + 1] for j in range(0, len(parts), 2)]
            return acc + parts[0]
        acc = lax.fori_loop(0, n_scan, body, jnp.zeros((rows, t), jnp.int16))
        return jnp.sum(acc.astype(F32), axis=0, keepdims=True)

    one16, zero16 = jnp.int16(1), jnp.int16(0)

    def kth_half(ref, need, n_all):
        def bit_step(it, state):
            prefix, cnt_prefix = state
            cand_u = prefix | lax.shift_left(jnp.int32(1), HALF_BITS - 1 - it)
            cand = (cand_u - HALF_OFFSET).astype(jnp.int16)
            cnt = count16(ref, lambda blk: jnp.where(blk >= cand, one16, zero16))
            take = cnt >= need
            return jnp.where(take, cand_u, prefix), jnp.where(take, cnt, cnt_prefix)
        prefix, cnt = lax.fori_loop(0, HALF_BITS, bit_step, (jnp.zeros((1, t), I32), n_all))
        return prefix - HALF_OFFSET, cnt


    n_scanned = (n_scan * scan_rows).astype(F32)
    kth_hi, n_ge_hi = kth_half(hi_ref, kf, jnp.full((1, t), n_scanned, F32))
    kth_hi16 = kth_hi.astype(jnp.int16)

    def above_and_bucket(trip, acc):
        rows = 16
        k0 = pl.multiple_of(trip * scan_rows, scan_rows)
        hi = hi_ref[pl.ds(k0, scan_rows), :]
        lo_ref[pl.ds(k0, scan_rows), :] = jnp.where(hi == kth_hi16, lo_ref[pl.ds(k0, scan_rows), :],
                                                    jnp.int16(-HALF_OFFSET))
        ind = jnp.where(hi > kth_hi16, one16, zero16)
        parts = [ind[r:r + rows, :] for r in range(0, scan_rows, rows)]
        while len(parts) > 1:
            parts = [parts[j] + parts[j + 1] for j in range(0, len(parts), 2)]
        return acc + parts[0]

    n_gt_hi = jnp.sum(lax.fori_loop(0, n_scan, above_and_bucket, jnp.zeros((16, t), jnp.int16)).astype(F32),
                      axis=0, keepdims=True)
    need_lo = kf - n_gt_hi
    kth_lo, n_ge_lo = kth_half(lo_ref, need_lo, n_ge_hi - n_gt_hi)
    kth = lax.shift_left(kth_hi, HALF_BITS) + (kth_lo + HALF_OFFSET)

    excess = jnp.where(kth == imin, float(2 * seq), n_ge_lo - need_lo)
    any_excess = jnp.max(excess) > 0.0

    @pl.when(jnp.logical_not(any_excess))
    def _():
        def select(kb, _):
            k0 = pl.multiple_of(kb * t, t)
            bias_ref[pl.ds(k0, t), :] = jnp.where(key_ref[pl.ds(k0, t), :] >= kth, 0.0, NEG_BIG)
            return 0

        lax.fori_loop(0, n_kb, select, 0)

    @pl.when(any_excess)
    def _():
        r = lax.broadcasted_iota(I32, (t, t), 0)
        c = lax.broadcasted_iota(I32, (t, t), 1)
        later = jnp.where(c > r, 1.0, 0.0).astype(BF16)

        def select(step, tied_after):
            kb = n_kb - 1 - step
            k0 = pl.multiple_of(kb * t, t)
            kblk = key_ref[pl.ds(k0, t), :]
            tied = kblk == kth
            tied01 = jnp.where(tied, 1.0, 0.0).astype(BF16)
            after = _dot(later, tied01) + tied_after
            tie_bias = jnp.where(tied, jnp.where(after >= excess, 0.0, NEG_BIG), NEG_BIG)
            bias_ref[pl.ds(k0, t), :] = jnp.where(kblk > kth, 0.0, tie_bias)
            return after[0:1, :] + tied01[0:1, :].astype(F32)

        lax.fori_loop(0, n_kb, select, jnp.zeros((1, t), F32))

    q_t = [q_ref[:, p * LANES:(p + 1) * LANES].astype(F32).T for p in range(PAIR_W // LANES)]
    qh = [jnp.where(_head_row_mask(h % HEADS_PER_BLOCK), q_t[h // HEADS_PER_BLOCK], 0.0).astype(BF16)
          for h in range(N_DSA)]

    def scores(kb):
        k0 = pl.multiple_of(kb * t, t)
        bias = bias_ref[pl.ds(k0, t), :]
        return [_dot(k_ref[pl.ds(k0, t), (h // HEADS_PER_BLOCK) * LANES:(h // HEADS_PER_BLOCK + 1) * LANES], qh[h])
                + bias for h in range(N_DSA)]

    def values_t(kb):
        return [vt_ref[kb, h * V_ROWS:(h + 1) * V_ROWS, :] for h in range(N_DSA)]

    carry = _flash_blocks(n_kb, scores, values_t, tuple(_softmax_init(t) for _ in range(N_DSA)), s_ref)
    outs = [_softmax_result(st) for st in carry]
    outs += [jnp.zeros((HEAD_DIM, t), F32)] * (PAIR_W // HEAD_DIM - N_DSA)
    for p in range(PAIR_W // LANES):
        o_ref[:, p * LANES:(p + 1) * LANES] = _heads_to_token_rows(
            outs[p * HEADS_PER_BLOCK:(p + 1) * HEADS_PER_BLOCK])


def _dsa_attention(pb, ps, batch, seq):
    t = min(256, seq)
    nq = seq // t
    m = batch * seq
    topk = min(TOPK_MAX, seq // 4)
    return pl.pallas_call(
        functools.partial(_dsa_kernel, topk=topk, seq=seq),
        grid=(batch, nq),
        in_specs=[pl.BlockSpec((t, PAIR_W), lambda b, i: (b * nq + i, BLK_DSA_Q // 3)),
                  pl.BlockSpec((t, 4 * LANES), lambda b, i: (b * nq + i, BLK_IDX_Q // 4)),
                  pl.BlockSpec((t, LANES), lambda b, i: (b * nq + i, 0)),
                  pl.BlockSpec((seq, PAIR_W), lambda b, i: (b, BLK_DSA_K // 3)),
                  pl.BlockSpec((seq, PAIR_W), lambda b, i: (b, BLK_DSA_V // 3)),
                  pl.BlockSpec((seq, LANES), lambda b, i: (b, BLK_IDX_KA)),
                  pl.BlockSpec((seq, LANES), lambda b, i: (b, BLK_IDX_KB))],
        out_specs=pl.BlockSpec((t, PAIR_W), lambda b, i: (b * nq + i, 0)),
        out_shape=jax.ShapeDtypeStruct((m, PAIR_W), BF16),
        scratch_shapes=[pltpu.VMEM((seq, t), I32), pltpu.VMEM((seq + t, t), jnp.int16),
                        pltpu.VMEM((seq + t, t), jnp.int16),
                        pltpu.VMEM((seq, t), F32),
                        pltpu.VMEM((nq, N_DSA * V_ROWS, t), BF16), pltpu.VMEM((2, N_DSA, t, t), F32)],
        compiler_params=_params(2),
        name="dsa_attention",
    )(pb, pb, ps, pb, pb, pb, pb)


def _mixer_out_cross_kernel(x_ref, gain_ref, of_ref, od_ref, os_ref, wg_ref, wf_ref, wd_ref, ws_ref, wo_ref,
                            ca_gain_ref, ca_wq_ref, kv_ref, ca_wo_ref, o_ref):
    d = x_ref.shape[1]
    x = x_ref[...]
    h = _rms(x, gain_ref[...]).astype(BF16)
    merged = None
    for b, (o_b, w_b) in enumerate(((of_ref, wf_ref), (od_ref, wd_ref), (os_ref, ws_ref))):
        term = jax.nn.sigmoid(_dot(h, wg_ref[:, b * d:(b + 1) * d])) * _dot(o_b[...], w_b[...])
        merged = term if merged is None else merged + term
    x = x + _dot(merged.astype(BF16), wo_ref[...])

    q = _dot(_rms(x, ca_gain_ref[...]).astype(BF16), ca_wq_ref[...])
    scale = CA_HEAD_DIM ** -0.5
    outs = []
    for hd in range(N_CA_HEADS):
        qh = q[:, hd * CA_HEAD_DIM:(hd + 1) * CA_HEAD_DIM].astype(BF16)
        logits = _dot_nt(qh, kv_ref[:, hd * CA_HEAD_DIM:(hd + 1) * CA_HEAD_DIM]) * scale
        e = jnp.exp(logits - jnp.max(logits, axis=1, keepdims=True))
        pv = _dot(e.astype(BF16), kv_ref[:, CA_W + hd * CA_HEAD_DIM:CA_W + (hd + 1) * CA_HEAD_DIM])
        outs.append(pv / jnp.sum(e, axis=1, keepdims=True))
    o_ref[...] = x + _dot(jnp.concatenate(outs, axis=1).astype(BF16), ca_wo_ref[...])


def _mixer_out_cross(x2, layer, mix, o_fox, o_dsa, o_sb, kv, batch, seq):
    m, d = x2.shape
    tm = min(1024, seq)
    n_mem = kv.shape[0] // batch
    per_batch = seq // tm
    row = lambda w: pl.BlockSpec((tm, w), lambda i: (i, 0))
    lay = lambda *shape: _layer(shape, layer)
    return pl.pallas_call(
        _mixer_out_cross_kernel,
        grid=(m // tm,),
        in_specs=[row(d), lay(1, d), row(FOX_W), row(PAIR_W), row(PAIR_W), lay(d, N_BRANCH * d),
                  lay(FOX_W, d), lay(PAIR_W, d), lay(PAIR_W, d), lay(d, d), lay(1, d), lay(d, CA_W),
                  pl.BlockSpec((n_mem, 2 * CA_W), lambda i: (i // per_batch, 0)), lay(CA_W, d)],
        out_specs=row(d),
        out_shape=jax.ShapeDtypeStruct((m, d), F32),
        compiler_params=_params(1),
        name="mixer_out_cross",
    )(x2, mix["mix_gain"], o_fox, o_dsa, o_sb, mix["w_gates"], mix["w_fox_out"], mix["w_dsa_out"], mix["w_sb_out"],
      mix["w_out"], mix["ca_gain"], mix["ca_wq"], kv, mix["ca_wo"])


def _norm_matmul_kernel(x_ref, g_ref, w_ref, o_ref):
    o_ref[...] = _dot(_rms(x_ref[...], g_ref[...]).astype(BF16), w_ref[...]).astype(o_ref.dtype)


def _norm_matmul(x2, layer, gains, w, out_dtype):
    m, d = x2.shape
    n = w.shape[-1]
    tm = min(512, m)
    return pl.pallas_call(
        _norm_matmul_kernel,
        grid=(m // tm,),
        in_specs=[pl.BlockSpec((tm, d), lambda i: (i, 0)), _layer((1, d), layer), _layer((d, n), layer)],
        out_specs=pl.BlockSpec((tm, n), lambda i: (i, 0)),
        out_shape=jax.ShapeDtypeStruct((m, n), out_dtype),
        compiler_params=_params(1),
        name="norm_matmul",
    )(x2, gains, w)


def _pad_last(w, n):
    return jnp.pad(w, [(0, 0)] * (w.ndim - 1) + [(0, n - w.shape[-1])])


def _pad_rows(w, n):
    return jnp.pad(w, [(0, 0)] * (w.ndim - 2) + [(0, n - w.shape[-2]), (0, 0)])


def _mixer_weights(w_in):
    offs = np.cumsum((0,) + IN_SPLITS)
    part = lambda j: w_in[..., offs[j]:offs[j + 1]]
    thirds = lambda w: jnp.split(w, 3, axis=-1)
    scale = HEAD_DIM ** -0.5
    fq, fk, fv = thirds(part(0))
    dq, dk, dv = thirds(part(2))
    sq, sk, sv = thirds(part(6))
    iq, ik, iw = part(3), part(4), part(5)
    zk = jnp.zeros_like(ik)
    small = jnp.concatenate([part(1), jnp.zeros_like(w_in[..., :SMALL_IDXW_LANE - N_FOX]), iw], axis=-1)
    wb = jnp.concatenate(
        [_pad_last(dq * scale, PAIR_W), _pad_last(dk, PAIR_W),
         jnp.concatenate([ik, zk], axis=-1), jnp.concatenate([zk, ik], axis=-1), iq,
         fq * scale, fk, fv, _pad_last(dv, PAIR_W),
         _pad_last(sq * scale, PAIR_W), _pad_last(sk, PAIR_W), _pad_last(sv, PAIR_W),
         _pad_last(small, LANES)], axis=-1).astype(BF16)
    return wb, part(7).astype(BF16)


def _stacked_parameters(mix_norm, w_in, b_fgate, w_fox_out, w_dsa_out, w_sb_out, w_out, ca_norm, ca_w_q, ca_w_o):
    depth, d = mix_norm.shape
    w_proj, w_gates = _mixer_weights(w_in)
    return {
        "mix_gain": mix_norm.reshape(depth, 1, d),
        "w_proj": w_proj,
        "w_gates": w_gates,
        "fgate_bias": _pad_last(b_fgate.astype(F32), LANES).reshape(depth, 1, LANES),
        "w_fox_out": w_fox_out.astype(BF16),
        "w_dsa_out": _pad_rows(w_dsa_out, PAIR_W).astype(BF16),
        "w_sb_out": _pad_rows(w_sb_out, PAIR_W).astype(BF16),
        "w_out": w_out.astype(BF16),
        "ca_gain": ca_norm.reshape(depth, 1, d),
        "ca_wq": ca_w_q.astype(BF16),
        "ca_wo": ca_w_o.astype(BF16),
    }


def _mixer_and_cross(x2, layer, mix, cos, sin_signed, feature_maps, kv, batch, seq):
    pb, ps, k_side, q_side = _mixer_in(x2, layer, mix["mix_gain"], cos, sin_signed, mix["w_proj"],
                                       mix["fgate_bias"], feature_maps, batch, seq)
    o_fox = _fox_attention(pb, k_side, q_side, batch, seq)
    o_dsa = _dsa_attention(pb, ps, batch, seq)
    o_sb = _sb_attention(pb, batch, seq)
    return _mixer_out_cross(x2, layer, mix, o_fox, o_dsa, o_sb, kv, batch, seq)


def kernel(x, mem, positions, ffn1_norm, ffn1_w_gu, ffn1_w_down, mix_norm, w_in, b_fgate, w_fox_out, w_dsa_out,
           w_sb_out, w_out, ca_norm, mem_norm, ca_w_q, ca_w_kv, ca_w_o, ffn2_norm, ffn2_w_gu, ffn2_w_down,
           final_norm):
    batch, seq, d = x.shape
    depth = ffn1_norm.shape[0]
    x2 = x.reshape(batch * seq, d)
    mem2 = mem.reshape(-1, d)
    pos2 = positions.reshape(batch * seq, 1).astype(I32)
    half = HEAD_DIM // 2
    inv_freq = jnp.power(ROPE_THETA, -jnp.arange(half, dtype=F32) * (2.0 / HEAD_DIM))
    invf = jnp.tile(inv_freq, LANES // half).reshape(1, LANES)
    cos, sin_signed = _rope_tables(pos2, invf)
    feature_maps = _fox_feature_maps()
    mix = _stacked_parameters(mix_norm, w_in, b_fgate, w_fox_out, w_dsa_out, w_sb_out, w_out, ca_norm, ca_w_q,
                              ca_w_o)
    gain = lambda g: g.reshape(depth, 1, d)
    ffn1 = (gain(ffn1_norm), ffn1_w_gu.astype(BF16), ffn1_w_down.astype(BF16))
    ffn2 = (gain(ffn2_norm), ffn2_w_gu.astype(BF16), ffn2_w_down.astype(BF16))
    mem_gain, w_kv = gain(mem_norm), ca_w_kv.astype(BF16)
    for l in range(depth):
        x2 = _ffn(x2, l, *ffn1)
        kv = _norm_matmul(mem2, l, mem_gain, w_kv, BF16)
        x2 = _mixer_and_cross(x2, l, mix, cos, sin_signed, feature_maps, kv, batch, seq)
        x2 = _ffn(x2, l, *ffn2, final_gain=final_norm if l == depth - 1 else None)
    return x2.reshape(batch, seq, d)
```

```python
import functools

import jax
import jax.numpy as jnp
import numpy as np
from jax import lax
from jax.experimental import pallas as pl
from jax.experimental.pallas import tpu as pltpu

F32 = jnp.float32
BF16 = jnp.bfloat16
I32 = jnp.int32

D_MODEL = 1024
HEAD_DIM = 64
N_FOX = 6
N_DSA = 5
N_SB = 5
N_IDX_HEADS = 8
IDX_DIM = 64
TOPK_MAX = 256
N_CA_HEADS = 4
CA_HEAD_DIM = 128
D_FF = 2816
ROPE_THETA = 10000.0
NORM_EPS = 1e-6
N_BRANCH = 3
HALF_STEP = 0.5
FOX_W = N_FOX * HEAD_DIM
DSA_W = N_DSA * HEAD_DIM
SB_W = N_SB * HEAD_DIM
CA_W = N_CA_HEADS * CA_HEAD_DIM
IN_SPLITS = (3 * FOX_W, N_FOX, 3 * DSA_W, N_IDX_HEADS * IDX_DIM, IDX_DIM, N_IDX_HEADS, 3 * SB_W,
             N_BRANCH * D_MODEL)

LANES = 128
HEADS_PER_BLOCK = LANES // HEAD_DIM
PAIR_W = 3 * LANES
VMEM_LIMIT = 56 * 2**20

BLK_DSA_Q, BLK_DSA_K = 0, 3
BLK_IDX_KA = 6
BLK_IDX_KB = 7
BLK_IDX_Q = 8
N_ROPE_BLOCKS = 12
BLK_FOX_Q, BLK_FOX_K, BLK_FOX_V = 12, 15, 18
BLK_DSA_V = 21
BLK_SB_Q, BLK_SB_K, BLK_SB_V = 24, 27, 30
N_PROJ_BLOCKS = 33
PROJ_CHUNK = 4 * LANES
SMALL_FGATE_LANE = 0
SMALL_IDXW_LANE = 8

NEG_BIG = -1e30
SB_CUTOFF = -110.0
INT_MIN = -2**31


def _params(n_grid):
    return pltpu.CompilerParams(dimension_semantics=("arbitrary",) * n_grid, vmem_limit_bytes=VMEM_LIMIT)


def _resident(shape):
    nd = len(shape)
    return pl.BlockSpec(shape, lambda *_: (0,) * nd, pipeline_mode=pl.Buffered(1))


def _token_tile(n, largest):
    tile = largest
    while n % tile:
        tile //= 2
    return tile


def _layer(shape, layer):
    nd = len(shape)
    return pl.BlockSpec((None,) + tuple(shape), lambda *_: (layer,) + (0,) * nd, pipeline_mode=pl.Buffered(1))


def _rms(x, g):
    return x * lax.rsqrt(jnp.mean(x * x, axis=-1, keepdims=True) + NORM_EPS) * g


def _dot(a, b):
    return jnp.dot(a, b, preferred_element_type=F32)


def _dot_nt(a, b):
    return lax.dot_general(a, b, (((1,), (1,)), ((), ())), preferred_element_type=F32)


def _split3(x):
    hi = x.astype(BF16)
    r = x - hi.astype(F32)
    mid = r.astype(BF16)
    lo = (r - mid.astype(F32)).astype(BF16)
    return hi, mid, lo


def _log1p_exp_neg_abs(z):
    return jnp.log1p(jnp.exp(-jnp.abs(z)))


def _ffn_chunks():
    out, c = [], 0
    while c < D_FF:
        w = min(512, D_FF - c)
        out.append((c, w))
        c += w
    return tuple(out)


def _ffn_kernel(*refs, final):
    if final:
        x_ref, g_ref, wgu_ref, wd_ref, fn_ref, o_ref, a_ref = refs
    else:
        x_ref, g_ref, wgu_ref, wd_ref, o_ref, a_ref = refs
    x = x_ref[...]
    h = _rms(x, g_ref[...]).astype(BF16)
    for c0, w in _ffn_chunks():
        g = _dot(h, wgu_ref[:, c0:c0 + w])
        u = _dot(h, wgu_ref[:, D_FF + c0:D_FF + c0 + w])
        a_ref[:, c0:c0 + w] = (g * jax.nn.sigmoid(g) * u).astype(BF16)
    y = x + HALF_STEP * _dot(a_ref[...], wd_ref[...])
    if final:
        y = _rms(y, fn_ref[...])
    o_ref[...] = y


def _ffn(x2, layer, gains, w_gu, w_down, final_gain=None):
    m, d = x2.shape
    tm = _token_tile(m, 1024)
    final = final_gain is not None
    in_specs = [pl.BlockSpec((tm, d), lambda i: (i, 0)), _layer((1, d), layer),
                _layer((d, 2 * D_FF), layer), _layer((D_FF, d), layer)]
    args = [x2, gains, w_gu, w_down]
    if final:
        in_specs.append(_resident((1, d)))
        args.append(final_gain.reshape(1, d))
    return pl.pallas_call(
        functools.partial(_ffn_kernel, final=final),
        grid=(m // tm,),
        in_specs=in_specs,
        out_specs=pl.BlockSpec((tm, d), lambda i: (i, 0)),
        out_shape=jax.ShapeDtypeStruct((m, d), F32),
        scratch_shapes=[pltpu.VMEM((tm, D_FF), BF16)],
        compiler_params=_params(1),
        name="ffn",
    )(*args)


def _first_half_lanes():
    lane = lax.broadcasted_iota(I32, (1, LANES), 1)
    return (lane % HEAD_DIM) < (HEAD_DIM // 2)


def _rope_tables_kernel(pos_ref, invf_ref, cos_ref, sin_ref):
    ang = pos_ref[...].astype(F32) * invf_ref[...]
    sin = jnp.sin(ang)
    cos_ref[...] = jnp.cos(ang)
    sin_ref[...] = jnp.where(_first_half_lanes(), -sin, sin)


def _rope_tables(pos2, invf):
    m = pos2.shape[0]
    tm = _token_tile(m, 512)
    tok = pl.BlockSpec((tm, LANES), lambda i: (i, 0))
    return pl.pallas_call(
        _rope_tables_kernel,
        grid=(m // tm,),
        in_specs=[pl.BlockSpec((tm, 1), lambda i: (i, 0)), _resident((1, LANES))],
        out_specs=[tok, tok],
        out_shape=[jax.ShapeDtypeStruct((m, LANES), F32)] * 2,
        compiler_params=_params(1),
        name="rope_tables",
    )(pos2, invf)


FOX_FEAT_ROWS = 16


def _fox_feature_maps():
    k_map = np.zeros((3, LANES, FOX_W), np.float32)
    q_map = np.zeros((3, LANES, LANES), np.float32)
    k_one = np.zeros((1, FOX_W), np.float32)
    q_one = np.zeros((1, LANES), np.float32)
    for h in range(N_FOX):
        p, hh = divmod(h, HEADS_PER_BLOCK)
        for j in range(3):
            k_map[j, h, p * LANES + 3 * hh + j] = -1.0
            q_map[j, h, FOX_FEAT_ROWS * h + 6 + j] = 1.0
            k_one[0, p * LANES + 6 + j] = 1.0
            q_one[0, FOX_FEAT_ROWS * h + 3 * hh + j] = 1.0
    return (jnp.asarray(k_map, BF16), jnp.asarray(q_map, BF16), jnp.asarray(k_one), jnp.asarray(q_one))


def _fox_decay_features(logits, b_ref, kmap_ref, qmap_ref, kone_ref, qone_ref, kf_ref, qf_ref, carry_ref):
    tc = logits.shape[0]
    z = logits + b_ref[...]
    log_f = jnp.minimum(z, 0.0) - _log1p_exp_neg_abs(z)
    row = lax.broadcasted_iota(I32, (tc, tc), 0)
    col = lax.broadcasted_iota(I32, (tc, tc), 1)
    tri = jnp.where(col <= row, 1.0, 0.0).astype(BF16)
    hi, mid, lo = _split3(log_f)
    yield
    cs = _dot(tri, hi) + _dot(tri, mid) + _dot(tri, lo) + carry_ref[...]
    carry_ref[...] = cs[tc - 1:tc, :]
    terms = _split3(cs)
    yield
    kf = kone_ref[...]
    qf = qone_ref[...]
    for j in range(3):
        kf = kf + _dot(terms[j], kmap_ref[j])
        qf = qf + _dot(terms[j], qmap_ref[j])
    kf_ref[...] = kf.astype(BF16)
    qf_ref[...] = qf.T.astype(BF16)


def _mixer_in_kernel(x_ref, g_ref, cos_ref, sin_ref, wb_ref, b_ref, kmap_ref, qmap_ref, kone_ref, qone_ref,
                     pb_ref, ps_ref, kf_ref, qf_ref, carry_ref, *, tiles_per_seq):
    @pl.when(pl.program_id(0) % tiles_per_seq == 0)
    def _():
        carry_ref[...] = jnp.zeros_like(carry_ref)

    h = _rms(x_ref[...], g_ref[...]).astype(BF16)
    cos = cos_ref[...]
    sin_signed = sin_ref[...]
    first_half = _first_half_lanes()
    n_cols = (N_PROJ_BLOCKS + 1) * LANES
    starts = list(range(0, n_cols, PROJ_CHUNK))
    decay_stages = iter(())
    for c0 in starts[-1:] + starts[:-1]:
        c1 = min(c0 + PROJ_CHUNK, n_cols)
        res = _dot(h, wb_ref[:, c0:c1])
        next(decay_stages, None)
        if c1 == n_cols:
            small = res[:, c1 - c0 - LANES:]
            ps_ref[...] = small
            decay_stages = _fox_decay_features(small, b_ref, kmap_ref, qmap_ref, kone_ref, qone_ref, kf_ref,
                                               qf_ref, carry_ref)
            next(decay_stages)
            c1 -= LANES
            res = res[:, :c1 - c0]
        if c0 < N_ROPE_BLOCKS * LANES:
            parts = []
            for b in range((c1 - c0) // LANES):
                xb = res[:, b * LANES:(b + 1) * LANES]
                partner = jnp.where(first_half, pltpu.roll(xb, LANES - HEAD_DIM // 2, 1),
                                    pltpu.roll(xb, HEAD_DIM // 2, 1))
                parts.append(xb * cos + partner * sin_signed)
            res = jnp.concatenate(parts, axis=1)
        pb_ref[:, c0:c1] = res.astype(BF16)


def _mixer_in(x2, layer, gains, cos, sin_signed, wb, bias_rows, feature_maps, batch, seq):
    m, d = x2.shape
    tm = _token_tile(seq, 512)
    nb = N_PROJ_BLOCKS * LANES
    tok = lambda w: pl.BlockSpec((tm, w), lambda i: (i, 0))
    return pl.pallas_call(
        functools.partial(_mixer_in_kernel, tiles_per_seq=seq // tm),
        grid=(m // tm,),
        in_specs=[tok(d), _layer((1, d), layer), tok(LANES), tok(LANES), _layer((d, nb + LANES), layer),
                  _layer((1, LANES), layer), _resident((3, LANES, FOX_W)), _resident((3, LANES, LANES)),
                  _resident((1, FOX_W)), _resident((1, LANES))],
        out_specs=[tok(nb), tok(LANES), tok(FOX_W), pl.BlockSpec((LANES, tm), lambda i: (0, i))],
        out_shape=[jax.ShapeDtypeStruct((m, nb), BF16), jax.ShapeDtypeStruct((m, LANES), F32),
                   jax.ShapeDtypeStruct((m, FOX_W), BF16), jax.ShapeDtypeStruct((LANES, m), BF16)],
        scratch_shapes=[pltpu.VMEM((1, LANES), F32)],
        compiler_params=_params(1),
        name="mixer_in",
    )(x2, gains, cos, sin_signed, wb, bias_rows, *feature_maps)


def _head_lane_mask(hh):
    lane = lax.broadcasted_iota(I32, (1, LANES), 1)
    return (lane < HEAD_DIM) if hh == 0 else (lane >= HEAD_DIM)


def _head_row_mask(hh):
    row = lax.broadcasted_iota(I32, (LANES, 1), 0)
    return (row < HEAD_DIM) if hh == 0 else (row >= HEAD_DIM)


def _transpose_bf16(x):
    return x.astype(F32).T.astype(BF16)


def _fill_transposed(src_ref, dst_ref):
    n_blocks, _, tk = dst_ref.shape
    for j in range(n_blocks):
        dst_ref[j] = _transpose_bf16(src_ref[j * tk:(j + 1) * tk, :])


V_ROWS = HEAD_DIM + 16


def _fill_transposed_with_ones(src_ref, dst_ref, n_heads):
    n_blocks, _, tk = dst_ref.shape
    ones_pad = jnp.where(lax.broadcasted_iota(I32, (V_ROWS - HEAD_DIM, tk), 0) == 0, 1.0, 0.0).astype(BF16)
    for j in range(n_blocks):
        v_t = _transpose_bf16(src_ref[j * tk:(j + 1) * tk, :])
        for h in range(n_heads):
            dst_ref[j, h * V_ROWS:h * V_ROWS + HEAD_DIM, :] = v_t[h * HEAD_DIM:(h + 1) * HEAD_DIM, :]
            dst_ref[j, h * V_ROWS + HEAD_DIM:(h + 1) * V_ROWS, :] = ones_pad


def _softmax_steps(carry, scores, values_t):
    stats = []
    for (m, _), s in zip(carry, scores):
        m_new = jnp.maximum(m, jnp.max(s, axis=0, keepdims=True))
        stats.append((m_new, jnp.exp(m - m_new), jnp.exp(s - m_new).astype(BF16)))
    return tuple((m_new, alpha * acc + _dot(v_t, p))
                 for (m_new, alpha, p), (_, acc), v_t in zip(stats, carry, values_t))


def _softmax_init(tq):
    return (jnp.full((1, tq), NEG_BIG, F32), jnp.zeros((V_ROWS, tq), F32))


def _softmax_result(state):
    _, acc = state
    return acc[:HEAD_DIM, :] / acc[HEAD_DIM:HEAD_DIM + 1, :]


def _flash_blocks(n, score_fn, value_fn, carry, s_ref):
    n_heads = len(carry)

    def load(slot):
        return [s_ref[slot, h] for h in range(n_heads)]

    def store(slot, kb):
        for h, s in enumerate(score_fn(kb)):
            s_ref[slot, h] = s

    odd = n % 2
    carry = lax.cond(odd == 1, lambda c: _softmax_steps(c, score_fn(0), value_fn(0)), lambda c: c, carry)

    @pl.when(n >= 2)
    def _():
        store(0, odd)

    def body(j, c):
        kb = odd + 2 * j
        store(1, kb + 1)
        c = _softmax_steps(c, load(0), value_fn(kb))
        store(0, jnp.minimum(kb + 2, n - 1))
        return _softmax_steps(c, load(1), value_fn(kb + 1))

    return lax.fori_loop(0, n // 2, body, carry)


def _heads_to_token_rows(out_t):
    return jnp.concatenate(out_t, axis=0).T.astype(BF16)


def _fox_kernel(q_ref, k_ref, v_ref, kf_ref, qf_ref, o_ref, vt_ref, s_ref):
    tq = q_ref.shape[0]
    tk = vt_ref.shape[2]
    i = pl.program_id(1)

    @pl.when(i == 0)
    def _():
        _fill_transposed_with_ones(v_ref, vt_ref, N_FOX)

    key_pos = lax.broadcasted_iota(I32, (tk, tq), 0)
    qry_pos = lax.broadcasted_iota(I32, (tk, tq), 1)
    causal = [key_pos + d * tk <= qry_pos for d in range(2)]
    pad = jnp.zeros((LANES - FOX_FEAT_ROWS, tq), BF16)
    qx = []
    for p in range(FOX_W // LANES):
        q_t = q_ref[:, p * LANES:(p + 1) * LANES].astype(F32).T
        for hh in range(HEADS_PER_BLOCK):
            qx.append(jnp.concatenate([jnp.where(_head_row_mask(hh), q_t, 0.0).astype(BF16),
                                       qf_ref[(p * HEADS_PER_BLOCK + hh) * FOX_FEAT_ROWS:
                                              (p * HEADS_PER_BLOCK + hh + 1) * FOX_FEAT_ROWS, :], pad], axis=0))

    def store(slot, kb):
        k0 = pl.multiple_of(kb * tk, tk)
        for h in range(N_FOX):
            p = h // HEADS_PER_BLOCK
            kx = jnp.concatenate([k_ref[pl.ds(k0, tk), p * LANES:(p + 1) * LANES],
                                  kf_ref[pl.ds(k0, tk), p * LANES:(p + 1) * LANES]], axis=1)
            s_ref[slot, h] = _dot(kx, qx[h])

    def step(carry, slot, kb, mask=None):
        scores = [s_ref[slot, h] for h in range(N_FOX)]
        if mask is not None:
            scores = [jnp.where(mask, s, NEG_BIG) for s in scores]
        return _softmax_steps(carry, scores, [vt_ref[kb, h * V_ROWS:(h + 1) * V_ROWS, :] for h in range(N_FOX)])

    def trip(j, carry):
        kb = 2 * j
        store(1, kb + 1)
        carry = step(carry, 0, kb)
        store(0, kb + 2)
        return step(carry, 1, kb + 1)

    store(0, 0)
    carry = lax.fori_loop(0, i, trip, tuple(_softmax_init(tq) for _ in range(N_FOX)))
    kb = 2 * i
    store(1, kb + 1)
    carry = step(carry, 0, kb, causal[0])
    carry = step(carry, 1, kb + 1, causal[1])
    for p in range(FOX_W // LANES):
        o_ref[:, p * LANES:(p + 1) * LANES] = _heads_to_token_rows(
            [_softmax_result(st) for st in carry[p * HEADS_PER_BLOCK:(p + 1) * HEADS_PER_BLOCK]])


def _fox_attention(pb, k_side, q_side, batch, seq):
    tq = min(512, seq)
    tk = tq // 2
    assert seq % tq == 0
    nq = seq // tq
    m = batch * seq
    return pl.pallas_call(
        _fox_kernel,
        grid=(batch, nq),
        in_specs=[pl.BlockSpec((tq, FOX_W), lambda b, i: (b * nq + i, BLK_FOX_Q // 3)),
                  pl.BlockSpec((seq, FOX_W), lambda b, i: (b, BLK_FOX_K // 3)),
                  pl.BlockSpec((seq, FOX_W), lambda b, i: (b, BLK_FOX_V // 3)),
                  pl.BlockSpec((seq, FOX_W), lambda b, i: (b, 0)),
                  pl.BlockSpec((LANES, tq), lambda b, i: (0, b * nq + i))],
        out_specs=pl.BlockSpec((tq, FOX_W), lambda b, i: (b * nq + i, 0)),
        out_shape=jax.ShapeDtypeStruct((m, FOX_W), BF16),
        scratch_shapes=[pltpu.VMEM((seq // tk, N_FOX * V_ROWS, tk), BF16), pltpu.VMEM((2, N_FOX, tk, tq), F32)],
        compiler_params=_params(2),
        name="fox_attention",
    )(pb, pb, pb, k_side, q_side)


def _sb_kernel(q_ref, k_ref, v_ref, o_ref, vt_ref):
    tq = q_ref.shape[0]
    tk = vt_ref.shape[2]
    i = pl.program_id(1)

    @pl.when(i == 0)
    def _():
        _fill_transposed(v_ref, vt_ref)

    heads = range(N_SB)
    q_t = [q_ref[:, p * LANES:(p + 1) * LANES].astype(F32).T for p in range(PAIR_W // LANES)]
    qh = [jnp.where(_head_row_mask(h % HEADS_PER_BLOCK), q_t[h // HEADS_PER_BLOCK], 0.0).astype(BF16) for h in heads]
    r = lax.broadcasted_iota(I32, (tk, tk), 0)
    c = lax.broadcasted_iota(I32, (tk, tk), 1)
    upper = jnp.where(c > r, 1.0, 0.0).astype(BF16)
    key_pos = lax.broadcasted_iota(I32, (tk, tq), 0)
    qry_pos = lax.broadcasted_iota(I32, (tk, tq), 1) + i * tq

    def process(kb, state, masked):
        k0 = pl.multiple_of(kb * tk, tk)
        strict = key_pos + k0 < qry_pos
        z = [_dot(k_ref[pl.ds(k0, tk), (h // HEADS_PER_BLOCK) * LANES:(h // HEADS_PER_BLOCK + 1) * LANES], qh[h])
             for h in heads]
        log_beta, log_1m = [], []
        for zh in z:
            l1m = jnp.minimum(-zh, 0.0) - jnp.log(1.0 + jnp.exp(-jnp.abs(zh)))
            log_beta.append(l1m + zh)
            log_1m.append(jnp.where(strict, l1m, 0.0) if masked else l1m)
        after = []
        for h in heads:
            hi, mid, lo = _split3(log_1m[h])
            after.append(_dot(upper, hi) + _dot(upper, mid) + _dot(upper, lo) + state[h][0])
        weights = []
        for h in heads:
            a = jnp.exp(log_beta[h] + after[h])
            weights.append((jnp.where(strict, a, 0.0) if masked else a).astype(BF16))
        return tuple((state[h][0] + jnp.sum(log_1m[h], axis=0, keepdims=True),
                      state[h][1] + _dot(vt_ref[kb, h * HEAD_DIM:(h + 1) * HEAD_DIM, :], weights[h]))
                     for h in heads)

    def run_max(state):
        m = state[0][0]
        for run, _ in state[1:]:
            m = jnp.maximum(m, run)
        return jnp.max(m)

    state = tuple((jnp.zeros((1, tq), F32), jnp.zeros((HEAD_DIM, tq), F32)) for _ in heads)
    n_diag = tq // tk
    for d in range(n_diag):
        state = process((i + 1) * n_diag - 1 - d, state, True)

    def cond(carry):
        kb, worst, _ = carry
        return jnp.logical_and(kb >= 0, worst > SB_CUTOFF)

    def body(carry):
        kb, _, st = carry
        st = process(kb, st, False)
        return kb - 1, run_max(st), st

    _, _, state = lax.while_loop(cond, body, (i * n_diag - 1, run_max(state), state))
    for p in range(PAIR_W // LANES):
        outs = [state[h][1] for h in heads if h // HEADS_PER_BLOCK == p]
        outs += [jnp.zeros((HEAD_DIM, tq), F32)] * (HEADS_PER_BLOCK - len(outs))
        o_ref[:, p * LANES:(p + 1) * LANES] = _heads_to_token_rows(outs)


def _sb_attention(pb, batch, seq):
    tq = min(256, seq)
    tk = min(256, seq)
    assert seq % tq == 0 and tq % tk == 0
    nq = seq // tq
    m = batch * seq
    return pl.pallas_call(
        _sb_kernel,
        grid=(batch, nq),
        in_specs=[pl.BlockSpec((tq, PAIR_W), lambda b, i: (b * nq + i, BLK_SB_Q // 3)),
                  pl.BlockSpec((seq, PAIR_W), lambda b, i: (b, BLK_SB_K // 3)),
                  pl.BlockSpec((seq, PAIR_W), lambda b, i: (b, BLK_SB_V // 3))],
        out_specs=pl.BlockSpec((tq, PAIR_W), lambda b, i: (b * nq + i, 0)),
        out_shape=jax.ShapeDtypeStruct((m, PAIR_W), BF16),
        scratch_shapes=[pltpu.VMEM((seq // tk, PAIR_W, tk), BF16)],
        compiler_params=_params(2),
        name="sb_attention",
    )(pb, pb, pb)


HALF_BITS = 16
HALF_OFFSET = 1 << (HALF_BITS - 1)


def _dsa_kernel(q_ref, iq_ref, w_ref, k_ref, v_ref, ka_ref, kb_ref, o_ref, key_ref, hi_ref, lo_ref, bias_ref,
                vt_ref, s_ref, *, topk, seq):
    t = q_ref.shape[0]
    i = pl.program_id(1)

    @pl.when(i == 0)
    def _():
        _fill_transposed_with_ones(v_ref, vt_ref, N_DSA)

    n_kb = i + 1
    kf = float(topk)
    imin = jnp.int32(INT_MIN)
    key_pos = lax.broadcasted_iota(I32, (t, t), 0)
    qry_pos = lax.broadcasted_iota(I32, (t, t), 1) + i * t

    iq_t = [_transpose_bf16(iq_ref[:, g * LANES:(g + 1) * LANES]) for g in range(N_IDX_HEADS // 2)]
    w_t = (w_ref[...] * (N_IDX_HEADS ** -0.5) * (IDX_DIM ** -0.5)).T

    def index_block(kb):
        k0 = pl.multiple_of(kb * t, t)
        ka = ka_ref[pl.ds(k0, t), :]
        kb_ = kb_ref[pl.ds(k0, t), :]
        score = jnp.zeros((t, t), F32)
        for g in range(N_IDX_HEADS // 2):
            lo = SMALL_IDXW_LANE + 2 * g
            score = score + jnp.maximum(_dot(ka, iq_t[g]), 0.0) * w_t[lo:lo + 1, :]
            score = score + jnp.maximum(_dot(kb_, iq_t[g]), 0.0) * w_t[lo + 1:lo + 2, :]
        bits = lax.bitcast_convert_type(score + 0.0, I32)
        key = jnp.where(bits < 0, bits ^ jnp.int32(0x7FFFFFFF), bits)
        key = jnp.where(key_pos + k0 <= qry_pos, key, imin)
        key_ref[pl.ds(k0, t), :] = key
        hi_ref[pl.ds(k0, t), :] = lax.shift_right_arithmetic(key, HALF_BITS).astype(jnp.int16)
        lo_ref[pl.ds(k0, t), :] = ((key & (2 * HALF_OFFSET - 1)) - HALF_OFFSET).astype(jnp.int16)

    odd = n_kb % 2

    @pl.when(odd == 1)
    def _():
        index_block(0)

    def index_pair(j, _):
        index_block(odd + 2 * j)
        index_block(odd + 2 * j + 1)
        return 0

    lax.fori_loop(0, n_kb // 2, index_pair, 0)

    masked16 = jnp.full((t, t), -HALF_OFFSET, jnp.int16)

    @pl.when(odd == 1)
    def _():
        k0 = pl.multiple_of(n_kb * t, t)
        hi_ref[pl.ds(k0, t), :] = masked16
        lo_ref[pl.ds(k0, t), :] = masked16

    n_scan = (n_kb + 1) // 2
    scan_rows = 2 * t

    def count16(ref, pred):
        rows = 16
        def body(trip, acc):
            k0 = pl.multiple_of(trip * scan_rows, scan_rows)
            ind = pred(ref[pl.ds(k0, scan_rows), :])
            parts = [ind[r:r + rows, :] for r in range(0, scan_rows, rows)]
            while len(parts) > 1:
                parts = [parts[j] + parts[j + 1] for j in range(0, len(parts), 2)]
            return acc + parts[0]
        acc = lax.fori_loop(0, n_scan, body, jnp.zeros((rows, t), jnp.int16))
        return jnp.sum(acc.astype(F32), axis=0, keepdims=True)

    one16, zero16 = jnp.int16(1), jnp.int16(0)

    def kth_half(ref, need, n_all):
        def bit_step(it, state):
            prefix, cnt_prefix = state
            cand_u = prefix | lax.shift_left(jnp.int32(1), HALF_BITS - 1 - it)
            cand = (cand_u - HALF_OFFSET).astype(jnp.int16)
            cnt = count16(ref, lambda blk: jnp.where(blk >= cand, one16, zero16))
            take = cnt >= need
            return jnp.where(take, cand_u, prefix), jnp.where(take, cnt, cnt_prefix)
        prefix, cnt = lax.fori_loop(0, HALF_BITS, bit_step, (jnp.zeros((1, t), I32), n_all))
        return prefix - HALF_OFFSET, cnt


    n_scanned = (n_scan * scan_rows).astype(F32)
    kth_hi, n_ge_hi = kth_half(hi_ref, kf, jnp.full((1, t), n_scanned, F32))
    kth_hi16 = kth_hi.astype(jnp.int16)

    def above_and_bucket(trip, acc):
        rows = 16
        k0 = pl.multiple_of(trip * scan_rows, scan_rows)
        hi = hi_ref[pl.ds(k0, scan_rows), :]
        lo_ref[pl.ds(k0, scan_rows), :] = jnp.where(hi == kth_hi16, lo_ref[pl.ds(k0, scan_rows), :],
                                                    jnp.int16(-HALF_OFFSET))
        ind = jnp.where(hi > kth_hi16, one16, zero16)
        parts = [ind[r:r + rows, :] for r in range(0, scan_rows, rows)]
        while len(parts) > 1:
            parts = [parts[j] + parts[j + 1] for j in range(0, len(parts), 2)]
        return acc + parts[0]

    n_gt_hi = jnp.sum(lax.fori_loop(0, n_scan, above_and_bucket, jnp.zeros((16, t), jnp.int16)).astype(F32),
                      axis=0, keepdims=True)
    need_lo = kf - n_gt_hi
    kth_lo, n_ge_lo = kth_half(lo_ref, need_lo, n_ge_hi - n_gt_hi)
    kth = lax.shift_left(kth_hi, HALF_BITS) + (kth_lo + HALF_OFFSET)

    excess = jnp.where(kth == imin, float(2 * seq), n_ge_lo - need_lo)
    any_excess = jnp.max(excess) > 0.0

    @pl.when(jnp.logical_not(any_excess))
    def _():
        def select(kb, _):
            k0 = pl.multiple_of(kb * t, t)
            bias_ref[pl.ds(k0, t), :] = jnp.where(key_ref[pl.ds(k0, t), :] >= kth, 0.0, NEG_BIG)
            return 0

        lax.fori_loop(0, n_kb, select, 0)

    @pl.when(any_excess)
    def _():
        r = lax.broadcasted_iota(I32, (t, t), 0)
        c = lax.broadcasted_iota(I32, (t, t), 1)
        later = jnp.where(c > r, 1.0, 0.0).astype(BF16)

        def select(step, tied_after):
            kb = n_kb - 1 - step
            k0 = pl.multiple_of(kb * t, t)
            kblk = key_ref[pl.ds(k0, t), :]
            tied = kblk == kth
            tied01 = jnp.where(tied, 1.0, 0.0).astype(BF16)
            after = _dot(later, tied01) + tied_after
            tie_bias = jnp.where(tied, jnp.where(after >= excess, 0.0, NEG_BIG), NEG_BIG)
            bias_ref[pl.ds(k0, t), :] = jnp.where(kblk > kth, 0.0, tie_bias)
            return after[0:1, :] + tied01[0:1, :].astype(F32)

        lax.fori_loop(0, n_kb, select, jnp.zeros((1, t), F32))

    q_t = [q_ref[:, p * LANES:(p + 1) * LANES].astype(F32).T for p in range(PAIR_W // LANES)]
    qh = [jnp.where(_head_row_mask(h % HEADS_PER_BLOCK), q_t[h // HEADS_PER_BLOCK], 0.0).astype(BF16)
          for h in range(N_DSA)]

    def scores(kb):
        k0 = pl.multiple_of(kb * t, t)
        bias = bias_ref[pl.ds(k0, t), :]
        return [_dot(k_ref[pl.ds(k0, t), (h // HEADS_PER_BLOCK) * LANES:(h // HEADS_PER_BLOCK + 1) * LANES], qh[h])
                + bias for h in range(N_DSA)]

    def values_t(kb):
        return [vt_ref[kb, h * V_ROWS:(h + 1) * V_ROWS, :] for h in range(N_DSA)]

    carry = _flash_blocks(n_kb, scores, values_t, tuple(_softmax_init(t) for _ in range(N_DSA)), s_ref)
    outs = [_softmax_result(st) for st in carry]
    outs += [jnp.zeros((HEAD_DIM, t), F32)] * (PAIR_W // HEAD_DIM - N_DSA)
    for p in range(PAIR_W // LANES):
        o_ref[:, p * LANES:(p + 1) * LANES] = _heads_to_token_rows(
            outs[p * HEADS_PER_BLOCK:(p + 1) * HEADS_PER_BLOCK])


def _dsa_attention(pb, ps, batch, seq):
    t = min(256, seq)
    assert seq % t == 0
    nq = seq // t
    m = batch * seq
    topk = min(TOPK_MAX, seq // 4)
    return pl.pallas_call(
        functools.partial(_dsa_kernel, topk=topk, seq=seq),
        grid=(batch, nq),
        in_specs=[pl.BlockSpec((t, PAIR_W), lambda b, i: (b * nq + i, BLK_DSA_Q // 3)),
                  pl.BlockSpec((t, 4 * LANES), lambda b, i: (b * nq + i, BLK_IDX_Q // 4)),
                  pl.BlockSpec((t, LANES), lambda b, i: (b * nq + i, 0)),
                  pl.BlockSpec((seq, PAIR_W), lambda b, i: (b, BLK_DSA_K // 3)),
                  pl.BlockSpec((seq, PAIR_W), lambda b, i: (b, BLK_DSA_V // 3)),
                  pl.BlockSpec((seq, LANES), lambda b, i: (b, BLK_IDX_KA)),
                  pl.BlockSpec((seq, LANES), lambda b, i: (b, BLK_IDX_KB))],
        out_specs=pl.BlockSpec((t, PAIR_W), lambda b, i: (b * nq + i, 0)),
        out_shape=jax.ShapeDtypeStruct((m, PAIR_W), BF16),
        scratch_shapes=[pltpu.VMEM((seq, t), I32), pltpu.VMEM((seq + t, t), jnp.int16),
                        pltpu.VMEM((seq + t, t), jnp.int16),
                        pltpu.VMEM((seq, t), F32),
                        pltpu.VMEM((nq, N_DSA * V_ROWS, t), BF16), pltpu.VMEM((2, N_DSA, t, t), F32)],
        compiler_params=_params(2),
        name="dsa_attention",
    )(pb, pb, ps, pb, pb, pb, pb)


def _mixer_out_cross_kernel(x_ref, gain_ref, of_ref, od_ref, os_ref, wg_ref, wf_ref, wd_ref, ws_ref, wo_ref,
                            ca_gain_ref, ca_wq_ref, kv_ref, ca_wo_ref, o_ref):
    d = x_ref.shape[1]
    x = x_ref[...]
    h = _rms(x, gain_ref[...]).astype(BF16)
    merged = None
    for b, (o_b, w_b) in enumerate(((of_ref, wf_ref), (od_ref, wd_ref), (os_ref, ws_ref))):
        term = jax.nn.sigmoid(_dot(h, wg_ref[:, b * d:(b + 1) * d])) * _dot(o_b[...], w_b[...])
        merged = term if merged is None else merged + term
    x = x + _dot(merged.astype(BF16), wo_ref[...])

    q = _dot(_rms(x, ca_gain_ref[...]).astype(BF16), ca_wq_ref[...])
    scale = CA_HEAD_DIM ** -0.5
    outs = []
    for hd in range(N_CA_HEADS):
        qh = q[:, hd * CA_HEAD_DIM:(hd + 1) * CA_HEAD_DIM].astype(BF16)
        logits = _dot_nt(qh, kv_ref[:, hd * CA_HEAD_DIM:(hd + 1) * CA_HEAD_DIM]) * scale
        e = jnp.exp(logits - jnp.max(logits, axis=1, keepdims=True))
        pv = _dot(e.astype(BF16), kv_ref[:, CA_W + hd * CA_HEAD_DIM:CA_W + (hd + 1) * CA_HEAD_DIM])
        outs.append(pv / jnp.sum(e, axis=1, keepdims=True))
    o_ref[...] = x + _dot(jnp.concatenate(outs, axis=1).astype(BF16), ca_wo_ref[...])


def _mixer_out_cross(x2, layer, mix, o_fox, o_dsa, o_sb, kv, batch, seq):
    m, d = x2.shape
    tm = _token_tile(seq, 1024)
    n_mem = kv.shape[0] // batch
    per_batch = seq // tm
    row = lambda w: pl.BlockSpec((tm, w), lambda i: (i, 0))
    lay = lambda *shape: _layer(shape, layer)
    return pl.pallas_call(
        _mixer_out_cross_kernel,
        grid=(m // tm,),
        in_specs=[row(d), lay(1, d), row(FOX_W), row(PAIR_W), row(PAIR_W), lay(d, N_BRANCH * d),
                  lay(FOX_W, d), lay(PAIR_W, d), lay(PAIR_W, d), lay(d, d), lay(1, d), lay(d, CA_W),
                  pl.BlockSpec((n_mem, 2 * CA_W), lambda i: (i // per_batch, 0)), lay(CA_W, d)],
        out_specs=row(d),
        out_shape=jax.ShapeDtypeStruct((m, d), F32),
        compiler_params=_params(1),
        name="mixer_out_cross",
    )(x2, mix["mix_gain"], o_fox, o_dsa, o_sb, mix["w_gates"], mix["w_fox_out"], mix["w_dsa_out"], mix["w_sb_out"],
      mix["w_out"], mix["ca_gain"], mix["ca_wq"], kv, mix["ca_wo"])


def _norm_matmul_kernel(x_ref, g_ref, w_ref, o_ref):
    o_ref[...] = _dot(_rms(x_ref[...], g_ref[...]).astype(BF16), w_ref[...]).astype(o_ref.dtype)


def _norm_matmul(x2, layer, gains, w, out_dtype):
    m, d = x2.shape
    n = w.shape[-1]
    tm = _token_tile(m, 512)
    return pl.pallas_call(
        _norm_matmul_kernel,
        grid=(m // tm,),
        in_specs=[pl.BlockSpec((tm, d), lambda i: (i, 0)), _layer((1, d), layer), _layer((d, n), layer)],
        out_specs=pl.BlockSpec((tm, n), lambda i: (i, 0)),
        out_shape=jax.ShapeDtypeStruct((m, n), out_dtype),
        compiler_params=_params(1),
        name="norm_matmul",
    )(x2, gains, w)


def _pad_last(w, n):
    return jnp.pad(w, [(0, 0)] * (w.ndim - 1) + [(0, n - w.shape[-1])])


def _pad_rows(w, n):
    return jnp.pad(w, [(0, 0)] * (w.ndim - 2) + [(0, n - w.shape[-2]), (0, 0)])


def _mixer_weights(w_in):
    offs = np.cumsum((0,) + IN_SPLITS)
    w_in = w_in.astype(BF16)
    part = lambda j: w_in[..., offs[j]:offs[j + 1]]
    thirds = lambda w: jnp.split(w, 3, axis=-1)
    scale = HEAD_DIM ** -0.5
    fq, fk, fv = thirds(part(0))
    dq, dk, dv = thirds(part(2))
    sq, sk, sv = thirds(part(6))
    iq, ik, iw = part(3), part(4), part(5)
    zk = jnp.zeros_like(ik)
    small = jnp.concatenate([part(1), jnp.zeros_like(w_in[..., :SMALL_IDXW_LANE - N_FOX]), iw], axis=-1)
    wb = jnp.concatenate(
        [_pad_last(dq * scale, PAIR_W), _pad_last(dk, PAIR_W),
         jnp.concatenate([ik, zk], axis=-1), jnp.concatenate([zk, ik], axis=-1), iq,
         fq * scale, fk, fv, _pad_last(dv, PAIR_W),
         _pad_last(sq * scale, PAIR_W), _pad_last(sk, PAIR_W), _pad_last(sv, PAIR_W),
         _pad_last(small, LANES)], axis=-1)
    return wb, part(7)


def _stacked_parameters(mix_norm, w_in, b_fgate, w_fox_out, w_dsa_out, w_sb_out, w_out, ca_norm, ca_w_q, ca_w_o):
    depth, d = mix_norm.shape
    w_proj, w_gates = _mixer_weights(w_in)
    return {
        "mix_gain": mix_norm.reshape(depth, 1, d),
        "w_proj": w_proj,
        "w_gates": w_gates,
        "fgate_bias": _pad_last(b_fgate.astype(F32), LANES).reshape(depth, 1, LANES),
        "w_fox_out": w_fox_out.astype(BF16),
        "w_dsa_out": _pad_rows(w_dsa_out, PAIR_W).astype(BF16),
        "w_sb_out": _pad_rows(w_sb_out, PAIR_W).astype(BF16),
        "w_out": w_out.astype(BF16),
        "ca_gain": ca_norm.reshape(depth, 1, d),
        "ca_wq": ca_w_q.astype(BF16),
        "ca_wo": ca_w_o.astype(BF16),
    }


def _mixer_and_cross(x2, layer, mix, cos, sin_signed, feature_maps, kv, batch, seq):
    pb, ps, k_side, q_side = _mixer_in(x2, layer, mix["mix_gain"], cos, sin_signed, mix["w_proj"],
                                       mix["fgate_bias"], feature_maps, batch, seq)
    o_fox = _fox_attention(pb, k_side, q_side, batch, seq)
    o_dsa = _dsa_attention(pb, ps, batch, seq)
    o_sb = _sb_attention(pb, batch, seq)
    return _mixer_out_cross(x2, layer, mix, o_fox, o_dsa, o_sb, kv, batch, seq)


def kernel(x, mem, positions, ffn1_norm, ffn1_w_gu, ffn1_w_down, mix_norm, w_in, b_fgate, w_fox_out, w_dsa_out,
           w_sb_out, w_out, ca_norm, mem_norm, ca_w_q, ca_w_kv, ca_w_o, ffn2_norm, ffn2_w_gu, ffn2_w_down,
           final_norm):
    batch, seq, d = x.shape
    depth = ffn1_norm.shape[0]
    x2 = x.reshape(batch * seq, d)
    mem2 = mem.reshape(-1, d)
    pos2 = positions.reshape(batch * seq, 1).astype(I32)
    half = HEAD_DIM // 2
    inv_freq = jnp.power(ROPE_THETA, -jnp.arange(half, dtype=F32) * (2.0 / HEAD_DIM))
    invf = jnp.tile(inv_freq, LANES // half).reshape(1, LANES)
    cos, sin_signed = _rope_tables(pos2, invf)
    feature_maps = _fox_feature_maps()
    mix = _stacked_parameters(mix_norm, w_in, b_fgate, w_fox_out, w_dsa_out, w_sb_out, w_out, ca_norm, ca_w_q,
                              ca_w_o)
    gain = lambda g: g.reshape(depth, 1, d)
    ffn1 = (gain(ffn1_norm), ffn1_w_gu.astype(BF16), ffn1_w_down.astype(BF16))
    ffn2 = (gain(ffn2_norm), ffn2_w_gu.astype(BF16), ffn2_w_down.astype(BF16))
    mem_gain, w_kv = gain(mem_norm), ca_w_kv.astype(BF16)
    for l in range(depth):
        x2 = _ffn(x2, l, *ffn1)
        kv = _norm_matmul(mem2, l, mem_gain, w_kv, BF16)
        x2 = _mixer_and_cross(x2, l, mix, cos, sin_signed, feature_maps, kv, batch, seq)
        x2 = _ffn(x2, l, *ffn2, final_gain=final_norm if l == depth - 1 else None)
    return x2.reshape(batch, seq, d)
```

```python
import functools

import jax
import jax.numpy as jnp
import numpy as np
from jax import lax
from jax.experimental import pallas as pl
from jax.experimental.pallas import tpu as pltpu

F32 = jnp.float32
BF16 = jnp.bfloat16
I32 = jnp.int32

D_MODEL = 1024
HEAD_DIM = 64
N_FOX = 6
N_DSA = 5
N_SB = 5
N_IDX_HEADS = 8
IDX_DIM = 64
TOPK_MAX = 256
N_CA_HEADS = 4
CA_HEAD_DIM = 128
D_FF = 2816
ROPE_THETA = 10000.0
NORM_EPS = 1e-6
N_BRANCH = 3
HALF_STEP = 0.5
FOX_W = N_FOX * HEAD_DIM
DSA_W = N_DSA * HEAD_DIM
SB_W = N_SB * HEAD_DIM
CA_W = N_CA_HEADS * CA_HEAD_DIM
IN_SPLITS = (3 * FOX_W, N_FOX, 3 * DSA_W, N_IDX_HEADS * IDX_DIM, IDX_DIM, N_IDX_HEADS, 3 * SB_W,
             N_BRANCH * D_MODEL)

LANES = 128
HEADS_PER_BLOCK = LANES // HEAD_DIM
PAIR_W = 3 * LANES
VMEM_LIMIT = 56 * 2**20

BLK_DSA_Q, BLK_DSA_K = 0, 3
BLK_IDX_KA = 6
BLK_IDX_KB = 7
BLK_IDX_Q = 8
N_ROPE_BLOCKS = 12
BLK_FOX_Q, BLK_FOX_K, BLK_FOX_V = 12, 15, 18
BLK_DSA_V = 21
BLK_SB_Q, BLK_SB_K, BLK_SB_V = 24, 27, 30
N_PROJ_BLOCKS = 33
PROJ_CHUNK = 4 * LANES
SMALL_FGATE_LANE = 0
SMALL_IDXW_LANE = 8

NEG_BIG = -1e30
SB_CUTOFF = -110.0
INT_MIN = -2**31


def _params(n_grid):
    return pltpu.CompilerParams(dimension_semantics=("arbitrary",) * n_grid, vmem_limit_bytes=VMEM_LIMIT)


def _resident(shape):
    nd = len(shape)
    return pl.BlockSpec(shape, lambda *_: (0,) * nd, pipeline_mode=pl.Buffered(1))


def _token_tile(n, largest):
    tile = largest
    while n % tile:
        tile //= 2
    return tile


def _layer(shape, layer):
    nd = len(shape)
    return pl.BlockSpec((None,) + tuple(shape), lambda *_: (layer,) + (0,) * nd, pipeline_mode=pl.Buffered(1))


def _rms(x, g):
    return x * lax.rsqrt(jnp.mean(x * x, axis=-1, keepdims=True) + NORM_EPS) * g


def _dot(a, b):
    return jnp.dot(a, b, preferred_element_type=F32)


def _dot_nt(a, b):
    return lax.dot_general(a, b, (((1,), (1,)), ((), ())), preferred_element_type=F32)


def _split3(x):
    hi = x.astype(BF16)
    r = x - hi.astype(F32)
    mid = r.astype(BF16)
    lo = (r - mid.astype(F32)).astype(BF16)
    return hi, mid, lo


def _log1p_exp_neg_abs(z):
    return jnp.log1p(jnp.exp(-jnp.abs(z)))


def _ffn_chunks():
    out, c = [], 0
    while c < D_FF:
        w = min(512, D_FF - c)
        out.append((c, w))
        c += w
    return tuple(out)


def _ffn_kernel(*refs, final):
    if final:
        x_ref, g_ref, wgu_ref, wd_ref, fn_ref, o_ref, a_ref = refs
    else:
        x_ref, g_ref, wgu_ref, wd_ref, o_ref, a_ref = refs
    x = x_ref[...]
    h = _rms(x, g_ref[...]).astype(BF16)
    for c0, w in _ffn_chunks():
        g = _dot(h, wgu_ref[:, c0:c0 + w])
        u = _dot(h, wgu_ref[:, D_FF + c0:D_FF + c0 + w])
        a_ref[:, c0:c0 + w] = (g * jax.nn.sigmoid(g) * u).astype(BF16)
    y = x + HALF_STEP * _dot(a_ref[...], wd_ref[...])
    if final:
        y = _rms(y, fn_ref[...])
    o_ref[...] = y


def _ffn(x2, layer, gains, w_gu, w_down, final_gain=None):
    m, d = x2.shape
    tm = _token_tile(m, 1024)
    final = final_gain is not None
    in_specs = [pl.BlockSpec((tm, d), lambda i: (i, 0)), _layer((1, d), layer),
                _layer((d, 2 * D_FF), layer), _layer((D_FF, d), layer)]
    args = [x2, gains, w_gu, w_down]
    if final:
        in_specs.append(_resident((1, d)))
        args.append(final_gain.reshape(1, d))
    return pl.pallas_call(
        functools.partial(_ffn_kernel, final=final),
        grid=(m // tm,),
        in_specs=in_specs,
        out_specs=pl.BlockSpec((tm, d), lambda i: (i, 0)),
        out_shape=jax.ShapeDtypeStruct((m, d), F32),
        scratch_shapes=[pltpu.VMEM((tm, D_FF), BF16)],
        compiler_params=_params(1),
        name="ffn",
    )(*args)


def _first_half_lanes():
    lane = lax.broadcasted_iota(I32, (1, LANES), 1)
    return (lane % HEAD_DIM) < (HEAD_DIM // 2)


def _rope_tables_kernel(pos_ref, invf_ref, cos_ref, sin_ref):
    ang = pos_ref[...].astype(F32) * invf_ref[...]
    sin = jnp.sin(ang)
    cos_ref[...] = jnp.cos(ang)
    sin_ref[...] = jnp.where(_first_half_lanes(), -sin, sin)


def _rope_tables(pos2, invf):
    m = pos2.shape[0]
    tm = _token_tile(m, 512)
    tok = pl.BlockSpec((tm, LANES), lambda i: (i, 0))
    return pl.pallas_call(
        _rope_tables_kernel,
        grid=(m // tm,),
        in_specs=[pl.BlockSpec((tm, 1), lambda i: (i, 0)), _resident((1, LANES))],
        out_specs=[tok, tok],
        out_shape=[jax.ShapeDtypeStruct((m, LANES), F32)] * 2,
        compiler_params=_params(1),
        name="rope_tables",
    )(pos2, invf)


FOX_FEAT_ROWS = 16


def _fox_feature_maps():
    k_map = np.zeros((3, LANES, FOX_W), np.float32)
    q_map = np.zeros((3, LANES, LANES), np.float32)
    k_one = np.zeros((1, FOX_W), np.float32)
    q_one = np.zeros((1, LANES), np.float32)
    for h in range(N_FOX):
        p, hh = divmod(h, HEADS_PER_BLOCK)
        for j in range(3):
            k_map[j, SMALL_FGATE_LANE + h, p * LANES + 3 * hh + j] = -1.0
            q_map[j, SMALL_FGATE_LANE + h, FOX_FEAT_ROWS * h + 6 + j] = 1.0
            k_one[0, p * LANES + 6 + j] = 1.0
            q_one[0, FOX_FEAT_ROWS * h + 3 * hh + j] = 1.0
    return (jnp.asarray(k_map, BF16), jnp.asarray(q_map, BF16), jnp.asarray(k_one), jnp.asarray(q_one))


def _fox_decay_features(logits, b_ref, kmap_ref, qmap_ref, kone_ref, qone_ref, kf_ref, qf_ref, carry_ref):
    tc = logits.shape[0]
    z = logits + b_ref[...]
    log_f = jnp.minimum(z, 0.0) - _log1p_exp_neg_abs(z)
    row = lax.broadcasted_iota(I32, (tc, tc), 0)
    col = lax.broadcasted_iota(I32, (tc, tc), 1)
    tri = jnp.where(col <= row, 1.0, 0.0).astype(BF16)
    hi, mid, lo = _split3(log_f)
    yield
    cs = _dot(tri, hi) + _dot(tri, mid) + _dot(tri, lo) + carry_ref[...]
    carry_ref[...] = cs[tc - 1:tc, :]
    terms = _split3(cs)
    yield
    kf = kone_ref[...]
    qf = qone_ref[...]
    for j in range(3):
        kf = kf + _dot(terms[j], kmap_ref[j])
        qf = qf + _dot(terms[j], qmap_ref[j])
    kf_ref[...] = kf.astype(BF16)
    qf_ref[...] = qf.T.astype(BF16)


def _mixer_in_kernel(x_ref, g_ref, cos_ref, sin_ref, wb_ref, b_ref, kmap_ref, qmap_ref, kone_ref, qone_ref,
                     pb_ref, ps_ref, kf_ref, qf_ref, carry_ref, *, tiles_per_seq):
    @pl.when(pl.program_id(0) % tiles_per_seq == 0)
    def _():
        carry_ref[...] = jnp.zeros_like(carry_ref)

    h = _rms(x_ref[...], g_ref[...]).astype(BF16)
    cos = cos_ref[...]
    sin_signed = sin_ref[...]
    first_half = _first_half_lanes()
    n_cols = (N_PROJ_BLOCKS + 1) * LANES
    starts = list(range(0, n_cols, PROJ_CHUNK))
    decay_stages = iter(())
    for c0 in starts[-1:] + starts[:-1]:
        c1 = min(c0 + PROJ_CHUNK, n_cols)
        res = _dot(h, wb_ref[:, c0:c1])
        next(decay_stages, None)
        if c1 == n_cols:
            small = res[:, c1 - c0 - LANES:]
            ps_ref[...] = small
            decay_stages = _fox_decay_features(small, b_ref, kmap_ref, qmap_ref, kone_ref, qone_ref, kf_ref,
                                               qf_ref, carry_ref)
            next(decay_stages)
            c1 -= LANES
            res = res[:, :c1 - c0]
        if c0 < N_ROPE_BLOCKS * LANES:
            parts = []
            for b in range((c1 - c0) // LANES):
                xb = res[:, b * LANES:(b + 1) * LANES]
                partner = jnp.where(first_half, pltpu.roll(xb, LANES - HEAD_DIM // 2, 1),
                                    pltpu.roll(xb, HEAD_DIM // 2, 1))
                parts.append(xb * cos + partner * sin_signed)
            res = jnp.concatenate(parts, axis=1)
        pb_ref[:, c0:c1] = res.astype(BF16)


def _mixer_in(x2, layer, gains, cos, sin_signed, wb, bias_rows, feature_maps, batch, seq):
    m, d = x2.shape
    tm = _token_tile(seq, 512)
    nb = N_PROJ_BLOCKS * LANES
    tok = lambda w: pl.BlockSpec((tm, w), lambda i: (i, 0))
    return pl.pallas_call(
        functools.partial(_mixer_in_kernel, tiles_per_seq=seq // tm),
        grid=(m // tm,),
        in_specs=[tok(d), _layer((1, d), layer), tok(LANES), tok(LANES), _layer((d, nb + LANES), layer),
                  _layer((1, LANES), layer), _resident((3, LANES, FOX_W)), _resident((3, LANES, LANES)),
                  _resident((1, FOX_W)), _resident((1, LANES))],
        out_specs=[tok(nb), tok(LANES), tok(FOX_W), pl.BlockSpec((LANES, tm), lambda i: (0, i))],
        out_shape=[jax.ShapeDtypeStruct((m, nb), BF16), jax.ShapeDtypeStruct((m, LANES), F32),
                   jax.ShapeDtypeStruct((m, FOX_W), BF16), jax.ShapeDtypeStruct((LANES, m), BF16)],
        scratch_shapes=[pltpu.VMEM((1, LANES), F32)],
        compiler_params=_params(1),
        name="mixer_in",
    )(x2, gains, cos, sin_signed, wb, bias_rows, *feature_maps)


def _head_row_mask(hh):
    row = lax.broadcasted_iota(I32, (LANES, 1), 0)
    return (row < HEAD_DIM) if hh == 0 else (row >= HEAD_DIM)


def _transpose_bf16(x):
    return x.astype(F32).T.astype(BF16)


def _fill_transposed(src_ref, dst_ref):
    n_blocks, _, tk = dst_ref.shape
    for j in range(n_blocks):
        dst_ref[j] = _transpose_bf16(src_ref[j * tk:(j + 1) * tk, :])


V_ROWS = HEAD_DIM + 16


def _fill_transposed_with_ones(src_ref, dst_ref, n_heads):
    n_blocks, _, tk = dst_ref.shape
    ones_pad = jnp.where(lax.broadcasted_iota(I32, (V_ROWS - HEAD_DIM, tk), 0) == 0, 1.0, 0.0).astype(BF16)
    for j in range(n_blocks):
        v_t = _transpose_bf16(src_ref[j * tk:(j + 1) * tk, :])
        for h in range(n_heads):
            dst_ref[j, h * V_ROWS:h * V_ROWS + HEAD_DIM, :] = v_t[h * HEAD_DIM:(h + 1) * HEAD_DIM, :]
            dst_ref[j, h * V_ROWS + HEAD_DIM:(h + 1) * V_ROWS, :] = ones_pad


def _softmax_steps(carry, scores, values_t):
    stats = []
    for (m, _), s in zip(carry, scores):
        m_new = jnp.maximum(m, jnp.max(s, axis=0, keepdims=True))
        stats.append((m_new, jnp.exp(m - m_new), jnp.exp(s - m_new).astype(BF16)))
    return tuple((m_new, alpha * acc + _dot(v_t, p))
                 for (m_new, alpha, p), (_, acc), v_t in zip(stats, carry, values_t))


def _softmax_init(tq):
    return (jnp.full((1, tq), NEG_BIG, F32), jnp.zeros((V_ROWS, tq), F32))


def _softmax_result(state):
    _, acc = state
    return acc[:HEAD_DIM, :] / acc[HEAD_DIM:HEAD_DIM + 1, :]


def _flash_blocks(n, score_fn, value_fn, carry, s_ref):
    n_heads = len(carry)

    def load(slot):
        return [s_ref[slot, h] for h in range(n_heads)]

    def store(slot, kb):
        for h, s in enumerate(score_fn(kb)):
            s_ref[slot, h] = s

    odd = n % 2
    carry = lax.cond(odd == 1, lambda c: _softmax_steps(c, score_fn(0), value_fn(0)), lambda c: c, carry)

    @pl.when(n >= 2)
    def _():
        store(0, odd)

    def body(j, c):
        kb = odd + 2 * j
        store(1, kb + 1)
        c = _softmax_steps(c, load(0), value_fn(kb))
        store(0, jnp.minimum(kb + 2, n - 1))
        return _softmax_steps(c, load(1), value_fn(kb + 1))

    return lax.fori_loop(0, n // 2, body, carry)


def _heads_to_token_rows(out_t):
    return jnp.concatenate(out_t, axis=0).T.astype(BF16)


def _fox_kernel(q_ref, k_ref, v_ref, kf_ref, qf_ref, o_ref, vt_ref, s_ref):
    tq = q_ref.shape[0]
    tk = vt_ref.shape[2]
    i = pl.program_id(1)

    @pl.when(i == 0)
    def _():
        _fill_transposed_with_ones(v_ref, vt_ref, N_FOX)

    key_pos = lax.broadcasted_iota(I32, (tk, tq), 0)
    qry_pos = lax.broadcasted_iota(I32, (tk, tq), 1)
    causal = [key_pos + d * tk <= qry_pos for d in range(2)]
    pad = jnp.zeros((LANES - FOX_FEAT_ROWS, tq), BF16)
    qx = []
    for p in range(FOX_W // LANES):
        q_t = q_ref[:, p * LANES:(p + 1) * LANES].astype(F32).T
        for hh in range(HEADS_PER_BLOCK):
            qx.append(jnp.concatenate([jnp.where(_head_row_mask(hh), q_t, 0.0).astype(BF16),
                                       qf_ref[(p * HEADS_PER_BLOCK + hh) * FOX_FEAT_ROWS:
                                              (p * HEADS_PER_BLOCK + hh + 1) * FOX_FEAT_ROWS, :], pad], axis=0))

    def store(slot, kb):
        k0 = pl.multiple_of(kb * tk, tk)
        for h in range(N_FOX):
            p = h // HEADS_PER_BLOCK
            kx = jnp.concatenate([k_ref[pl.ds(k0, tk), p * LANES:(p + 1) * LANES],
                                  kf_ref[pl.ds(k0, tk), p * LANES:(p + 1) * LANES]], axis=1)
            s_ref[slot, h] = _dot(kx, qx[h])

    def step(carry, slot, kb, mask=None):
        scores = [s_ref[slot, h] for h in range(N_FOX)]
        if mask is not None:
            scores = [jnp.where(mask, s, NEG_BIG) for s in scores]
        return _softmax_steps(carry, scores, [vt_ref[kb, h * V_ROWS:(h + 1) * V_ROWS, :] for h in range(N_FOX)])

    def trip(j, carry):
        kb = 2 * j
        store(1, kb + 1)
        carry = step(carry, 0, kb)
        store(0, kb + 2)
        return step(carry, 1, kb + 1)

    store(0, 0)
    carry = lax.fori_loop(0, i, trip, tuple(_softmax_init(tq) for _ in range(N_FOX)))
    kb = 2 * i
    store(1, kb + 1)
    carry = step(carry, 0, kb, causal[0])
    carry = step(carry, 1, kb + 1, causal[1])
    for p in range(FOX_W // LANES):
        o_ref[:, p * LANES:(p + 1) * LANES] = _heads_to_token_rows(
            [_softmax_result(st) for st in carry[p * HEADS_PER_BLOCK:(p + 1) * HEADS_PER_BLOCK]])


def _fox_attention(pb, k_side, q_side, batch, seq):
    tq = min(512, seq)
    tk = tq // 2
    assert seq % tq == 0
    nq = seq // tq
    m = batch * seq
    return pl.pallas_call(
        _fox_kernel,
        grid=(batch, nq),
        in_specs=[pl.BlockSpec((tq, FOX_W), lambda b, i: (b * nq + i, BLK_FOX_Q // 3)),
                  pl.BlockSpec((seq, FOX_W), lambda b, i: (b, BLK_FOX_K // 3)),
                  pl.BlockSpec((seq, FOX_W), lambda b, i: (b, BLK_FOX_V // 3)),
                  pl.BlockSpec((seq, FOX_W), lambda b, i: (b, 0)),
                  pl.BlockSpec((LANES, tq), lambda b, i: (0, b * nq + i))],
        out_specs=pl.BlockSpec((tq, FOX_W), lambda b, i: (b * nq + i, 0)),
        out_shape=jax.ShapeDtypeStruct((m, FOX_W), BF16),
        scratch_shapes=[pltpu.VMEM((seq // tk, N_FOX * V_ROWS, tk), BF16), pltpu.VMEM((2, N_FOX, tk, tq), F32)],
        compiler_params=_params(2),
        name="fox_attention",
    )(pb, pb, pb, k_side, q_side)


def _sb_kernel(q_ref, k_ref, v_ref, o_ref, vt_ref):
    tq = q_ref.shape[0]
    tk = vt_ref.shape[2]
    i = pl.program_id(1)

    @pl.when(i == 0)
    def _():
        _fill_transposed(v_ref, vt_ref)

    heads = range(N_SB)
    q_t = [q_ref[:, p * LANES:(p + 1) * LANES].astype(F32).T for p in range(PAIR_W // LANES)]
    qh = [jnp.where(_head_row_mask(h % HEADS_PER_BLOCK), q_t[h // HEADS_PER_BLOCK], 0.0).astype(BF16) for h in heads]
    r = lax.broadcasted_iota(I32, (tk, tk), 0)
    c = lax.broadcasted_iota(I32, (tk, tk), 1)
    upper = jnp.where(c > r, 1.0, 0.0).astype(BF16)
    key_pos = lax.broadcasted_iota(I32, (tk, tq), 0)
    qry_pos = lax.broadcasted_iota(I32, (tk, tq), 1) + i * tq

    def process(kb, state, masked):
        k0 = pl.multiple_of(kb * tk, tk)
        strict = key_pos + k0 < qry_pos
        z = [_dot(k_ref[pl.ds(k0, tk), (h // HEADS_PER_BLOCK) * LANES:(h // HEADS_PER_BLOCK + 1) * LANES], qh[h])
             for h in heads]
        log_beta, log_1m = [], []
        for zh in z:
            l1m = jnp.minimum(-zh, 0.0) - jnp.log(1.0 + jnp.exp(-jnp.abs(zh)))
            log_beta.append(l1m + zh)
            log_1m.append(jnp.where(strict, l1m, 0.0) if masked else l1m)
        after = []
        for h in heads:
            hi, mid, lo = _split3(log_1m[h])
            after.append(_dot(upper, hi) + _dot(upper, mid) + _dot(upper, lo) + state[h][0])
        weights = []
        for h in heads:
            a = jnp.exp(log_beta[h] + after[h])
            weights.append((jnp.where(strict, a, 0.0) if masked else a).astype(BF16))
        return tuple((state[h][0] + jnp.sum(log_1m[h], axis=0, keepdims=True),
                      state[h][1] + _dot(vt_ref[kb, h * HEAD_DIM:(h + 1) * HEAD_DIM, :], weights[h]))
                     for h in heads)

    def run_max(state):
        m = state[0][0]
        for run, _ in state[1:]:
            m = jnp.maximum(m, run)
        return jnp.max(m)

    state = tuple((jnp.zeros((1, tq), F32), jnp.zeros((HEAD_DIM, tq), F32)) for _ in heads)
    n_diag = tq // tk
    for d in range(n_diag):
        state = process((i + 1) * n_diag - 1 - d, state, True)

    def cond(carry):
        kb, worst, _ = carry
        return jnp.logical_and(kb >= 0, worst > SB_CUTOFF)

    def body(carry):
        kb, _, st = carry
        st = process(kb, st, False)
        return kb - 1, run_max(st), st

    _, _, state = lax.while_loop(cond, body, (i * n_diag - 1, run_max(state), state))
    for p in range(PAIR_W // LANES):
        outs = [state[h][1] for h in heads if h // HEADS_PER_BLOCK == p]
        outs += [jnp.zeros((HEAD_DIM, tq), F32)] * (HEADS_PER_BLOCK - len(outs))
        o_ref[:, p * LANES:(p + 1) * LANES] = _heads_to_token_rows(outs)


def _sb_attention(pb, batch, seq):
    tq = min(256, seq)
    tk = min(256, seq)
    assert seq % tq == 0 and tq % tk == 0
    nq = seq // tq
    m = batch * seq
    return pl.pallas_call(
        _sb_kernel,
        grid=(batch, nq),
        in_specs=[pl.BlockSpec((tq, PAIR_W), lambda b, i: (b * nq + i, BLK_SB_Q // 3)),
                  pl.BlockSpec((seq, PAIR_W), lambda b, i: (b, BLK_SB_K // 3)),
                  pl.BlockSpec((seq, PAIR_W), lambda b, i: (b, BLK_SB_V // 3))],
        out_specs=pl.BlockSpec((tq, PAIR_W), lambda b, i: (b * nq + i, 0)),
        out_shape=jax.ShapeDtypeStruct((m, PAIR_W), BF16),
        scratch_shapes=[pltpu.VMEM((seq // tk, PAIR_W, tk), BF16)],
        compiler_params=_params(2),
        name="sb_attention",
    )(pb, pb, pb)


HALF_BITS = 16
HALF_OFFSET = 1 << (HALF_BITS - 1)


def _dsa_kernel(q_ref, iq_ref, w_ref, k_ref, v_ref, ka_ref, kb_ref, o_ref, key_ref, hi_ref, lo_ref, bias_ref,
                vt_ref, s_ref, *, topk, seq):
    t = q_ref.shape[0]
    i = pl.program_id(1)

    @pl.when(i == 0)
    def _():
        _fill_transposed_with_ones(v_ref, vt_ref, N_DSA)

    n_kb = i + 1
    kf = float(topk)
    imin = jnp.int32(INT_MIN)
    key_pos = lax.broadcasted_iota(I32, (t, t), 0)
    qry_pos = lax.broadcasted_iota(I32, (t, t), 1) + i * t

    iq_t = [_transpose_bf16(iq_ref[:, g * LANES:(g + 1) * LANES]) for g in range(N_IDX_HEADS // 2)]
    w_t = (w_ref[...] * (N_IDX_HEADS ** -0.5) * (IDX_DIM ** -0.5)).T

    def index_block(kb):
        k0 = pl.multiple_of(kb * t, t)
        ka = ka_ref[pl.ds(k0, t), :]
        kb_ = kb_ref[pl.ds(k0, t), :]
        score = jnp.zeros((t, t), F32)
        for g in range(N_IDX_HEADS // 2):
            lo = SMALL_IDXW_LANE + 2 * g
            score = score + jnp.maximum(_dot(ka, iq_t[g]), 0.0) * w_t[lo:lo + 1, :]
            score = score + jnp.maximum(_dot(kb_, iq_t[g]), 0.0) * w_t[lo + 1:lo + 2, :]
        bits = lax.bitcast_convert_type(score + 0.0, I32)
        key = jnp.where(bits < 0, bits ^ jnp.int32(0x7FFFFFFF), bits)
        key = jnp.where(key_pos + k0 <= qry_pos, key, imin)
        key_ref[pl.ds(k0, t), :] = key
        hi_ref[pl.ds(k0, t), :] = lax.shift_right_arithmetic(key, HALF_BITS).astype(jnp.int16)
        lo_ref[pl.ds(k0, t), :] = ((key & (2 * HALF_OFFSET - 1)) - HALF_OFFSET).astype(jnp.int16)

    odd = n_kb % 2

    @pl.when(odd == 1)
    def _():
        index_block(0)

    def index_pair(j, _):
        index_block(odd + 2 * j)
        index_block(odd + 2 * j + 1)
        return 0

    lax.fori_loop(0, n_kb // 2, index_pair, 0)

    masked16 = jnp.full((t, t), -HALF_OFFSET, jnp.int16)

    @pl.when(odd == 1)
    def _():
        k0 = pl.multiple_of(n_kb * t, t)
        hi_ref[pl.ds(k0, t), :] = masked16
        lo_ref[pl.ds(k0, t), :] = masked16

    n_scan = (n_kb + 1) // 2
    scan_rows = 2 * t

    def count16(ref, pred):
        rows = 16
        def body(trip, acc):
            k0 = pl.multiple_of(trip * scan_rows, scan_rows)
            ind = pred(ref[pl.ds(k0, scan_rows), :])
            parts = [ind[r:r + rows, :] for r in range(0, scan_rows, rows)]
            while len(parts) > 1:
                parts = [parts[j] + parts[j + 1] for j in range(0, len(parts), 2)]
            return acc + parts[0]
        acc = lax.fori_loop(0, n_scan, body, jnp.zeros((rows, t), jnp.int16))
        return jnp.sum(acc.astype(F32), axis=0, keepdims=True)

    one16, zero16 = jnp.int16(1), jnp.int16(0)

    def kth_half(ref, need, n_all):
        def bit_step(it, state):
            prefix, cnt_prefix = state
            cand_u = prefix | lax.shift_left(jnp.int32(1), HALF_BITS - 1 - it)
            cand = (cand_u - HALF_OFFSET).astype(jnp.int16)
            cnt = count16(ref, lambda blk: jnp.where(blk >= cand, one16, zero16))
            take = cnt >= need
            return jnp.where(take, cand_u, prefix), jnp.where(take, cnt, cnt_prefix)
        prefix, cnt = lax.fori_loop(0, HALF_BITS, bit_step, (jnp.zeros((1, t), I32), n_all))
        return prefix - HALF_OFFSET, cnt


    n_scanned = (n_scan * scan_rows).astype(F32)
    kth_hi, n_ge_hi = kth_half(hi_ref, kf, jnp.full((1, t), n_scanned, F32))
    kth_hi16 = kth_hi.astype(jnp.int16)

    def above_and_bucket(trip, acc):
        rows = 16
        k0 = pl.multiple_of(trip * scan_rows, scan_rows)
        hi = hi_ref[pl.ds(k0, scan_rows), :]
        lo_ref[pl.ds(k0, scan_rows), :] = jnp.where(hi == kth_hi16, lo_ref[pl.ds(k0, scan_rows), :],
                                                    jnp.int16(-HALF_OFFSET))
        ind = jnp.where(hi > kth_hi16, one16, zero16)
        parts = [ind[r:r + rows, :] for r in range(0, scan_rows, rows)]
        while len(parts) > 1:
            parts = [parts[j] + parts[j + 1] for j in range(0, len(parts), 2)]
        return acc + parts[0]

    n_gt_hi = jnp.sum(lax.fori_loop(0, n_scan, above_and_bucket, jnp.zeros((16, t), jnp.int16)).astype(F32),
                      axis=0, keepdims=True)
    need_lo = kf - n_gt_hi
    kth_lo, n_ge_lo = kth_half(lo_ref, need_lo, n_ge_hi - n_gt_hi)
    kth = lax.shift_left(kth_hi, HALF_BITS) + (kth_lo + HALF_OFFSET)

    excess = jnp.where(kth == imin, float(2 * seq), n_ge_lo - need_lo)
    any_excess = jnp.max(excess) > 0.0

    @pl.when(jnp.logical_not(any_excess))
    def _():
        def select(kb, _):
            k0 = pl.multiple_of(kb * t, t)
            bias_ref[pl.ds(k0, t), :] = jnp.where(key_ref[pl.ds(k0, t), :] >= kth, 0.0, NEG_BIG)
            return 0

        lax.fori_loop(0, n_kb, select, 0)

    @pl.when(any_excess)
    def _():
        r = lax.broadcasted_iota(I32, (t, t), 0)
        c = lax.broadcasted_iota(I32, (t, t), 1)
        later = jnp.where(c > r, 1.0, 0.0).astype(BF16)

        def tied_in(kb):
            kblk = key_ref[pl.ds(pl.multiple_of(kb * t, t), t), :]
            tied01 = jnp.where(kblk == kth, 1.0, 0.0).astype(BF16)
            return tied01, _dot(later, tied01)

        def write_bias(kb, tied01, after):
            k0 = pl.multiple_of(kb * t, t)
            kblk = key_ref[pl.ds(k0, t), :]
            tie_bias = jnp.where(kblk == kth, jnp.where(after >= excess, 0.0, NEG_BIG), NEG_BIG)
            bias_ref[pl.ds(k0, t), :] = jnp.where(kblk > kth, 0.0, tie_bias)
            return after[0:1, :] + tied01[0:1, :].astype(F32)

        def select_pair(trip, tied_after):
            kb = n_kb - 1 - 2 * trip
            tied_a, local_a = tied_in(kb)
            tied_b, local_b = tied_in(kb - 1)
            tied_after = write_bias(kb, tied_a, local_a + tied_after)
            return write_bias(kb - 1, tied_b, local_b + tied_after)

        tied_after = lax.fori_loop(0, n_kb // 2, select_pair, jnp.zeros((1, t), F32))

        @pl.when(odd == 1)
        def _():
            tied_0, local_0 = tied_in(0)
            write_bias(0, tied_0, local_0 + tied_after)

    q_t = [q_ref[:, p * LANES:(p + 1) * LANES].astype(F32).T for p in range(PAIR_W // LANES)]
    qh = [jnp.where(_head_row_mask(h % HEADS_PER_BLOCK), q_t[h // HEADS_PER_BLOCK], 0.0).astype(BF16)
          for h in range(N_DSA)]

    def scores(kb):
        k0 = pl.multiple_of(kb * t, t)
        bias = bias_ref[pl.ds(k0, t), :]
        return [_dot(k_ref[pl.ds(k0, t), (h // HEADS_PER_BLOCK) * LANES:(h // HEADS_PER_BLOCK + 1) * LANES], qh[h])
                + bias for h in range(N_DSA)]

    def values_t(kb):
        return [vt_ref[kb, h * V_ROWS:(h + 1) * V_ROWS, :] for h in range(N_DSA)]

    carry = _flash_blocks(n_kb, scores, values_t, tuple(_softmax_init(t) for _ in range(N_DSA)), s_ref)
    outs = [_softmax_result(st) for st in carry]
    outs += [jnp.zeros((HEAD_DIM, t), F32)] * (PAIR_W // HEAD_DIM - N_DSA)
    for p in range(PAIR_W // LANES):
        o_ref[:, p * LANES:(p + 1) * LANES] = _heads_to_token_rows(
            outs[p * HEADS_PER_BLOCK:(p + 1) * HEADS_PER_BLOCK])


def _dsa_attention(pb, ps, batch, seq):
    t = min(256, seq)
    assert seq % t == 0
    nq = seq // t
    m = batch * seq
    topk = min(TOPK_MAX, seq // 4)
    return pl.pallas_call(
        functools.partial(_dsa_kernel, topk=topk, seq=seq),
        grid=(batch, nq),
        in_specs=[pl.BlockSpec((t, PAIR_W), lambda b, i: (b * nq + i, BLK_DSA_Q // 3)),
                  pl.BlockSpec((t, 4 * LANES), lambda b, i: (b * nq + i, BLK_IDX_Q // 4)),
                  pl.BlockSpec((t, LANES), lambda b, i: (b * nq + i, 0)),
                  pl.BlockSpec((seq, PAIR_W), lambda b, i: (b, BLK_DSA_K // 3)),
                  pl.BlockSpec((seq, PAIR_W), lambda b, i: (b, BLK_DSA_V // 3)),
                  pl.BlockSpec((seq, LANES), lambda b, i: (b, BLK_IDX_KA)),
                  pl.BlockSpec((seq, LANES), lambda b, i: (b, BLK_IDX_KB))],
        out_specs=pl.BlockSpec((t, PAIR_W), lambda b, i: (b * nq + i, 0)),
        out_shape=jax.ShapeDtypeStruct((m, PAIR_W), BF16),
        scratch_shapes=[pltpu.VMEM((seq, t), I32), pltpu.VMEM((seq + t, t), jnp.int16),
                        pltpu.VMEM((seq + t, t), jnp.int16),
                        pltpu.VMEM((seq, t), F32),
                        pltpu.VMEM((nq, N_DSA * V_ROWS, t), BF16), pltpu.VMEM((2, N_DSA, t, t), F32)],
        compiler_params=_params(2),
        name="dsa_attention",
    )(pb, pb, ps, pb, pb, pb, pb)


def _mixer_out_cross_kernel(x_ref, gain_ref, of_ref, od_ref, os_ref, wg_ref, wf_ref, wd_ref, ws_ref, wo_ref,
                            ca_gain_ref, ca_wq_ref, kv_ref, ca_wo_ref, o_ref):
    d = x_ref.shape[1]
    x = x_ref[...]
    h = _rms(x, gain_ref[...]).astype(BF16)
    merged = None
    for b, (o_b, w_b) in enumerate(((of_ref, wf_ref), (od_ref, wd_ref), (os_ref, ws_ref))):
        term = jax.nn.sigmoid(_dot(h, wg_ref[:, b * d:(b + 1) * d])) * _dot(o_b[...], w_b[...])
        merged = term if merged is None else merged + term
    x = x + _dot(merged.astype(BF16), wo_ref[...])

    q = _dot(_rms(x, ca_gain_ref[...]).astype(BF16), ca_wq_ref[...])
    scale = CA_HEAD_DIM ** -0.5
    outs = []
    for hd in range(N_CA_HEADS):
        qh = q[:, hd * CA_HEAD_DIM:(hd + 1) * CA_HEAD_DIM].astype(BF16)
        logits = _dot_nt(qh, kv_ref[:, hd * CA_HEAD_DIM:(hd + 1) * CA_HEAD_DIM]) * scale
        e = jnp.exp(logits - jnp.max(logits, axis=1, keepdims=True))
        pv = _dot(e.astype(BF16), kv_ref[:, CA_W + hd * CA_HEAD_DIM:CA_W + (hd + 1) * CA_HEAD_DIM])
        outs.append(pv / jnp.sum(e, axis=1, keepdims=True))
    o_ref[...] = x + _dot(jnp.concatenate(outs, axis=1).astype(BF16), ca_wo_ref[...])


def _mixer_out_cross(x2, layer, mix, o_fox, o_dsa, o_sb, kv, batch, seq):
    m, d = x2.shape
    tm = _token_tile(seq, 1024)
    n_mem = kv.shape[0] // batch
    per_batch = seq // tm
    row = lambda w: pl.BlockSpec((tm, w), lambda i: (i, 0))
    lay = lambda *shape: _layer(shape, layer)
    return pl.pallas_call(
        _mixer_out_cross_kernel,
        grid=(m // tm,),
        in_specs=[row(d), lay(1, d), row(FOX_W), row(PAIR_W), row(PAIR_W), lay(d, N_BRANCH * d),
                  lay(FOX_W, d), lay(PAIR_W, d), lay(PAIR_W, d), lay(d, d), lay(1, d), lay(d, CA_W),
                  pl.BlockSpec((n_mem, 2 * CA_W), lambda i: (i // per_batch, 0)), lay(CA_W, d)],
        out_specs=row(d),
        out_shape=jax.ShapeDtypeStruct((m, d), F32),
        compiler_params=_params(1),
        name="mixer_out_cross",
    )(x2, mix["mix_gain"], o_fox, o_dsa, o_sb, mix["w_gates"], mix["w_fox_out"], mix["w_dsa_out"], mix["w_sb_out"],
      mix["w_out"], mix["ca_gain"], mix["ca_wq"], kv, mix["ca_wo"])


def _norm_matmul_kernel(x_ref, g_ref, w_ref, o_ref):
    o_ref[...] = _dot(_rms(x_ref[...], g_ref[...]).astype(BF16), w_ref[...]).astype(o_ref.dtype)


def _norm_matmul(x2, layer, gains, w, out_dtype):
    m, d = x2.shape
    n = w.shape[-1]
    tm = _token_tile(m, 512)
    return pl.pallas_call(
        _norm_matmul_kernel,
        grid=(m // tm,),
        in_specs=[pl.BlockSpec((tm, d), lambda i: (i, 0)), _layer((1, d), layer), _layer((d, n), layer)],
        out_specs=pl.BlockSpec((tm, n), lambda i: (i, 0)),
        out_shape=jax.ShapeDtypeStruct((m, n), out_dtype),
        compiler_params=_params(1),
        name="norm_matmul",
    )(x2, gains, w)


def _pad_last(w, n):
    return jnp.pad(w, [(0, 0)] * (w.ndim - 1) + [(0, n - w.shape[-1])])


def _pad_rows(w, n):
    return jnp.pad(w, [(0, 0)] * (w.ndim - 2) + [(0, n - w.shape[-2]), (0, 0)])


def _mixer_weights(w_in):
    offs = np.cumsum((0,) + IN_SPLITS)
    w_in = w_in.astype(BF16)
    part = lambda j: w_in[..., offs[j]:offs[j + 1]]
    thirds = lambda w: jnp.split(w, 3, axis=-1)
    scale = HEAD_DIM ** -0.5
    fq, fk, fv = thirds(part(0))
    dq, dk, dv = thirds(part(2))
    sq, sk, sv = thirds(part(6))
    iq, ik, iw = part(3), part(4), part(5)
    zk = jnp.zeros_like(ik)
    small = jnp.concatenate([part(1), jnp.zeros_like(w_in[..., :SMALL_IDXW_LANE - N_FOX]), iw], axis=-1)
    wb = jnp.concatenate(
        [_pad_last(dq * scale, PAIR_W), _pad_last(dk, PAIR_W),
         jnp.concatenate([ik, zk], axis=-1), jnp.concatenate([zk, ik], axis=-1), iq,
         fq * scale, fk, fv, _pad_last(dv, PAIR_W),
         _pad_last(sq * scale, PAIR_W), _pad_last(sk, PAIR_W), _pad_last(sv, PAIR_W),
         _pad_last(small, LANES)], axis=-1)
    return wb, part(7)


def _stacked_parameters(mix_norm, w_in, b_fgate, w_fox_out, w_dsa_out, w_sb_out, w_out, ca_norm, ca_w_q, ca_w_o):
    depth, d = mix_norm.shape
    w_proj, w_gates = _mixer_weights(w_in)
    return {
        "mix_gain": mix_norm.reshape(depth, 1, d),
        "w_proj": w_proj,
        "w_gates": w_gates,
        "fgate_bias": _pad_last(b_fgate.astype(F32), LANES).reshape(depth, 1, LANES),
        "w_fox_out": w_fox_out.astype(BF16),
        "w_dsa_out": _pad_rows(w_dsa_out, PAIR_W).astype(BF16),
        "w_sb_out": _pad_rows(w_sb_out, PAIR_W).astype(BF16),
        "w_out": w_out.astype(BF16),
        "ca_gain": ca_norm.reshape(depth, 1, d),
        "ca_wq": ca_w_q.astype(BF16),
        "ca_wo": ca_w_o.astype(BF16),
    }


def _mixer_and_cross(x2, layer, mix, cos, sin_signed, feature_maps, kv, batch, seq):
    pb, ps, k_side, q_side = _mixer_in(x2, layer, mix["mix_gain"], cos, sin_signed, mix["w_proj"],
                                       mix["fgate_bias"], feature_maps, batch, seq)
    o_fox = _fox_attention(pb, k_side, q_side, batch, seq)
    o_dsa = _dsa_attention(pb, ps, batch, seq)
    o_sb = _sb_attention(pb, batch, seq)
    return _mixer_out_cross(x2, layer, mix, o_fox, o_dsa, o_sb, kv, batch, seq)


def kernel(x, mem, positions, ffn1_norm, ffn1_w_gu, ffn1_w_down, mix_norm, w_in, b_fgate, w_fox_out, w_dsa_out,
           w_sb_out, w_out, ca_norm, mem_norm, ca_w_q, ca_w_kv, ca_w_o, ffn2_norm, ffn2_w_gu, ffn2_w_down,
           final_norm):
    batch, seq, d = x.shape
    depth = ffn1_norm.shape[0]
    x2 = x.reshape(batch * seq, d)
    mem2 = mem.reshape(-1, d)
    pos2 = positions.reshape(batch * seq, 1).astype(I32)
    half = HEAD_DIM // 2
    inv_freq = jnp.power(ROPE_THETA, -jnp.arange(half, dtype=F32) * (2.0 / HEAD_DIM))
    invf = jnp.tile(inv_freq, LANES // half).reshape(1, LANES)
    cos, sin_signed = _rope_tables(pos2, invf)
    feature_maps = _fox_feature_maps()
    mix = _stacked_parameters(mix_norm, w_in, b_fgate, w_fox_out, w_dsa_out, w_sb_out, w_out, ca_norm, ca_w_q,
                              ca_w_o)
    gain = lambda g: g.reshape(depth, 1, d)
    ffn1 = (gain(ffn1_norm), ffn1_w_gu.astype(BF16), ffn1_w_down.astype(BF16))
    ffn2 = (gain(ffn2_norm), ffn2_w_gu.astype(BF16), ffn2_w_down.astype(BF16))
    mem_gain, w_kv = gain(mem_norm), ca_w_kv.astype(BF16)
    for l in range(depth):
        x2 = _ffn(x2, l, *ffn1)
        kv = _norm_matmul(mem2, l, mem_gain, w_kv, BF16)
        x2 = _mixer_and_cross(x2, l, mix, cos, sin_signed, feature_maps, kv, batch, seq)
        x2 = _ffn(x2, l, *ffn2, final_gain=final_norm if l == depth - 1 else None)
    return x2.reshape(batch, seq, d)
```
